```python
import jax, jax.numpy as jnp
from jax import lax
import numpy as np

D_MODEL = 2048
BATCH = 4
SEQ = 2048
DEPTH = 1

N_META = 16
CHUNK = 64
CONV_K = 4
NORM_EPS = 1e-6
L2_EPS = 1e-6

SSD_HEADS = 32
SSD_HEAD_DIM = 64
SSD_D_INNER = SSD_HEADS * SSD_HEAD_DIM
SSD_GROUPS = 4
SSD_STATE = 128
SSD_CONV_DIM = SSD_D_INNER + 2 * SSD_GROUPS * SSD_STATE

GDN_HEADS = 16
GDN_DK = 128
GDN_DV = 128
GDN_DQK = GDN_HEADS * GDN_DK
GDN_DVW = GDN_HEADS * GDN_DV
GDN_CONV_DIM = 2 * GDN_DQK + GDN_DVW

N_BRANCH = 2

PEER_HEADS = 8
PEER_N_KEYS = 128
PEER_N_EXPERTS = PEER_N_KEYS * PEER_N_KEYS
PEER_TOPK = 16
PEER_D_KEY = 128
PEER_POS_BLOCK = 16

IN_WIDTHS = (SSD_D_INNER, SSD_CONV_DIM, SSD_HEADS, GDN_CONV_DIM, GDN_DVW, GDN_HEADS, GDN_HEADS, N_BRANCH * D_MODEL)
IN_DIM = sum(IN_WIDTHS)
IN_SPLITS = tuple(sum(IN_WIDTHS[:i + 1]) for i in range(len(IN_WIDTHS) - 1))

kernel_name = "hybrid_ssd_gdn_peer_block"


def _rms(x):
    xf = x.astype(jnp.float32)
    return (xf * lax.rsqrt(jnp.mean(xf * xf, axis=-1, keepdims=True) + NORM_EPS)).astype(x.dtype)


def _rmsnorm(x, w):
    return _rms(x) * w


def _group_rmsnorm(y, w):
    shp = y.shape
    yg = _rms(y.reshape(shp[:-1] + (SSD_GROUPS, shp[-1] // SSD_GROUPS)))
    return yg.reshape(shp) * w


def _l2norm(x):
    xf = x.astype(jnp.float32)
    return (xf * lax.rsqrt(jnp.sum(xf * xf, axis=-1, keepdims=True) + L2_EPS)).astype(x.dtype)


def _causal_dwconv(x, w):
    return lax.conv_general_dilated(
        x, w[:, None, :].astype(x.dtype), window_strides=(1,), padding=((CONV_K - 1, 0),),
        dimension_numbers=("NWC", "WIO", "NWC"), feature_group_count=x.shape[-1])


def _decay_matrix(cs):
    q = cs.shape[-1]
    mask = jnp.tril(jnp.ones((q, q), dtype=bool))
    diff = cs[..., :, None] - cs[..., None, :]
    return jnp.exp(jnp.where(mask, diff, -jnp.inf))


def _ssd_chunked(x, dt, A, Bm, Cm):
    b, lp = x.shape[:2]
    nc = lp // CHUNK
    r = SSD_HEADS // SSD_GROUPS
    cdt = x.dtype
    xc = (x * dt[..., None].astype(cdt)).reshape(b, nc, CHUNK, SSD_GROUPS, r, SSD_HEAD_DIM)
    bc = Bm.reshape(b, nc, CHUNK, SSD_GROUPS, SSD_STATE)
    cc = Cm.reshape(b, nc, CHUNK, SSD_GROUPS, SSD_STATE)
    a = (dt * A).reshape(b, nc, CHUNK, SSD_GROUPS, r).transpose(0, 3, 4, 1, 2)
    a_cs = jnp.cumsum(a, axis=-1)
    cb = jnp.einsum("bclgn,bcsgn->bgcls", cc, bc)
    w_intra = cb[:, :, None] * _decay_matrix(a_cs).astype(cdt)
    y_diag = jnp.einsum("bgrcls,bcsgrp->bclgrp", w_intra, xc)
    to_end = jnp.exp(a_cs[..., -1:] - a_cs).astype(cdt).transpose(0, 3, 4, 1, 2)
    states = jnp.einsum("bcsgn,bcsgrp->bcgrpn", bc, xc * to_end[..., None])
    chunk_decay = jnp.exp(a_cs[..., -1]).astype(cdt)

    def step(s, inp):
        st, dec = inp
        return s * dec[..., None, None] + st, s

    s0 = jnp.zeros(states.shape[:1] + states.shape[2:], states.dtype)
    _, prev = lax.scan(step, s0, (jnp.moveaxis(states, 1, 0), jnp.moveaxis(chunk_decay, 3, 0)))
    prev = jnp.moveaxis(prev, 0, 1)
    from_start = jnp.exp(a_cs).astype(cdt).transpose(0, 3, 4, 1, 2)
    y_off = jnp.einsum("bclgn,bcgrpn->bclgrp", cc, prev) * from_start[..., None]
    return (y_diag + y_off).reshape(b, lp, SSD_HEADS, SSD_HEAD_DIM)


def _gated_delta_chunked(q, k, v, g, beta):
    b, lp = q.shape[:2]
    nc = lp // CHUNK
    cdt = q.dtype

    def chunks(t):
        return jnp.moveaxis(t.reshape(b, nc, CHUNK, GDN_HEADS, t.shape[-1]), 3, 1)

    q = chunks(q) * (GDN_DK ** -0.5)
    k = chunks(k)
    v = chunks(v)
    g_cs = jnp.cumsum(jnp.moveaxis(g.reshape(b, nc, CHUNK, GDN_HEADS), 3, 1), axis=-1)
    beta = jnp.moveaxis(beta.reshape(b, nc, CHUNK, GDN_HEADS), 3, 1)[..., None].astype(cdt)
    decay = _decay_matrix(g_cs)
    kb = k * beta
    strict = jnp.tril(jnp.ones((CHUNK, CHUNK), dtype=bool), -1)
    lm = jnp.where(strict, jnp.einsum("bhcid,bhcjd->bhcij", kb, k).astype(jnp.float32) * decay, 0.0)
    eye = jnp.eye(CHUNK, dtype=jnp.float32)
    tinv = lax.linalg.triangular_solve(lm + eye, jnp.broadcast_to(eye, lm.shape),
                                       left_side=True, lower=True, unit_diagonal=True)
    u = jnp.einsum("bhcij,bhcje->bhcie", tinv, (v * beta).astype(jnp.float32)).astype(cdt)
    w = jnp.einsum("bhcij,bhcjd->bhcid", tinv,
                   kb.astype(jnp.float32) * jnp.exp(g_cs)[..., None]).astype(cdt)
    qk = (jnp.einsum("bhcid,bhcjd->bhcij", q, k).astype(jnp.float32) * decay).astype(cdt)
    qg = (q.astype(jnp.float32) * jnp.exp(g_cs)[..., None]).astype(cdt)
    kd = (k.astype(jnp.float32) * jnp.exp(g_cs[..., -1:] - g_cs)[..., None]).astype(cdt)
    dl = jnp.exp(g_cs[..., -1]).astype(cdt)

    def step(s, inp):
        qg_c, w_c, u_c, kd_c, qk_c, dl_c = inp
        v_new = u_c - jnp.einsum("bhid,bhde->bhie", w_c, s)
        o = jnp.einsum("bhid,bhde->bhie", qg_c, s) + jnp.einsum("bhij,bhje->bhie", qk_c, v_new)
        s = s * dl_c[..., None, None] + jnp.einsum("bhid,bhie->bhde", kd_c, v_new)
        return s, o

    s0 = jnp.zeros((b, GDN_HEADS, GDN_DK, GDN_DV), cdt)
    xs = tuple(jnp.moveaxis(t, 2, 0) for t in (qg, w, u, kd, qk, dl))
    _, o = lax.scan(step, s0, xs)
    return jnp.transpose(o, (1, 0, 3, 2, 4)).reshape(b, lp, GDN_HEADS, GDN_DV)


def _hybrid_mixer(h, w_in, ssd_conv_w, ssd_conv_b, ssd_dt_bias, ssd_A_log, ssd_D, ssd_norm_w,
                  gdn_conv_w, gdn_dt_bias, gdn_A_log, gdn_norm_w, w_branch_ssd, w_branch_gdn, w_out):
    b, l, _ = h.shape
    pad = CHUNK - N_META
    lp = l + pad
    proj = h @ w_in
    z, xbc, dt_raw, qkv, gate, a_raw, b_raw, merge_logits = jnp.split(proj, IN_SPLITS, axis=-1)
    valid = (jnp.arange(lp) >= pad).astype(h.dtype)[None, :, None]

    def front_pad(t):
        return jnp.pad(t, ((0, 0), (pad, 0), (0, 0)))

    xbc = jax.nn.silu(_causal_dwconv(front_pad(xbc), ssd_conv_w) + ssd_conv_b) * valid
    xs, bs, cs = jnp.split(xbc, [SSD_D_INNER, SSD_D_INNER + SSD_GROUPS * SSD_STATE], axis=-1)
    xs = xs.reshape(b, lp, SSD_HEADS, SSD_HEAD_DIM)
    dt = jax.nn.softplus(front_pad(dt_raw).astype(jnp.float32) + ssd_dt_bias.astype(jnp.float32))
    A = -jnp.exp(ssd_A_log.astype(jnp.float32))
    y = _ssd_chunked(xs, dt, A, bs.reshape(b, lp, SSD_GROUPS, SSD_STATE),
                     cs.reshape(b, lp, SSD_GROUPS, SSD_STATE)) + ssd_D[:, None] * xs
    y = y[:, pad:].reshape(b, l, SSD_D_INNER)
    y_ssd = _group_rmsnorm(y * jax.nn.silu(z), ssd_norm_w)

    qkv = jax.nn.silu(_causal_dwconv(front_pad(qkv), gdn_conv_w)) * valid
    q, k, v = jnp.split(qkv, [GDN_DQK, 2 * GDN_DQK], axis=-1)
    q = _l2norm(q.reshape(b, lp, GDN_HEADS, GDN_DK))
    k = _l2norm(k.reshape(b, lp, GDN_HEADS, GDN_DK))
    v = v.reshape(b, lp, GDN_HEADS, GDN_DV)
    g = -jnp.exp(gdn_A_log.astype(jnp.float32)) * jax.nn.softplus(
        front_pad(a_raw).astype(jnp.float32) + gdn_dt_bias.astype(jnp.float32))
    beta = jax.nn.sigmoid(front_pad(b_raw)) * valid
    o = _gated_delta_chunked(q, k, v, g, beta)[:, pad:]
    o = _rmsnorm(o, gdn_norm_w) * jax.nn.silu(gate).reshape(b, l, GDN_HEADS, GDN_DV)
    y_gdn = o.reshape(b, l, GDN_DVW)

    gate_ssd, gate_gdn = jnp.split(merge_logits, N_BRANCH, axis=-1)
    m = jax.nn.sigmoid(gate_ssd) * (y_ssd @ w_branch_ssd) + jax.nn.sigmoid(gate_gdn) * (y_gdn @ w_branch_gdn)
    return m @ w_out


def _peer(h, w_query, sub_keys, expert_down, expert_up):
    b, l, d = h.shape
    q = (h @ w_query).reshape(b, l, PEER_HEADS, 2, PEER_D_KEY)
    s = jnp.einsum("blhkd,knd->blhkn", q, sub_keys).astype(jnp.float32)
    top_s, top_i = lax.top_k(s, PEER_TOPK)
    cand_s = (top_s[..., 0, :, None] + top_s[..., 1, None, :]).reshape(b, l, PEER_HEADS, PEER_TOPK * PEER_TOPK)
    cand_i = (top_i[..., 0, :, None] * PEER_N_KEYS + top_i[..., 1, None, :]).reshape(b, l, PEER_HEADS, PEER_TOPK * PEER_TOPK)
    best_s, pos = lax.top_k(cand_s, PEER_TOPK)
    idx = jnp.take_along_axis(cand_i, pos, axis=-1)
    gates = jax.nn.softmax(best_s, axis=-1).astype(h.dtype)

    nb = l // PEER_POS_BLOCK

    def blocks(t):
        return jnp.moveaxis(t.reshape((b, nb, PEER_POS_BLOCK) + t.shape[2:]), 1, 0)

    def expert_block(args):
        hb, ib, gb = args
        act = jax.nn.gelu(jnp.einsum("btd,bthkd->bthk", hb, expert_down[ib]), approximate=False) * gb
        return jnp.einsum("bthk,bthkd->btd", act, expert_up[ib])

    out = lax.map(expert_block, (blocks(h), blocks(idx), blocks(gates)))
    return jnp.moveaxis(out, 0, 1).reshape(b, l, d)


def setup_inputs(seed: int = 0) -> dict:
    key = jax.random.key(seed)
    ks = jax.random.split(key, 24)
    f32 = jnp.float32

    def nrm(k, shape, scale):
        return jax.random.normal(k, shape, f32) * scale

    def gain(k, shape):
        return 1.0 + 0.02 * jax.random.normal(k, shape, f32)

    def dt_bias(k, shape):
        dt = jnp.exp(jax.random.uniform(k, shape, f32, minval=np.log(1e-3), maxval=np.log(1e-1)))
        return dt + jnp.log(-jnp.expm1(-dt))

    def a_log(k, shape):
        return jnp.log(jax.random.uniform(k, shape, f32, minval=1.0, maxval=16.0))

    return {
        "x": nrm(ks[0], (BATCH, SEQ, D_MODEL), 1.0),
        "meta_tokens": nrm(ks[1], (N_META, D_MODEL), 1.0),
        "norm_mix_w": gain(ks[2], (DEPTH, D_MODEL)),
        "w_in": nrm(ks[3], (DEPTH, D_MODEL, IN_DIM), D_MODEL ** -0.5),
        "ssd_conv_w": nrm(ks[4], (DEPTH, CONV_K, SSD_CONV_DIM), CONV_K ** -0.5),
        "ssd_conv_b": nrm(ks[5], (DEPTH, SSD_CONV_DIM), 0.02),
        "ssd_dt_bias": dt_bias(ks[6], (DEPTH, SSD_HEADS)),
        "ssd_A_log": a_log(ks[7], (DEPTH, SSD_HEADS)),
        "ssd_D": 1.0 + 0.1 * jax.random.normal(ks[8], (DEPTH, SSD_HEADS), f32),
        "ssd_norm_w": gain(ks[9], (DEPTH, SSD_D_INNER)),
        "gdn_conv_w": nrm(ks[10], (DEPTH, CONV_K, GDN_CONV_DIM), CONV_K ** -0.5),
        "gdn_dt_bias": dt_bias(ks[11], (DEPTH, GDN_HEADS)),
        "gdn_A_log": a_log(ks[12], (DEPTH, GDN_HEADS)),
        "gdn_norm_w": gain(ks[13], (DEPTH, GDN_DV)),
        "w_branch_ssd": nrm(ks[14], (DEPTH, SSD_D_INNER, D_MODEL), SSD_D_INNER ** -0.5),
        "w_branch_gdn": nrm(ks[15], (DEPTH, GDN_DVW, D_MODEL), GDN_DVW ** -0.5),
        "w_out": nrm(ks[16], (DEPTH, D_MODEL, D_MODEL), D_MODEL ** -0.5),
        "norm_ffn_w": gain(ks[17], (DEPTH, D_MODEL)),
        "peer_w_query": nrm(ks[18], (DEPTH, D_MODEL, PEER_HEADS * 2 * PEER_D_KEY), D_MODEL ** -0.5),
        "peer_sub_keys": nrm(ks[19], (DEPTH, 2, PEER_N_KEYS, PEER_D_KEY), PEER_D_KEY ** -0.5),
        "peer_expert_down": nrm(ks[20], (DEPTH, PEER_N_EXPERTS, D_MODEL), D_MODEL ** -0.5),
        "peer_expert_up": nrm(ks[21], (DEPTH, PEER_N_EXPERTS, D_MODEL), 0.5),
        "final_norm_w": gain(ks[22], (D_MODEL,)),
    }


def reference(x, meta_tokens, norm_mix_w, w_in, ssd_conv_w, ssd_conv_b, ssd_dt_bias, ssd_A_log, ssd_D,
              ssd_norm_w, gdn_conv_w, gdn_dt_bias, gdn_A_log, gdn_norm_w, w_branch_ssd, w_branch_gdn,
              w_out, norm_ffn_w, peer_w_query, peer_sub_keys, peer_expert_down, peer_expert_up,
              final_norm_w):
    b = x.shape[0]
    meta = jnp.broadcast_to(meta_tokens[None].astype(x.dtype), (b, N_META, D_MODEL))
    h = jnp.concatenate([meta, x], axis=1)
    for i in range(DEPTH):
        h = h + _hybrid_mixer(_rmsnorm(h, norm_mix_w[i]), w_in[i], ssd_conv_w[i], ssd_conv_b[i],
                              ssd_dt_bias[i], ssd_A_log[i], ssd_D[i], ssd_norm_w[i], gdn_conv_w[i],
                              gdn_dt_bias[i], gdn_A_log[i], gdn_norm_w[i], w_branch_ssd[i],
                              w_branch_gdn[i], w_out[i])
        h = h + _peer(_rmsnorm(h, norm_ffn_w[i]), peer_w_query[i], peer_sub_keys[i],
                      peer_expert_down[i], peer_expert_up[i])
    return _rmsnorm(h, final_norm_w)[:, N_META:]
```

```python
import functools

import jax
import jax.numpy as jnp
from jax import lax
from jax.experimental import pallas as pl
from jax.experimental.pallas import tpu as pltpu

F32 = jnp.float32
BF16 = jnp.bfloat16

N_META = 16
CHUNK = 64
CONV_K = 4
NORM_EPS = 1e-6
L2_EPS = 1e-6

SSD_HEADS = 32
SSD_HEAD_DIM = 64
SSD_GROUPS = 4
SSD_STATE = 128
SSD_HPG = SSD_HEADS // SSD_GROUPS
SSD_D_INNER = SSD_HEADS * SSD_HEAD_DIM
SSD_BC = SSD_GROUPS * SSD_STATE

GDN_HEADS = 16
GDN_DK = 128
GDN_DV = 128
GDN_DQK = GDN_HEADS * GDN_DK

PEER_HEADS = 8
PEER_N_KEYS = 128
PEER_TOPK = 16
PEER_D_KEY = 128

CONV_HALO = 8
LANES = 128
VMEM_LIMIT = 56 * 1024 * 1024

_NT = (((1,), (1,)), ((), ()))
_TN = (((0,), (0,)), ((), ()))


def _dot(a, b):
    return jnp.dot(a, b, preferred_element_type=F32)


def _dot_nt(a, b):
    return lax.dot_general(a, b, _NT, preferred_element_type=F32)


def _dot_tn(a, b):
    return lax.dot_general(a, b, _TN, preferred_element_type=F32)


def _silu(x):
    return x / (1.0 + jnp.exp(-x))


def _sigmoid(x):
    return 1.0 / (1.0 + jnp.exp(-x))


def _softplus(x):
    return jnp.maximum(x, 0.0) + jnp.log(1.0 + jnp.exp(-jnp.abs(x)))


def _divisor_tile(n, preferred, multiple):
    t = min(preferred, n)
    t -= t % multiple
    while t >= multiple:
        if n % t == 0:
            return t
        t -= multiple
    return n


def _params(*sem):
    return pltpu.CompilerParams(dimension_semantics=sem, vmem_limit_bytes=VMEM_LIMIT)


def _norm_matmul_body(x_ref, g_ref, w_ref, o_ref, n_ref):
    @pl.when(pl.program_id(1) == 0)
    def _():
        x = x_ref[...]
        ms = jnp.mean(x * x, axis=-1, keepdims=True)
        n_ref[...] = (x * lax.rsqrt(ms + NORM_EPS) * g_ref[...]).astype(BF16)

    o_ref[...] = _dot(n_ref[...], w_ref[...])


def _norm_matmul(x2d, gain, w, tm_pref=768, tn_pref=1024):
    m, k = x2d.shape
    n = w.shape[1]
    tm = _divisor_tile(m, tm_pref, 8)
    tn = _divisor_tile(n, tn_pref, LANES)
    return pl.pallas_call(
        _norm_matmul_body,
        grid=(m // tm, n // tn),
        in_specs=[
            pl.BlockSpec((tm, k), lambda i, j: (i, 0)),
            pl.BlockSpec((1, k), lambda i, j: (0, 0)),
            pl.BlockSpec((k, tn), lambda i, j: (0, j)),
        ],
        out_specs=pl.BlockSpec((tm, tn), lambda i, j: (i, j)),
        out_shape=jax.ShapeDtypeStruct((m, n), F32),
        scratch_shapes=[pltpu.VMEM((tm, k), BF16)],
        compiler_params=_params("arbitrary", "arbitrary"),
        name="norm_matmul",
    )(x2d, gain.reshape(1, k), w)


def _gates_body(raw_ref, bias_ref, alog_ref, sp_ref, cs_ref, bt_ref, *, n_chunks, pad):
    ri = lax.broadcasted_iota(jnp.int32, (CHUNK, CHUNK), 0)
    ci = lax.broadcasted_iota(jnp.int32, (CHUNK, CHUNK), 1)
    tri = (ri >= ci).astype(F32)
    neg_a = -jnp.exp(alog_ref[...])
    bias = bias_ref[...]
    row = lax.broadcasted_iota(jnp.int32, (CHUNK, 1), 0)

    def body(c, carry):
        sl = pl.ds(pl.multiple_of(c * CHUNK, CHUNK), CHUNK)
        r = raw_ref[0, sl, :]
        sp = _softplus(r + bias)
        cs = jnp.dot(tri, sp * neg_a, precision=lax.Precision.HIGHEST, preferred_element_type=F32)
        valid = (row + c * CHUNK) >= pad
        sp_ref[0, sl, :] = sp
        cs_ref[0, sl, :] = cs
        bt_ref[0, sl, :] = jnp.where(valid, _sigmoid(r), 0.0)
        return carry

    lax.fori_loop(0, n_chunks, body, 0)


def _gates(raw, bias_vec, alog_vec, pad):
    b, lp, w = raw.shape
    spec = pl.BlockSpec((1, lp, w), lambda i: (i, 0, 0))
    vec = pl.BlockSpec((1, w), lambda i: (0, 0))
    shp = jax.ShapeDtypeStruct((b, lp, w), F32)
    return pl.pallas_call(
        functools.partial(_gates_body, n_chunks=lp // CHUNK, pad=pad),
        grid=(b,),
        in_specs=[spec, vec, vec],
        out_specs=(spec, spec, spec),
        out_shape=(shp, shp, shp),
        compiler_params=_params("arbitrary"),
        name="gates",
    )(raw, bias_vec, alog_vec)


def _causal_conv(xpad_ref, w_ref, lo, width, bias):
    acc = bias
    for j in range(CONV_K):
        start = CONV_HALO - (CONV_K - 1) + j
        term = w_ref[j:j + 1, lo:lo + width] * xpad_ref[start:start + CHUNK, lo:lo + width]
        acc = term if acc is None else acc + term
    return acc


def _ssd_body(x_ref, bc_ref, cw_ref, cb_ref, sp_ref, cs_ref, csr_ref, dvec_ref, y_ref,
              xpad_ref, st_ref, *, pad):
    c = pl.program_id(1)

    @pl.when(c == 0)
    def _():
        st_ref[...] = jnp.zeros_like(st_ref)
        xpad_ref[0:CONV_HALO, :] = jnp.zeros((CONV_HALO, xpad_ref.shape[1]), F32)

    xpad_ref[CONV_HALO:CONV_HALO + CHUNK, 0:SSD_D_INNER] = x_ref[0]
    xpad_ref[CONV_HALO:CONV_HALO + CHUNK, SSD_D_INNER:] = bc_ref[0]

    row = lax.broadcasted_iota(jnp.int32, (CHUNK, 1), 0)
    valid = (row + c * CHUNK) >= pad
    ri = lax.broadcasted_iota(jnp.int32, (CHUNK, CHUNK), 0)
    ci = lax.broadcasted_iota(jnp.int32, (CHUNK, CHUNK), 1)
    tril = ri >= ci

    sp = sp_ref[0]
    cs = cs_ref[0]
    csr = csr_ref[0, 0]

    def conv(lo, width):
        a = _causal_conv(xpad_ref, cw_ref, lo, width, cb_ref[:, lo:lo + width])
        return jnp.where(valid, _silu(a), 0.0)

    gw = SSD_HPG * SSD_HEAD_DIM
    for g in range(SSD_GROUPS):
        xg = conv(g * gw, gw)
        bg = conv(SSD_D_INNER + g * SSD_STATE, SSD_STATE).astype(BF16)
        cg = conv(SSD_D_INNER + SSD_BC + g * SSD_STATE, SSD_STATE).astype(BF16)
        cb = _dot_nt(cg, bg)
        st_g = st_ref[g]
        yoff = _dot_nt(cg, st_g.astype(BF16))
        ys = []
        for r in range(SSD_HPG):
            h = g * SSD_HPG + r
            lo, hi = r * SSD_HEAD_DIM, (r + 1) * SSD_HEAD_DIM
            col = cs[:, h:h + 1]
            rowv = csr[h:h + 1, :]
            last = rowv[:, CHUNK - 1:CHUNK]
            decay = jnp.where(tril, jnp.exp(col - rowv), 0.0)
            w_intra = (cb * decay).astype(BF16)
            xdt = xg[:, lo:hi] * sp[:, h:h + 1]
            y_r = _dot(w_intra, xdt.astype(BF16)) + yoff[:, lo:hi] * jnp.exp(col)
            xw = (xdt * jnp.exp(last - col)).astype(BF16)
            st_ref[g, lo:hi, :] = st_g[lo:hi, :] * jnp.exp(last) + _dot_tn(xw, bg)
            ys.append(y_r)
        y_ref[0, :, g * gw:(g + 1) * gw] = jnp.concatenate(ys, axis=1) + dvec_ref[:, g * gw:(g + 1) * gw] * xg

    xpad_ref[0:CONV_HALO, :] = xpad_ref[CHUNK:CHUNK + CONV_HALO, :]


def _ssd(proj, conv_w, conv_b, sp, cs, cs_rows, d_vec, seq, pad):
    b, lp, _ = proj.shape
    nc = lp // CHUNK
    conv_dim = SSD_D_INNER + 2 * SSD_BC
    x_blk = 3 * GDN_DQK // SSD_D_INNER
    bc_blk = (3 * GDN_DQK + SSD_D_INNER) // (2 * SSD_BC)
    return pl.pallas_call(
        functools.partial(_ssd_body, pad=pad),
        grid=(b, nc),
        in_specs=[
            pl.BlockSpec((1, CHUNK, SSD_D_INNER), lambda i, c: (i, c, x_blk)),
            pl.BlockSpec((1, CHUNK, 2 * SSD_BC), lambda i, c: (i, c, bc_blk)),
            pl.BlockSpec((CONV_K, conv_dim), lambda i, c: (0, 0)),
            pl.BlockSpec((1, conv_dim), lambda i, c: (0, 0)),
            pl.BlockSpec((1, CHUNK, LANES), lambda i, c: (i, c, 0)),
            pl.BlockSpec((1, CHUNK, LANES), lambda i, c: (i, c, 0)),
            pl.BlockSpec((1, 1, SSD_HEADS, CHUNK), lambda i, c: (i, c, 0, 0)),
            pl.BlockSpec((1, SSD_D_INNER), lambda i, c: (0, 0)),
        ],
        out_specs=pl.BlockSpec((1, CHUNK, SSD_D_INNER), lambda i, c: (i, jnp.maximum(c - 1, 0), 0)),
        out_shape=jax.ShapeDtypeStruct((b, seq, SSD_D_INNER), F32),
        scratch_shapes=[
            pltpu.VMEM((CONV_HALO + CHUNK, conv_dim), F32),
            pltpu.VMEM((SSD_GROUPS, SSD_HPG * SSD_HEAD_DIM, SSD_STATE), F32),
        ],
        compiler_params=_params("arbitrary", "arbitrary"),
        name="ssd_scan",
    )(proj, proj, conv_w, conv_b.reshape(1, conv_dim), sp, cs, cs_rows, d_vec)


def _unit_lower_inverse(strict_l, eye):
    p = strict_l.astype(BF16)
    t = eye - strict_l
    n = 2
    while n < CHUNK:
        p2 = _dot(p, p)
        t = t + _dot(t.astype(BF16), p2.astype(BF16))
        p = p2.astype(BF16)
        n *= 2
    return t


def _gdn_body(q_ref, k_ref, v_ref, cw_ref, cs_ref, bt_ref, gr_ref, nw_ref, o_ref,
              xpad_ref, st_ref, *, pad):
    c = pl.program_id(1)

    @pl.when(c == 0)
    def _():
        st_ref[...] = jnp.zeros_like(st_ref)
        xpad_ref[0:CONV_HALO, :] = jnp.zeros((CONV_HALO, xpad_ref.shape[1]), F32)

    xpad_ref[CONV_HALO:CONV_HALO + CHUNK, 0:GDN_DQK] = q_ref[0]
    xpad_ref[CONV_HALO:CONV_HALO + CHUNK, GDN_DQK:2 * GDN_DQK] = k_ref[0]
    xpad_ref[CONV_HALO:CONV_HALO + CHUNK, 2 * GDN_DQK:] = v_ref[0]

    row = lax.broadcasted_iota(jnp.int32, (CHUNK, 1), 0)
    valid = (row + c * CHUNK) >= pad
    ri = lax.broadcasted_iota(jnp.int32, (CHUNK, CHUNK), 0)
    ci = lax.broadcasted_iota(jnp.int32, (CHUNK, CHUNK), 1)
    tril = ri >= ci
    strict = ri > ci
    eye = (ri == ci).astype(F32)

    cs = cs_ref[0]
    bt = bt_ref[0]
    gr = gr_ref[0, 0]
    nw = nw_ref[...]
    g_off = SSD_HEADS
    b_off = SSD_HEADS + GDN_HEADS

    def conv(lo):
        a = _causal_conv(xpad_ref, cw_ref, lo, GDN_DK, None)
        return jnp.where(valid, _silu(a), 0.0)

    for h in range(GDN_HEADS):
        q = conv(h * GDN_DK)
        k = conv(GDN_DQK + h * GDN_DK)
        v = conv(2 * GDN_DQK + h * GDN_DV)
        q = q * lax.rsqrt(jnp.sum(q * q, axis=-1, keepdims=True) + L2_EPS) * (GDN_DK ** -0.5)
        k = k * lax.rsqrt(jnp.sum(k * k, axis=-1, keepdims=True) + L2_EPS)
        col = cs[:, g_off + h:g_off + h + 1]
        rowv = gr[h:h + 1, :]
        last = rowv[:, CHUNK - 1:CHUNK]
        beta = bt[:, b_off + h:b_off + h + 1]
        ecol = jnp.exp(col)
        decay = jnp.where(tril, jnp.exp(col - rowv), 0.0)
        kb = k * beta
        k16 = k.astype(BF16)
        kk = _dot_nt(kb.astype(BF16), k16)
        qk = _dot_nt(q.astype(BF16), k16)
        tinv = _unit_lower_inverse(jnp.where(strict, kk * decay, 0.0), eye)
        rhs = jnp.concatenate([v * beta, kb * ecol], axis=1).astype(BF16)
        uw = _dot(tinv.astype(BF16), rhs)
        u = uw[:, :GDN_DV]
        w = uw[:, GDN_DV:]
        s = st_ref[h]
        ws = _dot(jnp.concatenate([w, q * ecol], axis=0).astype(BF16), s.astype(BF16))
        v_new = u - ws[:CHUNK]
        v16 = v_new.astype(BF16)
        o = ws[CHUNK:] + _dot((qk * decay).astype(BF16), v16)
        kd = (k * jnp.exp(last - col)).astype(BF16)
        st_ref[h] = s * jnp.exp(last) + _dot_tn(kd, v16)
        o = o * lax.rsqrt(jnp.mean(o * o, axis=-1, keepdims=True) + NORM_EPS) * nw
        o_ref[0, :, h * GDN_DV:(h + 1) * GDN_DV] = o

    xpad_ref[0:CONV_HALO, :] = xpad_ref[CHUNK:CHUNK + CONV_HALO, :]


def _gdn(proj, conv_w, cs, bt, g_rows, norm_w, seq, pad):
    b, lp, _ = proj.shape
    nc = lp // CHUNK
    conv_dim = 3 * GDN_DQK
    qkv = lambda blk: pl.BlockSpec((1, CHUNK, GDN_DQK), lambda i, c: (i, c, blk))
    return pl.pallas_call(
        functools.partial(_gdn_body, pad=pad),
        grid=(b, nc),
        in_specs=[
            qkv(0), qkv(1), qkv(2),
            pl.BlockSpec((CONV_K, conv_dim), lambda i, c: (0, 0)),
            pl.BlockSpec((1, CHUNK, LANES), lambda i, c: (i, c, 0)),
            pl.BlockSpec((1, CHUNK, LANES), lambda i, c: (i, c, 0)),
            pl.BlockSpec((1, 1, GDN_HEADS, CHUNK), lambda i, c: (i, c, 0, 0)),
            pl.BlockSpec((1, GDN_DV), lambda i, c: (0, 0)),
        ],
        out_specs=pl.BlockSpec((1, CHUNK, GDN_DQK), lambda i, c: (i, jnp.maximum(c - 1, 0), 0)),
        out_shape=jax.ShapeDtypeStruct((b, seq, GDN_HEADS * GDN_DV), F32),
        scratch_shapes=[
            pltpu.VMEM((CONV_HALO + CHUNK, conv_dim), F32),
            pltpu.VMEM((GDN_HEADS, GDN_DK, GDN_DV), F32),
        ],
        compiler_params=_params("arbitrary", "arbitrary"),
        name="gdn_scan",
    )(proj, proj, proj, conv_w, cs, bt, g_rows, norm_w.reshape(1, GDN_DV))


def _merge_body(y_ref, z_ref, o_ref, gate_ref, g1_ref, g2_ref, nw_ref, w1_ref, w2_ref, m_ref,
                a1_ref, a2_ref):
    @pl.when(pl.program_id(1) == 0)
    def _():
        gw = SSD_D_INNER // SSD_GROUPS
        for g in range(SSD_GROUPS):
            sl = slice(g * gw, (g + 1) * gw)
            t = y_ref[:, sl] * _silu(z_ref[:, sl])
            ms = jnp.mean(t * t, axis=-1, keepdims=True)
            a1_ref[:, sl] = (t * lax.rsqrt(ms + NORM_EPS) * nw_ref[:, sl]).astype(BF16)
        a2_ref[...] = (o_ref[...] * _silu(gate_ref[...])).astype(BF16)

    m_ref[...] = (_sigmoid(g1_ref[...]) * _dot(a1_ref[...], w1_ref[...])
                  + _sigmoid(g2_ref[...]) * _dot(a2_ref[...], w2_ref[...]))


def _merge(y, o, proj_tok, norm_w, w1, w2, tm_pref=256, tn_pref=1024):
    m, d = y.shape
    n = w1.shape[1]
    tm = _divisor_tile(m, tm_pref, 8)
    tn = _divisor_tile(n, tn_pref, LANES)
    nj = n // tn
    row_d = lambda blk: pl.BlockSpec((tm, d), lambda i, j: (i, blk))
    return pl.pallas_call(
        _merge_body,
        grid=(m // tm, nj),
        in_specs=[
            pl.BlockSpec((tm, d), lambda i, j: (i, 0)),
            row_d(0),
            pl.BlockSpec((tm, d), lambda i, j: (i, 0)),
            row_d(1),
            pl.BlockSpec((tm, tn), lambda i, j: (i, 2 * nj + j)),
            pl.BlockSpec((tm, tn), lambda i, j: (i, 3 * nj + j)),
            pl.BlockSpec((1, d), lambda i, j: (0, 0)),
            pl.BlockSpec((d, tn), lambda i, j: (0, j)),
            pl.BlockSpec((d, tn), lambda i, j: (0, j)),
        ],
        out_specs=pl.BlockSpec((tm, tn), lambda i, j: (i, j)),
        out_shape=jax.ShapeDtypeStruct((m, n), F32),
        scratch_shapes=[pltpu.VMEM((tm, d), BF16), pltpu.VMEM((tm, d), BF16)],
        compiler_params=_params("arbitrary", "arbitrary"),
        name="merge",
    )(y, proj_tok, o, proj_tok, proj_tok, proj_tok, norm_w.reshape(1, d), w1, w2)


def _matmul_residual_body(a_ref, w_ref, r_ref, o_ref):
    o_ref[...] = r_ref[...] + _dot(a_ref[...].astype(BF16), w_ref[...])


def _matmul_residual(a, w, res, tm_pref=512, tn_pref=1024):
    m, k = a.shape
    n = w.shape[1]
    tm = _divisor_tile(m, tm_pref, 8)
    tn = _divisor_tile(n, tn_pref, LANES)
    return pl.pallas_call(
        _matmul_residual_body,
        grid=(m // tm, n // tn),
        in_specs=[
            pl.BlockSpec((tm, k), lambda i, j: (i, 0)),
            pl.BlockSpec((k, tn), lambda i, j: (0, j)),
            pl.BlockSpec((tm, tn), lambda i, j: (i, j)),
        ],
        out_specs=pl.BlockSpec((tm, tn), lambda i, j: (i, j)),
        out_shape=jax.ShapeDtypeStruct((m, n), F32),
        compiler_params=_params("arbitrary", "arbitrary"),
        name="out_proj_residual",
    )(a, w, res)


def _top_values(x, count):
    out = []
    for _ in range(count):
        m = jnp.max(x, axis=0, keepdims=True)
        out.append(m)
        x = jnp.where(x == m, -jnp.inf, x)
    return out


def _peer_route_body(q_ref, keys_ref, s2_ref, e2_ref, thr_ref, f1_ref):
    k1 = keys_ref[0].astype(BF16)
    k2 = keys_ref[1].astype(BF16)
    kk = PEER_TOPK
    for h in range(PEER_HEADS):
        lo = 2 * h * PEER_D_KEY
        q1 = q_ref[:, lo:lo + PEER_D_KEY].astype(BF16)
        q2 = q_ref[:, lo + PEER_D_KEY:lo + 2 * PEER_D_KEY].astype(BF16)
        s1 = _dot_nt(k1, q1)
        s2 = _dot_nt(k2, q2)
        t1 = _top_values(s1, kk + 1)
        t2 = _top_values(s2, kk + 1)
        t2_blk = jnp.concatenate(t2[:kk], axis=0)
        cand = jnp.concatenate([t1[a] + t2_blk for a in range(kk)], axis=0)
        best = _top_values(cand, kk + 1)
        runner_up = jnp.maximum(best[kk], jnp.maximum(t1[kk] + t2[0], t1[0] + t2[kk]))
        tau = 0.5 * (best[kk - 1] + runner_up)
        m1, m2 = t1[0], t2[0]
        z = jnp.sum(jnp.where(cand >= tau, jnp.exp(cand - (m1 + m2)), 0.0), axis=0, keepdims=True)
        s2_ref[h] = s2
        e2_ref[h] = jnp.exp(s2 - m2)
        thr_ref[h] = tau - s1
        f1_ref[h] = jnp.exp(s1 - m1) / z


def _peer_route(q, keys, t_tok):
    ntok = q.shape[0]
    spec = pl.BlockSpec((PEER_HEADS, PEER_N_KEYS, t_tok), lambda t: (0, 0, t))
    shp = jax.ShapeDtypeStruct((PEER_HEADS, PEER_N_KEYS, ntok), F32)
    return pl.pallas_call(
        _peer_route_body,
        grid=(ntok // t_tok,),
        in_specs=[
            pl.BlockSpec((t_tok, q.shape[1]), lambda t: (t, 0)),
            pl.BlockSpec(keys.shape, lambda t: (0, 0, 0)),
        ],
        out_specs=(spec, spec, spec, spec),
        out_shape=(shp, shp, shp, shp),
        compiler_params=_params("arbitrary"),
        name="peer_route",
    )(q, keys)


def _peer_expert_body(h_ref, nw_ref, fw_ref, s2_ref, e2_ref, thr_ref, f1_ref, down_ref, upt_ref,
                      out_ref, hn_ref, act_ref, p_ref, acc_ref, *, keys_per_block):
    e = pl.program_id(1)
    t_tok = h_ref.shape[0]

    @pl.when(e == 0)
    def _():
        x = h_ref[...]
        ms = jnp.mean(x * x, axis=-1, keepdims=True)
        hn_ref[...] = (x * lax.rsqrt(ms + NORM_EPS) * nw_ref[...]).astype(BF16)
        acc_ref[...] = jnp.zeros_like(acc_ref)

    act_ref[...] = _dot_nt(down_ref[...], hn_ref[...])

    def strip(s, carry):
        ln = pl.ds(pl.multiple_of(s * LANES, LANES), LANES)

        for i in range(keys_per_block):
            w = jnp.zeros((PEER_N_KEYS, LANES), F32)
            for hd in range(PEER_HEADS):
                thr = thr_ref[hd, i:i + 1, ln]
                f1 = f1_ref[hd, i:i + 1, ln]
                w = w + f1 * jnp.where(s2_ref[hd, :, ln] >= thr, e2_ref[hd, :, ln], 0.0)
            rows = slice(i * PEER_N_KEYS, (i + 1) * PEER_N_KEYS)
            a = act_ref[rows, ln]
            gelu = 0.5 * a * (1.0 + lax.erf(a * (2.0 ** -0.5)))
            p_ref[rows, ln] = (w * gelu).astype(BF16)
        return carry

    lax.fori_loop(0, t_tok // LANES, strip, 0)
    acc_ref[...] += _dot(upt_ref[...], p_ref[...])

    @pl.when(e == pl.num_programs(1) - 1)
    def _():
        y = h_ref[...] + acc_ref[...].T
        ms = jnp.mean(y * y, axis=-1, keepdims=True)
        out_ref[...] = y * lax.rsqrt(ms + NORM_EPS) * fw_ref[...]


def _peer_experts(h2, norm_w, final_w, s2, e2, thr, f1, down, up_t, t_tok, keys_per_block):
    ntok, d = h2.shape
    n_exp = down.shape[0]
    te = keys_per_block * PEER_N_KEYS
    full = pl.BlockSpec((PEER_HEADS, PEER_N_KEYS, t_tok), lambda t, e: (0, 0, t))
    part = pl.BlockSpec((PEER_HEADS, keys_per_block, t_tok), lambda t, e: (0, e, t))
    vec = pl.BlockSpec((1, d), lambda t, e: (0, 0))
    return pl.pallas_call(
        functools.partial(_peer_expert_body, keys_per_block=keys_per_block),
        grid=(ntok // t_tok, n_exp // te),
        in_specs=[
            pl.BlockSpec((t_tok, d), lambda t, e: (t, 0)),
            vec, vec, full, full, part, part,
            pl.BlockSpec((te, d), lambda t, e: (e, 0)),
            pl.BlockSpec((d, te), lambda t, e: (0, e)),
        ],
        out_specs=pl.BlockSpec((t_tok, d), lambda t, e: (t, 0)),
        out_shape=jax.ShapeDtypeStruct((ntok, d), F32),
        scratch_shapes=[
            pltpu.VMEM((t_tok, d), BF16),
            pltpu.VMEM((te, t_tok), F32),
            pltpu.VMEM((te, t_tok), BF16),
            pltpu.VMEM((d, t_tok), F32),
        ],
        compiler_params=_params("arbitrary", "arbitrary"),
        name="peer_experts",
    )(h2, norm_w.reshape(1, d), final_w.reshape(1, d), s2, e2, thr, f1, down, up_t)


def _split_w_in(w_in):
    d = w_in.shape[0]
    widths = (SSD_D_INNER, SSD_D_INNER + 2 * SSD_BC, SSD_HEADS, 3 * GDN_DQK, GDN_HEADS * GDN_DV,
              GDN_HEADS, GDN_HEADS, 2 * d)
    offs = [0]
    for wd in widths:
        offs.append(offs[-1] + wd)
    seg = [w_in[:, offs[i]:offs[i + 1]] for i in range(len(widths))]
    z, xbc, dt, qkv, gate, a, beta, merge = seg
    w_scan = jnp.concatenate([qkv, xbc], axis=1).astype(BF16)
    w_tok = jnp.concatenate([z, gate, merge], axis=1).astype(BF16)
    small = jnp.concatenate([dt, a, beta], axis=1)
    w_small = jnp.pad(small, ((0, 0), (0, LANES - small.shape[1]))).astype(BF16)
    return w_scan, w_tok, w_small


def _lane_vec(*parts):
    v = jnp.concatenate([p.astype(F32) for p in parts])
    return jnp.pad(v, (0, LANES - v.shape[0])).reshape(1, LANES)


def _mixer(x, meta_tokens, norm_w, w_in, ssd_conv_w, ssd_conv_b, ssd_dt_bias, ssd_A_log, ssd_D,
           ssd_norm_w, gdn_conv_w, gdn_dt_bias, gdn_A_log, gdn_norm_w, w_branch_ssd, w_branch_gdn, w_out):
    b, seq, d = x.shape
    pad = CHUNK - N_META
    lp = seq + CHUNK
    nc = lp // CHUNK
    w_scan, w_tok, w_small = _split_w_in(w_in)

    meta = jnp.broadcast_to(meta_tokens[None].astype(x.dtype), (b, N_META, d))
    hp = jnp.concatenate([jnp.zeros((b, pad, d), x.dtype), meta, x], axis=1).reshape(b * lp, d)
    x2d = x.reshape(b * seq, d)

    proj_scan = _norm_matmul(hp, norm_w, w_scan).reshape(b, lp, -1)
    raw_small = _norm_matmul(hp, norm_w, w_small).reshape(b, lp, LANES)
    proj_tok = _norm_matmul(x2d, norm_w, w_tok, tm_pref=512)

    zeros_h = jnp.zeros((GDN_HEADS,), F32)
    bias_vec = _lane_vec(ssd_dt_bias, gdn_dt_bias, zeros_h)
    alog_vec = _lane_vec(ssd_A_log, gdn_A_log, zeros_h)
    sp, cs, bt = _gates(raw_small, bias_vec, alog_vec, pad)
    cs_c = cs.reshape(b, nc, CHUNK, LANES)
    ssd_rows = jnp.swapaxes(cs_c[..., :SSD_HEADS], 2, 3)
    gdn_rows = jnp.swapaxes(cs_c[..., SSD_HEADS:SSD_HEADS + GDN_HEADS], 2, 3)

    d_vec = jnp.repeat(ssd_D.astype(F32), SSD_HEAD_DIM).reshape(1, SSD_D_INNER)
    y = _ssd(proj_scan, ssd_conv_w, ssd_conv_b, sp, cs, ssd_rows, d_vec, seq, pad)
    o = _gdn(proj_scan, gdn_conv_w, cs, bt, gdn_rows, gdn_norm_w, seq, pad)

    m = _merge(y.reshape(b * seq, -1), o.reshape(b * seq, -1), proj_tok, ssd_norm_w,
               w_branch_ssd.astype(BF16), w_branch_gdn.astype(BF16))
    return _matmul_residual(m, w_out.astype(BF16), x2d)


def _peer(h2, norm_w, w_query, sub_keys, expert_down, expert_up, final_w):
    ntok = h2.shape[0]
    t_tok = _divisor_tile(ntok, 256, LANES)
    q = _norm_matmul(h2, norm_w, w_query.astype(BF16), tm_pref=512)
    s2, e2, thr, f1 = _peer_route(q, sub_keys, t_tok)
    return _peer_experts(h2, norm_w, final_w, s2, e2, thr, f1, expert_down.astype(BF16),
                         expert_up.T.astype(BF16), t_tok, keys_per_block=8)


def kernel(x, meta_tokens, norm_mix_w, w_in, ssd_conv_w, ssd_conv_b, ssd_dt_bias, ssd_A_log, ssd_D, ssd_norm_w, gdn_conv_w, gdn_dt_bias, gdn_A_log, gdn_norm_w, w_branch_ssd, w_branch_gdn, w_out, norm_ffn_w, peer_w_query, peer_sub_keys, peer_expert_down, peer_expert_up, final_norm_w):
    b, seq, d = x.shape
    assert norm_mix_w.shape[0] == 1, "single-layer trunk"
    h2 = _mixer(x, meta_tokens, norm_mix_w[0], w_in[0], ssd_conv_w[0], ssd_conv_b[0], ssd_dt_bias[0],
                ssd_A_log[0], ssd_D[0], ssd_norm_w[0], gdn_conv_w[0], gdn_dt_bias[0], gdn_A_log[0],
                gdn_norm_w[0], w_branch_ssd[0], w_branch_gdn[0], w_out[0])
    out = _peer(h2, norm_ffn_w[0], peer_w_query[0], peer_sub_keys[0], peer_expert_down[0],
                peer_expert_up[0], final_norm_w)
    return out.reshape(b, seq, d)
```

```python
import functools

import jax
import jax.numpy as jnp
from jax import lax
from jax.experimental import pallas as pl
from jax.experimental.pallas import tpu as pltpu

F32 = jnp.float32
BF16 = jnp.bfloat16

N_META = 16
CHUNK = 64
CONV_K = 4
NORM_EPS = 1e-6
L2_EPS = 1e-6

SSD_HEADS = 32
SSD_HEAD_DIM = 64
SSD_GROUPS = 4
SSD_STATE = 128
SSD_HPG = SSD_HEADS // SSD_GROUPS
SSD_D_INNER = SSD_HEADS * SSD_HEAD_DIM
SSD_BC = SSD_GROUPS * SSD_STATE

GDN_HEADS = 16
GDN_DK = 128
GDN_DV = 128
GDN_DQK = GDN_HEADS * GDN_DK

PEER_HEADS = 8
PEER_N_KEYS = 128
PEER_TOPK = 16
PEER_D_KEY = 128

CONV_HALO = 8
LANES = 128
VMEM_LIMIT = 56 * 1024 * 1024

_NT = (((1,), (1,)), ((), ()))
_TN = (((0,), (0,)), ((), ()))


def _dot(a, b):
    return jnp.dot(a, b, preferred_element_type=F32)


def _dot_nt(a, b):
    return lax.dot_general(a, b, _NT, preferred_element_type=F32)


def _dot_tn(a, b):
    return lax.dot_general(a, b, _TN, preferred_element_type=F32)


def _silu(x):
    return x / (1.0 + jnp.exp(-x))


def _sigmoid(x):
    return 1.0 / (1.0 + jnp.exp(-x))


def _softplus(x):
    return jnp.maximum(x, 0.0) + jnp.log(1.0 + jnp.exp(-jnp.abs(x)))


def _divisor_tile(n, preferred, multiple):
    t = min(preferred, n)
    t -= t % multiple
    while t >= multiple:
        if n % t == 0:
            return t
        t -= multiple
    return n


def _params(*sem):
    return pltpu.CompilerParams(dimension_semantics=sem, vmem_limit_bytes=VMEM_LIMIT)


def _norm_matmul_body(x_ref, g_ref, w_ref, o_ref, n_ref):
    @pl.when(pl.program_id(1) == 0)
    def _():
        x = x_ref[...]
        ms = jnp.mean(x * x, axis=-1, keepdims=True)
        n_ref[...] = (x * lax.rsqrt(ms + NORM_EPS) * g_ref[...]).astype(BF16)

    o_ref[...] = _dot(n_ref[...], w_ref[...])


def _norm_matmul(x2d, gain, w, tm_pref=768, tn_pref=1024):
    m, k = x2d.shape
    n = w.shape[1]
    tm = _divisor_tile(m, tm_pref, 8)
    tn = _divisor_tile(n, tn_pref, LANES)
    return pl.pallas_call(
        _norm_matmul_body,
        grid=(m // tm, n // tn),
        in_specs=[
            pl.BlockSpec((tm, k), lambda i, j: (i, 0)),
            pl.BlockSpec((1, k), lambda i, j: (0, 0)),
            pl.BlockSpec((k, tn), lambda i, j: (0, j)),
        ],
        out_specs=pl.BlockSpec((tm, tn), lambda i, j: (i, j)),
        out_shape=jax.ShapeDtypeStruct((m, n), F32),
        scratch_shapes=[pltpu.VMEM((tm, k), BF16)],
        compiler_params=_params("arbitrary", "arbitrary"),
        name="norm_matmul",
    )(x2d, gain.reshape(1, k), w)


def _gates_body(raw_ref, bias_ref, alog_ref, sp_ref, cs_ref, bt_ref, *, n_chunks, pad):
    ri = lax.broadcasted_iota(jnp.int32, (CHUNK, CHUNK), 0)
    ci = lax.broadcasted_iota(jnp.int32, (CHUNK, CHUNK), 1)
    tri = (ri >= ci).astype(F32)
    neg_a = -jnp.exp(alog_ref[...])
    bias = bias_ref[...]
    row = lax.broadcasted_iota(jnp.int32, (CHUNK, 1), 0)

    def body(c, carry):
        sl = pl.ds(pl.multiple_of(c * CHUNK, CHUNK), CHUNK)
        r = raw_ref[0, sl, :]
        sp = _softplus(r + bias)
        cs = jnp.dot(tri, sp * neg_a, precision=lax.Precision.HIGHEST, preferred_element_type=F32)
        valid = (row + c * CHUNK) >= pad
        sp_ref[0, sl, :] = sp
        cs_ref[0, sl, :] = cs
        bt_ref[0, sl, :] = jnp.where(valid, _sigmoid(r), 0.0)
        return carry

    lax.fori_loop(0, n_chunks, body, 0)


def _gates(raw, bias_vec, alog_vec, pad):
    b, lp, w = raw.shape
    spec = pl.BlockSpec((1, lp, w), lambda i: (i, 0, 0))
    vec = pl.BlockSpec((1, w), lambda i: (0, 0))
    shp = jax.ShapeDtypeStruct((b, lp, w), F32)
    return pl.pallas_call(
        functools.partial(_gates_body, n_chunks=lp // CHUNK, pad=pad),
        grid=(b,),
        in_specs=[spec, vec, vec],
        out_specs=(spec, spec, spec),
        out_shape=(shp, shp, shp),
        compiler_params=_params("arbitrary"),
        name="gates",
    )(raw, bias_vec, alog_vec)


def _causal_conv(xpad_ref, w_ref, lo, width, bias):
    acc = bias
    for j in range(CONV_K):
        start = CONV_HALO - (CONV_K - 1) + j
        term = w_ref[j:j + 1, lo:lo + width] * xpad_ref[start:start + CHUNK, lo:lo + width]
        acc = term if acc is None else acc + term
    return acc


def _ssd_body(x_ref, bc_ref, cw_ref, cb_ref, sp_ref, cs_ref, csr_ref, dvec_ref, y_ref,
              xpad_ref, st_ref, *, pad):
    c = pl.program_id(1)

    @pl.when(c == 0)
    def _():
        st_ref[...] = jnp.zeros_like(st_ref)
        xpad_ref[0:CONV_HALO, :] = jnp.zeros((CONV_HALO, xpad_ref.shape[1]), F32)

    xpad_ref[CONV_HALO:CONV_HALO + CHUNK, 0:SSD_D_INNER] = x_ref[0]
    xpad_ref[CONV_HALO:CONV_HALO + CHUNK, SSD_D_INNER:] = bc_ref[0]

    row = lax.broadcasted_iota(jnp.int32, (CHUNK, 1), 0)
    valid = (row + c * CHUNK) >= pad
    ri = lax.broadcasted_iota(jnp.int32, (CHUNK, CHUNK), 0)
    ci = lax.broadcasted_iota(jnp.int32, (CHUNK, CHUNK), 1)
    tril = ri >= ci

    sp = sp_ref[0]
    cs = cs_ref[0]
    csr = csr_ref[0, 0]

    def conv(lo, width):
        a = _causal_conv(xpad_ref, cw_ref, lo, width, cb_ref[:, lo:lo + width])
        return jnp.where(valid, _silu(a), 0.0)

    gw = SSD_HPG * SSD_HEAD_DIM
    groups = range(SSD_GROUPS)
    heads = range(SSD_HEADS)
    xg = [conv(g * gw, gw) for g in groups]
    bg = [conv(SSD_D_INNER + g * SSD_STATE, SSD_STATE).astype(BF16) for g in groups]
    cg = [conv(SSD_D_INNER + SSD_BC + g * SSD_STATE, SSD_STATE).astype(BF16) for g in groups]
    cb = [_dot_nt(cg[g], bg[g]) for g in groups]
    st = [st_ref[g] for g in groups]
    yoff = [_dot_nt(cg[g], st[g].astype(BF16)) for g in groups]

    w_intra, xdt, xw, ecol, elast = [], [], [], [], []
    for h in heads:
        g, r = divmod(h, SSD_HPG)
        col = cs[:, h:h + 1]
        rowv = csr[h:h + 1, :]
        last = rowv[:, CHUNK - 1:CHUNK]
        decay = jnp.where(tril, jnp.exp(col - rowv), 0.0)
        w_intra.append((cb[g] * decay).astype(BF16))
        xd = xg[g][:, r * SSD_HEAD_DIM:(r + 1) * SSD_HEAD_DIM] * sp[:, h:h + 1]
        xdt.append(xd.astype(BF16))
        xw.append((xd * jnp.exp(last - col)).astype(BF16))
        ecol.append(jnp.exp(col))
        elast.append(jnp.exp(last))
    ydiag = [_dot(w_intra[h], xdt[h]) for h in heads]
    snew = [_dot_tn(xw[h], bg[h // SSD_HPG]) for h in heads]
    for h in heads:
        g, r = divmod(h, SSD_HPG)
        lo, hi = r * SSD_HEAD_DIM, (r + 1) * SSD_HEAD_DIM
        st_ref[g, lo:hi, :] = st[g][lo:hi, :] * elast[h] + snew[h]
    for g in groups:
        ys = [ydiag[g * SSD_HPG + r] + yoff[g][:, r * SSD_HEAD_DIM:(r + 1) * SSD_HEAD_DIM] * ecol[g * SSD_HPG + r]
              for r in range(SSD_HPG)]
        y_ref[0, :, g * gw:(g + 1) * gw] = jnp.concatenate(ys, axis=1) + dvec_ref[:, g * gw:(g + 1) * gw] * xg[g]

    xpad_ref[0:CONV_HALO, :] = xpad_ref[CHUNK:CHUNK + CONV_HALO, :]


def _ssd(proj, conv_w, conv_b, sp, cs, cs_rows, d_vec, seq, pad):
    b, lp, _ = proj.shape
    nc = lp // CHUNK
    conv_dim = SSD_D_INNER + 2 * SSD_BC
    x_blk = 3 * GDN_DQK // SSD_D_INNER
    bc_blk = (3 * GDN_DQK + SSD_D_INNER) // (2 * SSD_BC)
    return pl.pallas_call(
        functools.partial(_ssd_body, pad=pad),
        grid=(b, nc),
        in_specs=[
            pl.BlockSpec((1, CHUNK, SSD_D_INNER), lambda i, c: (i, c, x_blk)),
            pl.BlockSpec((1, CHUNK, 2 * SSD_BC), lambda i, c: (i, c, bc_blk)),
            pl.BlockSpec((CONV_K, conv_dim), lambda i, c: (0, 0)),
            pl.BlockSpec((1, conv_dim), lambda i, c: (0, 0)),
            pl.BlockSpec((1, CHUNK, LANES), lambda i, c: (i, c, 0)),
            pl.BlockSpec((1, CHUNK, LANES), lambda i, c: (i, c, 0)),
            pl.BlockSpec((1, 1, SSD_HEADS, CHUNK), lambda i, c: (i, c, 0, 0)),
            pl.BlockSpec((1, SSD_D_INNER), lambda i, c: (0, 0)),
        ],
        out_specs=pl.BlockSpec((1, CHUNK, SSD_D_INNER), lambda i, c: (i, jnp.maximum(c - 1, 0), 0)),
        out_shape=jax.ShapeDtypeStruct((b, seq, SSD_D_INNER), F32),
        scratch_shapes=[
            pltpu.VMEM((CONV_HALO + CHUNK, conv_dim), F32),
            pltpu.VMEM((SSD_GROUPS, SSD_HPG * SSD_HEAD_DIM, SSD_STATE), F32),
        ],
        compiler_params=_params("arbitrary", "arbitrary"),
        name="ssd_scan",
    )(proj, proj, conv_w, conv_b.reshape(1, conv_dim), sp, cs, cs_rows, d_vec)


def _unit_lower_inverses(strict_ls, eye):
    ps = [l.astype(BF16) for l in strict_ls]
    ts = [eye - l for l in strict_ls]
    n = 2
    while n < CHUNK:
        p2 = [_dot(p, p).astype(BF16) for p in ps]
        ts = [t + _dot(t.astype(BF16), q) for t, q in zip(ts, p2)]
        ps = p2
        n *= 2
    return ts


def _gdn_body(q_ref, k_ref, v_ref, cw_ref, cs_ref, bt_ref, gr_ref, nw_ref, o_ref,
              xpad_ref, st_ref, *, pad):
    c = pl.program_id(1)

    @pl.when(c == 0)
    def _():
        st_ref[...] = jnp.zeros_like(st_ref)
        xpad_ref[0:CONV_HALO, :] = jnp.zeros((CONV_HALO, xpad_ref.shape[1]), F32)

    xpad_ref[CONV_HALO:CONV_HALO + CHUNK, 0:GDN_DQK] = q_ref[0]
    xpad_ref[CONV_HALO:CONV_HALO + CHUNK, GDN_DQK:2 * GDN_DQK] = k_ref[0]
    xpad_ref[CONV_HALO:CONV_HALO + CHUNK, 2 * GDN_DQK:] = v_ref[0]

    row = lax.broadcasted_iota(jnp.int32, (CHUNK, 1), 0)
    valid = (row + c * CHUNK) >= pad
    ri = lax.broadcasted_iota(jnp.int32, (CHUNK, CHUNK), 0)
    ci = lax.broadcasted_iota(jnp.int32, (CHUNK, CHUNK), 1)
    tril = ri >= ci
    strict = ri > ci
    eye = (ri == ci).astype(F32)

    cs = cs_ref[0]
    bt = bt_ref[0]
    gr = gr_ref[0, 0]
    nw = nw_ref[...]
    g_off = SSD_HEADS
    b_off = SSD_HEADS + GDN_HEADS

    def conv(lo):
        a = _causal_conv(xpad_ref, cw_ref, lo, GDN_DK, None)
        return jnp.where(valid, _silu(a), 0.0)

    heads = range(GDN_HEADS)
    q16, k16, kb16, rhs16, qg, kd16, decay, elast = [], [], [], [], [], [], [], []
    for h in heads:
        q = conv(h * GDN_DK)
        k = conv(GDN_DQK + h * GDN_DK)
        v = conv(2 * GDN_DQK + h * GDN_DV)
        q = q * lax.rsqrt(jnp.sum(q * q, axis=-1, keepdims=True) + L2_EPS) * (GDN_DK ** -0.5)
        k = k * lax.rsqrt(jnp.sum(k * k, axis=-1, keepdims=True) + L2_EPS)
        col = cs[:, g_off + h:g_off + h + 1]
        rowv = gr[h:h + 1, :]
        last = rowv[:, CHUNK - 1:CHUNK]
        beta = bt[:, b_off + h:b_off + h + 1]
        ecol = jnp.exp(col)
        kb = k * beta
        decay.append(jnp.where(tril, jnp.exp(col - rowv), 0.0))
        q16.append(q.astype(BF16))
        k16.append(k.astype(BF16))
        kb16.append(kb.astype(BF16))
        rhs16.append(jnp.concatenate([v * beta, kb * ecol], axis=1).astype(BF16))
        qg.append(q * ecol)
        kd16.append((k * jnp.exp(last - col)).astype(BF16))
        elast.append(jnp.exp(last))
    kk = [_dot_nt(kb16[h], k16[h]) for h in heads]
    qk = [_dot_nt(q16[h], k16[h]) for h in heads]
    tinv = _unit_lower_inverses([jnp.where(strict, kk[h] * decay[h], 0.0) for h in heads], eye)
    uw = [_dot(tinv[h].astype(BF16), rhs16[h]) for h in heads]
    s = [st_ref[h] for h in heads]
    ws = [_dot(jnp.concatenate([uw[h][:, GDN_DV:], qg[h]], axis=0).astype(BF16), s[h].astype(BF16))
          for h in heads]
    v16 = [(uw[h][:, :GDN_DV] - ws[h][:CHUNK]).astype(BF16) for h in heads]
    o_intra = [_dot((qk[h] * decay[h]).astype(BF16), v16[h]) for h in heads]
    s_add = [_dot_tn(kd16[h], v16[h]) for h in heads]
    for h in heads:
        st_ref[h] = s[h] * elast[h] + s_add[h]
        o = ws[h][CHUNK:] + o_intra[h]
        o = o * lax.rsqrt(jnp.mean(o * o, axis=-1, keepdims=True) + NORM_EPS) * nw
        o_ref[0, :, h * GDN_DV:(h + 1) * GDN_DV] = o

    xpad_ref[0:CONV_HALO, :] = xpad_ref[CHUNK:CHUNK + CONV_HALO, :]


def _gdn(proj, conv_w, cs, bt, g_rows, norm_w, seq, pad):
    b, lp, _ = proj.shape
    nc = lp // CHUNK
    conv_dim = 3 * GDN_DQK
    qkv = lambda blk: pl.BlockSpec((1, CHUNK, GDN_DQK), lambda i, c: (i, c, blk))
    return pl.pallas_call(
        functools.partial(_gdn_body, pad=pad),
        grid=(b, nc),
        in_specs=[
            qkv(0), qkv(1), qkv(2),
            pl.BlockSpec((CONV_K, conv_dim), lambda i, c: (0, 0)),
            pl.BlockSpec((1, CHUNK, LANES), lambda i, c: (i, c, 0)),
            pl.BlockSpec((1, CHUNK, LANES), lambda i, c: (i, c, 0)),
            pl.BlockSpec((1, 1, GDN_HEADS, CHUNK), lambda i, c: (i, c, 0, 0)),
            pl.BlockSpec((1, GDN_DV), lambda i, c: (0, 0)),
        ],
        out_specs=pl.BlockSpec((1, CHUNK, GDN_DQK), lambda i, c: (i, jnp.maximum(c - 1, 0), 0)),
        out_shape=jax.ShapeDtypeStruct((b, seq, GDN_HEADS * GDN_DV), F32),
        scratch_shapes=[
            pltpu.VMEM((CONV_HALO + CHUNK, conv_dim), F32),
            pltpu.VMEM((GDN_HEADS, GDN_DK, GDN_DV), F32),
        ],
        compiler_params=_params("arbitrary", "arbitrary"),
        name="gdn_scan",
    )(proj, proj, proj, conv_w, cs, bt, g_rows, norm_w.reshape(1, GDN_DV))


def _merge_body(y_ref, z_ref, o_ref, gate_ref, g1_ref, g2_ref, nw_ref, w1_ref, w2_ref, m_ref,
                a1_ref, a2_ref):
    @pl.when(pl.program_id(1) == 0)
    def _():
        gw = SSD_D_INNER // SSD_GROUPS
        for g in range(SSD_GROUPS):
            sl = slice(g * gw, (g + 1) * gw)
            t = y_ref[:, sl] * _silu(z_ref[:, sl])
            ms = jnp.mean(t * t, axis=-1, keepdims=True)
            a1_ref[:, sl] = (t * lax.rsqrt(ms + NORM_EPS) * nw_ref[:, sl]).astype(BF16)
        a2_ref[...] = (o_ref[...] * _silu(gate_ref[...])).astype(BF16)

    m_ref[...] = (_sigmoid(g1_ref[...]) * _dot(a1_ref[...], w1_ref[...])
                  + _sigmoid(g2_ref[...]) * _dot(a2_ref[...], w2_ref[...]))


def _merge(y, o, proj_tok, norm_w, w1, w2, tm_pref=256, tn_pref=1024):
    m, d = y.shape
    n = w1.shape[1]
    tm = _divisor_tile(m, tm_pref, 8)
    tn = _divisor_tile(n, tn_pref, LANES)
    nj = n // tn
    row_d = lambda blk: pl.BlockSpec((tm, d), lambda i, j: (i, blk))
    return pl.pallas_call(
        _merge_body,
        grid=(m // tm, nj),
        in_specs=[
            pl.BlockSpec((tm, d), lambda i, j: (i, 0)),
            row_d(0),
            pl.BlockSpec((tm, d), lambda i, j: (i, 0)),
            row_d(1),
            pl.BlockSpec((tm, tn), lambda i, j: (i, 2 * nj + j)),
            pl.BlockSpec((tm, tn), lambda i, j: (i, 3 * nj + j)),
            pl.BlockSpec((1, d), lambda i, j: (0, 0)),
            pl.BlockSpec((d, tn), lambda i, j: (0, j)),
            pl.BlockSpec((d, tn), lambda i, j: (0, j)),
        ],
        out_specs=pl.BlockSpec((tm, tn), lambda i, j: (i, j)),
        out_shape=jax.ShapeDtypeStruct((m, n), F32),
        scratch_shapes=[pltpu.VMEM((tm, d), BF16), pltpu.VMEM((tm, d), BF16)],
        compiler_params=_params("arbitrary", "arbitrary"),
        name="merge",
    )(y, proj_tok, o, proj_tok, proj_tok, proj_tok, norm_w.reshape(1, d), w1, w2)


def _matmul_residual_body(a_ref, w_ref, r_ref, o_ref):
    o_ref[...] = r_ref[...] + _dot(a_ref[...].astype(BF16), w_ref[...])


def _matmul_residual(a, w, res, tm_pref=512, tn_pref=1024):
    m, k = a.shape
    n = w.shape[1]
    tm = _divisor_tile(m, tm_pref, 8)
    tn = _divisor_tile(n, tn_pref, LANES)
    return pl.pallas_call(
        _matmul_residual_body,
        grid=(m // tm, n // tn),
        in_specs=[
            pl.BlockSpec((tm, k), lambda i, j: (i, 0)),
            pl.BlockSpec((k, tn), lambda i, j: (0, j)),
            pl.BlockSpec((tm, tn), lambda i, j: (i, j)),
        ],
        out_specs=pl.BlockSpec((tm, tn), lambda i, j: (i, j)),
        out_shape=jax.ShapeDtypeStruct((m, n), F32),
        compiler_params=_params("arbitrary", "arbitrary"),
        name="out_proj_residual",
    )(a, w, res)


def _top_values(x, count):
    out = []
    for _ in range(count):
        m = jnp.max(x, axis=0, keepdims=True)
        out.append(m)
        x = jnp.where(x == m, -jnp.inf, x)
    return out


def _peer_route_body(q_ref, keys_ref, s2_ref, e2_ref, thr_ref, f1_ref):
    k1 = keys_ref[0].astype(BF16)
    k2 = keys_ref[1].astype(BF16)
    kk = PEER_TOPK
    for h in range(PEER_HEADS):
        lo = 2 * h * PEER_D_KEY
        q1 = q_ref[:, lo:lo + PEER_D_KEY].astype(BF16)
        q2 = q_ref[:, lo + PEER_D_KEY:lo + 2 * PEER_D_KEY].astype(BF16)
        s1 = _dot_nt(k1, q1)
        s2 = _dot_nt(k2, q2)
        t1 = _top_values(s1, kk + 1)
        t2 = _top_values(s2, kk + 1)
        t2_blk = jnp.concatenate(t2[:kk], axis=0)
        cand = jnp.concatenate([t1[a] + t2_blk for a in range(kk)], axis=0)
        best = _top_values(cand, kk + 1)
        runner_up = jnp.maximum(best[kk], jnp.maximum(t1[kk] + t2[0], t1[0] + t2[kk]))
        tau = 0.5 * (best[kk - 1] + runner_up)
        m1, m2 = t1[0], t2[0]
        z = jnp.sum(jnp.where(cand >= tau, jnp.exp(cand - (m1 + m2)), 0.0), axis=0, keepdims=True)
        s2_ref[h] = s2
        e2_ref[h] = jnp.exp(s2 - m2)
        thr_ref[h] = tau - s1
        f1_ref[h] = jnp.exp(s1 - m1) / z


def _peer_route(q, keys, t_tok):
    ntok = q.shape[0]
    spec = pl.BlockSpec((PEER_HEADS, PEER_N_KEYS, t_tok), lambda t: (0, 0, t))
    shp = jax.ShapeDtypeStruct((PEER_HEADS, PEER_N_KEYS, ntok), F32)
    return pl.pallas_call(
        _peer_route_body,
        grid=(ntok // t_tok,),
        in_specs=[
            pl.BlockSpec((t_tok, q.shape[1]), lambda t: (t, 0)),
            pl.BlockSpec(keys.shape, lambda t: (0, 0, 0)),
        ],
        out_specs=(spec, spec, spec, spec),
        out_shape=(shp, shp, shp, shp),
        compiler_params=_params("arbitrary"),
        name="peer_route",
    )(q, keys)


def _peer_expert_body(hnt_ref, s2_ref, e2_ref, thr_ref, f1_ref, down_ref, upt_ref, acc_ref,
                      w_ref, p_ref, *, keys_per_block):
    t_tok = hnt_ref.shape[1]

    @pl.when(pl.program_id(1) == 0)
    def _():
        acc_ref[...] = jnp.zeros_like(acc_ref)

    for s in range(t_tok // LANES):
        ln = slice(s * LANES, (s + 1) * LANES)
        for i in range(keys_per_block):
            w = None
            for hd in range(PEER_HEADS):
                thr = thr_ref[hd, i:i + 1, ln]
                f1 = f1_ref[hd, i:i + 1, ln]
                term = f1 * jnp.where(s2_ref[hd, :, ln] >= thr, e2_ref[hd, :, ln], 0.0)
                w = term if w is None else w + term
            w_ref[i * PEER_N_KEYS:(i + 1) * PEER_N_KEYS, ln] = w

    a = _dot(down_ref[...], hnt_ref[...])
    gelu = 0.5 * a * (1.0 + lax.erf(a * (2.0 ** -0.5)))
    p_ref[...] = (w_ref[...] * gelu).astype(BF16)
    acc_ref[...] += _dot(upt_ref[...], p_ref[...])


def _peer_experts(hn_t, s2, e2, thr, f1, down, up_t, t_tok, keys_per_block):
    d, ntok = hn_t.shape
    n_exp = down.shape[0]
    te = keys_per_block * PEER_N_KEYS
    full = pl.BlockSpec((PEER_HEADS, PEER_N_KEYS, t_tok), lambda t, e: (0, 0, t))
    part = pl.BlockSpec((PEER_HEADS, keys_per_block, t_tok), lambda t, e: (0, e, t))
    return pl.pallas_call(
        functools.partial(_peer_expert_body, keys_per_block=keys_per_block),
        grid=(ntok // t_tok, n_exp // te),
        in_specs=[
            pl.BlockSpec((d, t_tok), lambda t, e: (0, t)),
            full, full, part, part,
            pl.BlockSpec((te, d), lambda t, e: (e, 0)),
            pl.BlockSpec((d, te), lambda t, e: (0, e)),
        ],
        out_specs=pl.BlockSpec((d, t_tok), lambda t, e: (0, t)),
        out_shape=jax.ShapeDtypeStruct((d, ntok), F32),
        scratch_shapes=[
            pltpu.VMEM((te, t_tok), F32),
            pltpu.VMEM((te, t_tok), BF16),
        ],
        compiler_params=_params("arbitrary", "arbitrary"),
        name="peer_experts",
    )(hn_t, s2, e2, thr, f1, down, up_t)


def _norm_transpose_body(x_ref, g_ref, o_ref):
    x = x_ref[...]
    ms = jnp.mean(x * x, axis=-1, keepdims=True)
    o_ref[...] = (x * lax.rsqrt(ms + NORM_EPS) * g_ref[...]).T.astype(BF16)


def _norm_transpose(x2d, gain, tm_pref=256):
    m, d = x2d.shape
    tm = _divisor_tile(m, tm_pref, LANES)
    return pl.pallas_call(
        _norm_transpose_body,
        grid=(m // tm,),
        in_specs=[pl.BlockSpec((tm, d), lambda i: (i, 0)), pl.BlockSpec((1, d), lambda i: (0, 0))],
        out_specs=pl.BlockSpec((d, tm), lambda i: (0, i)),
        out_shape=jax.ShapeDtypeStruct((d, m), BF16),
        compiler_params=_params("arbitrary"),
        name="norm_transpose",
    )(x2d, gain.reshape(1, d))


def _residual_norm_body(h_ref, acc_ref, g_ref, o_ref):
    y = h_ref[...] + acc_ref[...].T
    ms = jnp.mean(y * y, axis=-1, keepdims=True)
    o_ref[...] = y * lax.rsqrt(ms + NORM_EPS) * g_ref[...]


def _residual_norm(h2, acc_t, gain, tm_pref=256):
    m, d = h2.shape
    tm = _divisor_tile(m, tm_pref, LANES)
    return pl.pallas_call(
        _residual_norm_body,
        grid=(m // tm,),
        in_specs=[
            pl.BlockSpec((tm, d), lambda i: (i, 0)),
            pl.BlockSpec((d, tm), lambda i: (0, i)),
            pl.BlockSpec((1, d), lambda i: (0, 0)),
        ],
        out_specs=pl.BlockSpec((tm, d), lambda i: (i, 0)),
        out_shape=jax.ShapeDtypeStruct((m, d), F32),
        compiler_params=_params("arbitrary"),
        name="residual_norm",
    )(h2, acc_t, gain.reshape(1, d))


def _split_w_in(w_in):
    d = w_in.shape[0]
    widths = (SSD_D_INNER, SSD_D_INNER + 2 * SSD_BC, SSD_HEADS, 3 * GDN_DQK, GDN_HEADS * GDN_DV,
              GDN_HEADS, GDN_HEADS, 2 * d)
    offs = [0]
    for wd in widths:
        offs.append(offs[-1] + wd)
    seg = [w_in[:, offs[i]:offs[i + 1]] for i in range(len(widths))]
    z, xbc, dt, qkv, gate, a, beta, merge = seg
    w_scan = jnp.concatenate([qkv, xbc], axis=1).astype(BF16)
    w_tok = jnp.concatenate([z, gate, merge], axis=1).astype(BF16)
    small = jnp.concatenate([dt, a, beta], axis=1)
    w_small = jnp.pad(small, ((0, 0), (0, LANES - small.shape[1]))).astype(BF16)
    return w_scan, w_tok, w_small


def _lane_vec(*parts):
    v = jnp.concatenate([p.astype(F32) for p in parts])
    return jnp.pad(v, (0, LANES - v.shape[0])).reshape(1, LANES)


def _mixer(x, meta_tokens, norm_w, w_in, ssd_conv_w, ssd_conv_b, ssd_dt_bias, ssd_A_log, ssd_D,
           ssd_norm_w, gdn_conv_w, gdn_dt_bias, gdn_A_log, gdn_norm_w, w_branch_ssd, w_branch_gdn, w_out):
    b, seq, d = x.shape
    pad = CHUNK - N_META
    lp = seq + CHUNK
    nc = lp // CHUNK
    w_scan, w_tok, w_small = _split_w_in(w_in)

    meta = jnp.broadcast_to(meta_tokens[None].astype(x.dtype), (b, N_META, d))
    hp = jnp.concatenate([jnp.zeros((b, pad, d), x.dtype), meta, x], axis=1).reshape(b * lp, d)
    x2d = x.reshape(b * seq, d)

    proj_scan = _norm_matmul(hp, norm_w, w_scan).reshape(b, lp, -1)
    raw_small = _norm_matmul(hp, norm_w, w_small).reshape(b, lp, LANES)
    proj_tok = _norm_matmul(x2d, norm_w, w_tok, tm_pref=512)

    zeros_h = jnp.zeros((GDN_HEADS,), F32)
    bias_vec = _lane_vec(ssd_dt_bias, gdn_dt_bias, zeros_h)
    alog_vec = _lane_vec(ssd_A_log, gdn_A_log, zeros_h)
    sp, cs, bt = _gates(raw_small, bias_vec, alog_vec, pad)
    cs_c = cs.reshape(b, nc, CHUNK, LANES)
    ssd_rows = jnp.swapaxes(cs_c[..., :SSD_HEADS], 2, 3)
    gdn_rows = jnp.swapaxes(cs_c[..., SSD_HEADS:SSD_HEADS + GDN_HEADS], 2, 3)

    d_vec = jnp.repeat(ssd_D.astype(F32), SSD_HEAD_DIM).reshape(1, SSD_D_INNER)
    y = _ssd(proj_scan, ssd_conv_w, ssd_conv_b, sp, cs, ssd_rows, d_vec, seq, pad)
    o = _gdn(proj_scan, gdn_conv_w, cs, bt, gdn_rows, gdn_norm_w, seq, pad)

    m = _merge(y.reshape(b * seq, -1), o.reshape(b * seq, -1), proj_tok, ssd_norm_w,
               w_branch_ssd.astype(BF16), w_branch_gdn.astype(BF16))
    return _matmul_residual(m, w_out.astype(BF16), x2d)


def _peer(h2, norm_w, w_query, sub_keys, expert_down, expert_up, final_w):
    ntok = h2.shape[0]
    q = _norm_matmul(h2, norm_w, w_query.astype(BF16), tm_pref=512)
    s2, e2, thr, f1 = _peer_route(q, sub_keys, _divisor_tile(ntok, 256, LANES))
    hn_t = _norm_transpose(h2, norm_w)
    acc_t = _peer_experts(hn_t, s2, e2, thr, f1, expert_down.astype(BF16), expert_up.T.astype(BF16),
                          _divisor_tile(ntok, 512, LANES), keys_per_block=8)
    return _residual_norm(h2, acc_t, final_w)


def kernel(x, meta_tokens, norm_mix_w, w_in, ssd_conv_w, ssd_conv_b, ssd_dt_bias, ssd_A_log, ssd_D, ssd_norm_w, gdn_conv_w, gdn_dt_bias, gdn_A_log, gdn_norm_w, w_branch_ssd, w_branch_gdn, w_out, norm_ffn_w, peer_w_query, peer_sub_keys, peer_expert_down, peer_expert_up, final_norm_w):
    b, seq, d = x.shape
    assert norm_mix_w.shape[0] == 1, "single-layer trunk"
    h2 = _mixer(x, meta_tokens, norm_mix_w[0], w_in[0], ssd_conv_w[0], ssd_conv_b[0], ssd_dt_bias[0],
                ssd_A_log[0], ssd_D[0], ssd_norm_w[0], gdn_conv_w[0], gdn_dt_bias[0], gdn_A_log[0],
                gdn_norm_w[0], w_branch_ssd[0], w_branch_gdn[0], w_out[0])
    out = _peer(h2, norm_ffn_w[0], peer_w_query[0], peer_sub_keys[0], peer_expert_down[0],
                peer_expert_up[0], final_norm_w)
    return out.reshape(b, seq, d)
```

```python
import functools

import jax
import jax.numpy as jnp
from jax import lax
from jax.experimental import pallas as pl
from jax.experimental.pallas import tpu as pltpu

F32 = jnp.float32
BF16 = jnp.bfloat16

N_META = 16
CHUNK = 64
CONV_K = 4
NORM_EPS = 1e-6
L2_EPS = 1e-6

SSD_HEADS = 32
SSD_HEAD_DIM = 64
SSD_GROUPS = 4
SSD_STATE = 128
SSD_HPG = SSD_HEADS // SSD_GROUPS
SSD_D_INNER = SSD_HEADS * SSD_HEAD_DIM
SSD_BC = SSD_GROUPS * SSD_STATE

GDN_HEADS = 16
GDN_DK = 128
GDN_DV = 128
GDN_DQK = GDN_HEADS * GDN_DK

PEER_HEADS = 8
PEER_N_KEYS = 128
PEER_TOPK = 16
PEER_D_KEY = 128

CONV_HALO = 8
LANES = 128
VMEM_LIMIT = 56 * 1024 * 1024

_NT = (((1,), (1,)), ((), ()))
_TN = (((0,), (0,)), ((), ()))


def _dot(a, b):
    return jnp.dot(a, b, preferred_element_type=F32)


def _dot_nt(a, b):
    return lax.dot_general(a, b, _NT, preferred_element_type=F32)


def _dot_tn(a, b):
    return lax.dot_general(a, b, _TN, preferred_element_type=F32)


def _silu(x):
    return x / (1.0 + jnp.exp(-x))


def _sigmoid(x):
    return 1.0 / (1.0 + jnp.exp(-x))


def _softplus(x):
    return jnp.maximum(x, 0.0) + jnp.log(1.0 + jnp.exp(-jnp.abs(x)))


def _divisor_tile(n, preferred, multiple):
    t = min(preferred, n)
    t -= t % multiple
    while t >= multiple:
        if n % t == 0:
            return t
        t -= multiple
    return n


def _params(*sem):
    return pltpu.CompilerParams(dimension_semantics=sem, vmem_limit_bytes=VMEM_LIMIT)


def _norm_matmul_body(x_ref, g_ref, w_ref, o_ref, n_ref):
    @pl.when(pl.program_id(1) == 0)
    def _():
        x = x_ref[...]
        ms = jnp.mean(x * x, axis=-1, keepdims=True)
        n_ref[...] = (x * lax.rsqrt(ms + NORM_EPS) * g_ref[...]).astype(BF16)

    o_ref[...] = _dot(n_ref[...], w_ref[...]).astype(o_ref.dtype)


def _norm_matmul(x2d, gain, w, tm_pref=768, tn_pref=1024, out_dtype=BF16):
    m, k = x2d.shape
    n = w.shape[1]
    tm = _divisor_tile(m, tm_pref, 16)
    tn = _divisor_tile(n, tn_pref, LANES)
    return pl.pallas_call(
        _norm_matmul_body,
        grid=(m // tm, n // tn),
        in_specs=[
            pl.BlockSpec((tm, k), lambda i, j: (i, 0)),
            pl.BlockSpec((1, k), lambda i, j: (0, 0)),
            pl.BlockSpec((k, tn), lambda i, j: (0, j)),
        ],
        out_specs=pl.BlockSpec((tm, tn), lambda i, j: (i, j)),
        out_shape=jax.ShapeDtypeStruct((m, n), out_dtype),
        scratch_shapes=[pltpu.VMEM((tm, k), BF16)],
        compiler_params=_params("arbitrary", "arbitrary"),
        name="norm_matmul",
    )(x2d, gain.reshape(1, k), w)


def _gates_body(raw_ref, bias_ref, alog_ref, sp_ref, cs_ref, bt_ref, *, n_chunks, pad):
    ri = lax.broadcasted_iota(jnp.int32, (CHUNK, CHUNK), 0)
    ci = lax.broadcasted_iota(jnp.int32, (CHUNK, CHUNK), 1)
    tri = (ri >= ci).astype(F32)
    neg_a = -jnp.exp(alog_ref[...])
    bias = bias_ref[...]
    row = lax.broadcasted_iota(jnp.int32, (CHUNK, 1), 0)

    def body(c, carry):
        sl = pl.ds(pl.multiple_of(c * CHUNK, CHUNK), CHUNK)
        r = raw_ref[0, sl, :]
        sp = _softplus(r + bias)
        cs = jnp.dot(tri, sp * neg_a, precision=lax.Precision.HIGHEST, preferred_element_type=F32)
        valid = (row + c * CHUNK) >= pad
        sp_ref[0, sl, :] = sp
        cs_ref[0, sl, :] = cs
        bt_ref[0, sl, :] = jnp.where(valid, _sigmoid(r), 0.0)
        return carry

    lax.fori_loop(0, n_chunks, body, 0)


def _gates(raw, bias_vec, alog_vec, pad):
    b, lp, w = raw.shape
    spec = pl.BlockSpec((1, lp, w), lambda i: (i, 0, 0))
    vec = pl.BlockSpec((1, w), lambda i: (0, 0))
    shp = jax.ShapeDtypeStruct((b, lp, w), F32)
    return pl.pallas_call(
        functools.partial(_gates_body, n_chunks=lp // CHUNK, pad=pad),
        grid=(b,),
        in_specs=[spec, vec, vec],
        out_specs=(spec, spec, spec),
        out_shape=(shp, shp, shp),
        compiler_params=_params("arbitrary"),
        name="gates",
    )(raw, bias_vec, alog_vec)


def _causal_conv(xpad_ref, w_ref, lo, width, bias):
    acc = bias
    for j in range(CONV_K):
        start = CONV_HALO - (CONV_K - 1) + j
        term = w_ref[j:j + 1, lo:lo + width] * xpad_ref[start:start + CHUNK, lo:lo + width]
        acc = term if acc is None else acc + term
    return acc


def _ssd_body(x_ref, bc_ref, cw_ref, cb_ref, sp_ref, cs_ref, csr_ref, dvec_ref, y_ref,
              xpad_ref, st_ref, *, pad):
    c = pl.program_id(1)

    @pl.when(c == 0)
    def _():
        st_ref[...] = jnp.zeros_like(st_ref)
        xpad_ref[0:CONV_HALO, :] = jnp.zeros((CONV_HALO, xpad_ref.shape[1]), F32)

    xpad_ref[CONV_HALO:CONV_HALO + CHUNK, 0:SSD_D_INNER] = x_ref[0].astype(F32)
    xpad_ref[CONV_HALO:CONV_HALO + CHUNK, SSD_D_INNER:] = bc_ref[0].astype(F32)

    row = lax.broadcasted_iota(jnp.int32, (CHUNK, 1), 0)
    valid = (row + c * CHUNK) >= pad
    ri = lax.broadcasted_iota(jnp.int32, (CHUNK, CHUNK), 0)
    ci = lax.broadcasted_iota(jnp.int32, (CHUNK, CHUNK), 1)
    tril = ri >= ci

    sp = sp_ref[0]
    cs = cs_ref[0]
    csr = csr_ref[0, 0]

    def conv(lo, width):
        a = _causal_conv(xpad_ref, cw_ref, lo, width, cb_ref[:, lo:lo + width])
        return jnp.where(valid, _silu(a), 0.0)

    gw = SSD_HPG * SSD_HEAD_DIM
    groups = range(SSD_GROUPS)
    heads = range(SSD_HEADS)
    xg = [conv(g * gw, gw) for g in groups]
    bg = [conv(SSD_D_INNER + g * SSD_STATE, SSD_STATE).astype(BF16) for g in groups]
    cg = [conv(SSD_D_INNER + SSD_BC + g * SSD_STATE, SSD_STATE).astype(BF16) for g in groups]
    cb = [_dot_nt(cg[g], bg[g]) for g in groups]
    st = [st_ref[g] for g in groups]
    yoff = [_dot_nt(cg[g], st[g].astype(BF16)) for g in groups]

    w_intra, xdt, xw, ecol, elast = [], [], [], [], []
    for h in heads:
        g, r = divmod(h, SSD_HPG)
        col = cs[:, h:h + 1]
        rowv = csr[h:h + 1, :]
        last = rowv[:, CHUNK - 1:CHUNK]
        decay = jnp.where(tril, jnp.exp(col - rowv), 0.0)
        w_intra.append((cb[g] * decay).astype(BF16))
        xd = xg[g][:, r * SSD_HEAD_DIM:(r + 1) * SSD_HEAD_DIM] * sp[:, h:h + 1]
        xdt.append(xd.astype(BF16))
        xw.append((xd * jnp.exp(last - col)).astype(BF16))
        ecol.append(jnp.exp(col))
        elast.append(jnp.exp(last))
    ydiag = [_dot(w_intra[h], xdt[h]) for h in heads]
    snew = [_dot_tn(xw[h], bg[h // SSD_HPG]) for h in heads]
    for h in heads:
        g, r = divmod(h, SSD_HPG)
        lo, hi = r * SSD_HEAD_DIM, (r + 1) * SSD_HEAD_DIM
        st_ref[g, lo:hi, :] = st[g][lo:hi, :] * elast[h] + snew[h]
    for g in groups:
        ys = [ydiag[g * SSD_HPG + r] + yoff[g][:, r * SSD_HEAD_DIM:(r + 1) * SSD_HEAD_DIM] * ecol[g * SSD_HPG + r]
              for r in range(SSD_HPG)]
        y_g = jnp.concatenate(ys, axis=1) + dvec_ref[:, g * gw:(g + 1) * gw] * xg[g]
        y_ref[0, :, g * gw:(g + 1) * gw] = y_g.astype(y_ref.dtype)

    xpad_ref[0:CONV_HALO, :] = xpad_ref[CHUNK:CHUNK + CONV_HALO, :]


def _ssd(proj, conv_w, conv_b, sp, cs, cs_rows, d_vec, seq, pad):
    b, lp, _ = proj.shape
    nc = lp // CHUNK
    conv_dim = SSD_D_INNER + 2 * SSD_BC
    x_blk = 3 * GDN_DQK // SSD_D_INNER
    bc_blk = (3 * GDN_DQK + SSD_D_INNER) // (2 * SSD_BC)
    return pl.pallas_call(
        functools.partial(_ssd_body, pad=pad),
        grid=(b, nc),
        in_specs=[
            pl.BlockSpec((1, CHUNK, SSD_D_INNER), lambda i, c: (i, c, x_blk)),
            pl.BlockSpec((1, CHUNK, 2 * SSD_BC), lambda i, c: (i, c, bc_blk)),
            pl.BlockSpec((CONV_K, conv_dim), lambda i, c: (0, 0)),
            pl.BlockSpec((1, conv_dim), lambda i, c: (0, 0)),
            pl.BlockSpec((1, CHUNK, LANES), lambda i, c: (i, c, 0)),
            pl.BlockSpec((1, CHUNK, LANES), lambda i, c: (i, c, 0)),
            pl.BlockSpec((1, 1, SSD_HEADS, CHUNK), lambda i, c: (i, c, 0, 0)),
            pl.BlockSpec((1, SSD_D_INNER), lambda i, c: (0, 0)),
        ],
        out_specs=pl.BlockSpec((1, CHUNK, SSD_D_INNER), lambda i, c: (i, jnp.maximum(c - 1, 0), 0)),
        out_shape=jax.ShapeDtypeStruct((b, seq, SSD_D_INNER), BF16),
        scratch_shapes=[
            pltpu.VMEM((CONV_HALO + CHUNK, conv_dim), F32),
            pltpu.VMEM((SSD_GROUPS, SSD_HPG * SSD_HEAD_DIM, SSD_STATE), F32),
        ],
        compiler_params=_params("arbitrary", "arbitrary"),
        name="ssd_scan",
    )(proj, proj, conv_w, conv_b.reshape(1, conv_dim), sp, cs, cs_rows, d_vec)


def _unit_lower_inverses(strict_ls, eye):
    ps = [l.astype(BF16) for l in strict_ls]
    ts = [eye - l for l in strict_ls]
    n = 2
    while n < CHUNK:
        p2 = [_dot(p, p).astype(BF16) for p in ps]
        ts = [t + _dot(t.astype(BF16), q) for t, q in zip(ts, p2)]
        ps = p2
        n *= 2
    return ts


def _gdn_body(q_ref, k_ref, v_ref, cw_ref, cs_ref, bt_ref, gr_ref, nw_ref, o_ref,
              xpad_ref, st_ref, *, pad):
    c = pl.program_id(1)

    @pl.when(c == 0)
    def _():
        st_ref[...] = jnp.zeros_like(st_ref)
        xpad_ref[0:CONV_HALO, :] = jnp.zeros((CONV_HALO, xpad_ref.shape[1]), F32)

    xpad_ref[CONV_HALO:CONV_HALO + CHUNK, 0:GDN_DQK] = q_ref[0].astype(F32)
    xpad_ref[CONV_HALO:CONV_HALO + CHUNK, GDN_DQK:2 * GDN_DQK] = k_ref[0].astype(F32)
    xpad_ref[CONV_HALO:CONV_HALO + CHUNK, 2 * GDN_DQK:] = v_ref[0].astype(F32)

    row = lax.broadcasted_iota(jnp.int32, (CHUNK, 1), 0)
    valid = (row + c * CHUNK) >= pad
    ri = lax.broadcasted_iota(jnp.int32, (CHUNK, CHUNK), 0)
    ci = lax.broadcasted_iota(jnp.int32, (CHUNK, CHUNK), 1)
    tril = ri >= ci
    strict = ri > ci
    eye = (ri == ci).astype(F32)

    cs = cs_ref[0]
    bt = bt_ref[0]
    gr = gr_ref[0, 0]
    nw = nw_ref[...]
    g_off = SSD_HEADS
    b_off = SSD_HEADS + GDN_HEADS

    def conv(lo):
        a = _causal_conv(xpad_ref, cw_ref, lo, GDN_DK, None)
        return jnp.where(valid, _silu(a), 0.0)

    heads = range(GDN_HEADS)
    q16, k16, kb16, rhs16, qg, kd16, decay, elast = [], [], [], [], [], [], [], []
    for h in heads:
        q = conv(h * GDN_DK)
        k = conv(GDN_DQK + h * GDN_DK)
        v = conv(2 * GDN_DQK + h * GDN_DV)
        q = q * lax.rsqrt(jnp.sum(q * q, axis=-1, keepdims=True) + L2_EPS) * (GDN_DK ** -0.5)
        k = k * lax.rsqrt(jnp.sum(k * k, axis=-1, keepdims=True) + L2_EPS)
        col = cs[:, g_off + h:g_off + h + 1]
        rowv = gr[h:h + 1, :]
        last = rowv[:, CHUNK - 1:CHUNK]
        beta = bt[:, b_off + h:b_off + h + 1]
        ecol = jnp.exp(col)
        kb = k * beta
        decay.append(jnp.where(tril, jnp.exp(col - rowv), 0.0))
        q16.append(q.astype(BF16))
        k16.append(k.astype(BF16))
        kb16.append(kb.astype(BF16))
        rhs16.append(jnp.concatenate([v * beta, kb * ecol], axis=1).astype(BF16))
        qg.append(q * ecol)
        kd16.append((k * jnp.exp(last - col)).astype(BF16))
        elast.append(jnp.exp(last))
    kk = [_dot_nt(kb16[h], k16[h]) for h in heads]
    qk = [_dot_nt(q16[h], k16[h]) for h in heads]
    tinv = _unit_lower_inverses([jnp.where(strict, kk[h] * decay[h], 0.0) for h in heads], eye)
    uw = [_dot(tinv[h].astype(BF16), rhs16[h]) for h in heads]
    s = [st_ref[h] for h in heads]
    ws = [_dot(jnp.concatenate([uw[h][:, GDN_DV:], qg[h]], axis=0).astype(BF16), s[h].astype(BF16))
          for h in heads]
    v16 = [(uw[h][:, :GDN_DV] - ws[h][:CHUNK]).astype(BF16) for h in heads]
    o_intra = [_dot((qk[h] * decay[h]).astype(BF16), v16[h]) for h in heads]
    s_add = [_dot_tn(kd16[h], v16[h]) for h in heads]
    for h in heads:
        st_ref[h] = s[h] * elast[h] + s_add[h]
        o = ws[h][CHUNK:] + o_intra[h]
        o = o * lax.rsqrt(jnp.mean(o * o, axis=-1, keepdims=True) + NORM_EPS) * nw
        o_ref[0, :, h * GDN_DV:(h + 1) * GDN_DV] = o.astype(o_ref.dtype)

    xpad_ref[0:CONV_HALO, :] = xpad_ref[CHUNK:CHUNK + CONV_HALO, :]


def _gdn(proj, conv_w, cs, bt, g_rows, norm_w, seq, pad):
    b, lp, _ = proj.shape
    nc = lp // CHUNK
    conv_dim = 3 * GDN_DQK
    qkv = lambda blk: pl.BlockSpec((1, CHUNK, GDN_DQK), lambda i, c: (i, c, blk))
    return pl.pallas_call(
        functools.partial(_gdn_body, pad=pad),
        grid=(b, nc),
        in_specs=[
            qkv(0), qkv(1), qkv(2),
            pl.BlockSpec((CONV_K, conv_dim), lambda i, c: (0, 0)),
            pl.BlockSpec((1, CHUNK, LANES), lambda i, c: (i, c, 0)),
            pl.BlockSpec((1, CHUNK, LANES), lambda i, c: (i, c, 0)),
            pl.BlockSpec((1, 1, GDN_HEADS, CHUNK), lambda i, c: (i, c, 0, 0)),
            pl.BlockSpec((1, GDN_DV), lambda i, c: (0, 0)),
        ],
        out_specs=pl.BlockSpec((1, CHUNK, GDN_DQK), lambda i, c: (i, jnp.maximum(c - 1, 0), 0)),
        out_shape=jax.ShapeDtypeStruct((b, seq, GDN_HEADS * GDN_DV), BF16),
        scratch_shapes=[
            pltpu.VMEM((CONV_HALO + CHUNK, conv_dim), F32),
            pltpu.VMEM((GDN_HEADS, GDN_DK, GDN_DV), F32),
        ],
        compiler_params=_params("arbitrary", "arbitrary"),
        name="gdn_scan",
    )(proj, proj, proj, conv_w, cs, bt, g_rows, norm_w.reshape(1, GDN_DV))


def _merge_body(y_ref, z_ref, o_ref, gate_ref, g1_ref, g2_ref, nw_ref, w1_ref, w2_ref, m_ref,
                a1_ref, a2_ref):
    @pl.when(pl.program_id(1) == 0)
    def _():
        gw = SSD_D_INNER // SSD_GROUPS
        for g in range(SSD_GROUPS):
            sl = slice(g * gw, (g + 1) * gw)
            t = y_ref[:, sl].astype(F32) * _silu(z_ref[:, sl].astype(F32))
            ms = jnp.mean(t * t, axis=-1, keepdims=True)
            a1_ref[:, sl] = (t * lax.rsqrt(ms + NORM_EPS) * nw_ref[:, sl]).astype(BF16)
        a2_ref[...] = (o_ref[...].astype(F32) * _silu(gate_ref[...].astype(F32))).astype(BF16)

    m = (_sigmoid(g1_ref[...].astype(F32)) * _dot(a1_ref[...], w1_ref[...])
         + _sigmoid(g2_ref[...].astype(F32)) * _dot(a2_ref[...], w2_ref[...]))
    m_ref[...] = m.astype(m_ref.dtype)


def _merge(y, o, proj_tok, norm_w, w1, w2, tm_pref=512, tn_pref=1024):
    m, d = y.shape
    n = w1.shape[1]
    tm = _divisor_tile(m, tm_pref, 16)
    tn = _divisor_tile(n, tn_pref, LANES)
    nj = n // tn
    row_d = lambda blk: pl.BlockSpec((tm, d), lambda i, j: (i, blk))
    return pl.pallas_call(
        _merge_body,
        grid=(m // tm, nj),
        in_specs=[
            pl.BlockSpec((tm, d), lambda i, j: (i, 0)),
            row_d(0),
            pl.BlockSpec((tm, d), lambda i, j: (i, 0)),
            row_d(1),
            pl.BlockSpec((tm, tn), lambda i, j: (i, 2 * nj + j)),
            pl.BlockSpec((tm, tn), lambda i, j: (i, 3 * nj + j)),
            pl.BlockSpec((1, d), lambda i, j: (0, 0)),
            pl.BlockSpec((d, tn), lambda i, j: (0, j)),
            pl.BlockSpec((d, tn), lambda i, j: (0, j)),
        ],
        out_specs=pl.BlockSpec((tm, tn), lambda i, j: (i, j)),
        out_shape=jax.ShapeDtypeStruct((m, n), BF16),
        scratch_shapes=[pltpu.VMEM((tm, d), BF16), pltpu.VMEM((tm, d), BF16)],
        compiler_params=_params("arbitrary", "arbitrary"),
        name="merge",
    )(y, proj_tok, o, proj_tok, proj_tok, proj_tok, norm_w.reshape(1, d), w1, w2)


def _matmul_residual_body(a_ref, w_ref, r_ref, o_ref):
    o_ref[...] = r_ref[...] + _dot(a_ref[...], w_ref[...])


def _matmul_residual(a, w, res, tm_pref=512, tn_pref=1024):
    m, k = a.shape
    n = w.shape[1]
    tm = _divisor_tile(m, tm_pref, 8)
    tn = _divisor_tile(n, tn_pref, LANES)
    return pl.pallas_call(
        _matmul_residual_body,
        grid=(m // tm, n // tn),
        in_specs=[
            pl.BlockSpec((tm, k), lambda i, j: (i, 0)),
            pl.BlockSpec((k, tn), lambda i, j: (0, j)),
            pl.BlockSpec((tm, tn), lambda i, j: (i, j)),
        ],
        out_specs=pl.BlockSpec((tm, tn), lambda i, j: (i, j)),
        out_shape=jax.ShapeDtypeStruct((m, n), F32),
        compiler_params=_params("arbitrary", "arbitrary"),
        name="out_proj_residual",
    )(a, w, res)


def _top_values(x, count):
    out = []
    for _ in range(count):
        m = jnp.max(x, axis=0, keepdims=True)
        out.append(m)
        x = jnp.where(x == m, -jnp.inf, x)
    return out


def _peer_route_body(q_ref, keys_ref, s2_ref, e2_ref, thr_ref, f1_ref):
    k1 = keys_ref[0].astype(BF16)
    k2 = keys_ref[1].astype(BF16)
    kk = PEER_TOPK
    for h in range(PEER_HEADS):
        lo = 2 * h * PEER_D_KEY
        q1 = q_ref[:, lo:lo + PEER_D_KEY]
        q2 = q_ref[:, lo + PEER_D_KEY:lo + 2 * PEER_D_KEY]
        s1 = _dot_nt(k1, q1)
        s2 = _dot_nt(k2, q2)
        t1 = _top_values(s1, kk + 1)
        t2 = _top_values(s2, kk + 1)
        t2_blk = jnp.concatenate(t2[:kk], axis=0)
        cand = jnp.concatenate([t1[a] + t2_blk for a in range(kk)], axis=0)
        best = _top_values(cand, kk + 1)
        runner_up = jnp.maximum(best[kk], jnp.maximum(t1[kk] + t2[0], t1[0] + t2[kk]))
        tau = 0.5 * (best[kk - 1] + runner_up)
        m1, m2 = t1[0], t2[0]
        z = jnp.sum(jnp.where(cand >= tau, jnp.exp(cand - (m1 + m2)), 0.0), axis=0, keepdims=True)
        e2 = jnp.exp(s2 - m2)
        for s in range(s2_ref.shape[1]):
            ln = slice(s * LANES, (s + 1) * LANES)
            s2_ref[h, s] = s2[:, ln]
            e2_ref[h, s] = e2[:, ln]
        thr_ref[h] = tau - s1
        f1_ref[h] = jnp.exp(s1 - m1) / z


def _peer_route(q, keys, t_tok):
    ntok = q.shape[0]
    spec = pl.BlockSpec((PEER_HEADS, PEER_N_KEYS, t_tok), lambda t: (0, 0, t))
    shp = jax.ShapeDtypeStruct((PEER_HEADS, PEER_N_KEYS, ntok), F32)
    strip_spec = pl.BlockSpec((PEER_HEADS, t_tok // LANES, PEER_N_KEYS, LANES), lambda t: (0, t, 0, 0))
    strip_shp = jax.ShapeDtypeStruct((PEER_HEADS, ntok // LANES, PEER_N_KEYS, LANES), F32)
    return pl.pallas_call(
        _peer_route_body,
        grid=(ntok // t_tok,),
        in_specs=[
            pl.BlockSpec((t_tok, q.shape[1]), lambda t: (t, 0)),
            pl.BlockSpec(keys.shape, lambda t: (0, 0, 0)),
        ],
        out_specs=(strip_spec, strip_spec, spec, spec),
        out_shape=(strip_shp, strip_shp, shp, shp),
        compiler_params=_params("arbitrary"),
        name="peer_route",
    )(q, keys)


def _peer_expert_body(hnt_ref, s2_ref, e2_ref, thr_ref, f1_ref, down_ref, upt_ref, acc_ref,
                      w_ref, act_ref, p_ref, *, keys_per_block):
    d, t_tok = hnt_ref.shape
    te = down_ref.shape[0]
    n_part = t_tok // LANES
    rows_a = te // n_part
    rows_o = d // n_part
    e = pl.program_id(1)
    last = pl.num_programs(1) - 1
    slot = lax.rem(e, 2)

    def run(build, project):
        def part(s, carry):
            if project:
                ro = pl.ds(s * rows_o, rows_o)
                acc_ref[ro, :] += _dot(upt_ref[ro, :], p_ref[1 - slot])
            if build:
                ln = pl.ds(s * LANES, LANES)
                for i in range(keys_per_block):
                    w = None
                    for hd in range(PEER_HEADS):
                        thr = thr_ref[hd, i:i + 1, ln]
                        f1 = f1_ref[hd, i:i + 1, ln]
                        term = f1 * jnp.where(s2_ref[hd, s] >= thr, e2_ref[hd, s], 0.0)
                        w = term if w is None else w + term
                    w_ref[i * PEER_N_KEYS:(i + 1) * PEER_N_KEYS, ln] = w
                ra = pl.ds(s * rows_a, rows_a)
                act_ref[ra, :] = _dot(down_ref[ra, :], hnt_ref[...])
            return carry

        for s in range(n_part):
            part(s, 0)
        if build:
            a = act_ref[...]
            gelu = 0.5 * a * (1.0 + lax.erf(a * (2.0 ** -0.5)))
            p_ref[slot] = (w_ref[...] * gelu).astype(BF16)

    @pl.when(e == 0)
    def _():
        acc_ref[...] = jnp.zeros_like(acc_ref)
        run(build=True, project=False)

    @pl.when(jnp.logical_and(e > 0, e < last))
    def _():
        run(build=True, project=True)

    @pl.when(e == last)
    def _():
        run(build=False, project=True)


def _peer_experts(hn_t, s2, e2, thr, f1, down, up_t, t_tok, keys_per_block):
    d, ntok = hn_t.shape
    n_exp = down.shape[0]
    te = keys_per_block * PEER_N_KEYS
    n_blk = n_exp // te
    cur = lambda e: jnp.minimum(e, n_blk - 1)
    prev = lambda e: jnp.maximum(e - 1, 0)
    full = pl.BlockSpec((PEER_HEADS, t_tok // LANES, PEER_N_KEYS, LANES), lambda t, e: (0, t, 0, 0))
    part = pl.BlockSpec((PEER_HEADS, keys_per_block, t_tok), lambda t, e: (0, cur(e), t))
    return pl.pallas_call(
        functools.partial(_peer_expert_body, keys_per_block=keys_per_block),
        grid=(ntok // t_tok, n_blk + 1),
        in_specs=[
            pl.BlockSpec((d, t_tok), lambda t, e: (0, t)),
            full, full, part, part,
            pl.BlockSpec((te, d), lambda t, e: (cur(e), 0)),
            pl.BlockSpec((d, te), lambda t, e: (0, prev(e))),
        ],
        out_specs=pl.BlockSpec((d, t_tok), lambda t, e: (0, t)),
        out_shape=jax.ShapeDtypeStruct((d, ntok), F32),
        scratch_shapes=[
            pltpu.VMEM((te, t_tok), F32),
            pltpu.VMEM((te, t_tok), F32),
            pltpu.VMEM((2, te, t_tok), BF16),
        ],
        compiler_params=_params("arbitrary", "arbitrary"),
        name="peer_experts",
    )(hn_t, s2, e2, thr, f1, down, up_t)


def _norm_transpose_body(x_ref, g_ref, o_ref):
    x = x_ref[...]
    ms = jnp.mean(x * x, axis=-1, keepdims=True)
    o_ref[...] = (x * lax.rsqrt(ms + NORM_EPS) * g_ref[...]).T.astype(BF16)


def _norm_transpose(x2d, gain, tm_pref=256):
    m, d = x2d.shape
    tm = _divisor_tile(m, tm_pref, LANES)
    return pl.pallas_call(
        _norm_transpose_body,
        grid=(m // tm,),
        in_specs=[pl.BlockSpec((tm, d), lambda i: (i, 0)), pl.BlockSpec((1, d), lambda i: (0, 0))],
        out_specs=pl.BlockSpec((d, tm), lambda i: (0, i)),
        out_shape=jax.ShapeDtypeStruct((d, m), BF16),
        compiler_params=_params("arbitrary"),
        name="norm_transpose",
    )(x2d, gain.reshape(1, d))


def _residual_norm_body(h_ref, acc_ref, g_ref, o_ref):
    y = h_ref[...] + acc_ref[...].T
    ms = jnp.mean(y * y, axis=-1, keepdims=True)
    o_ref[...] = y * lax.rsqrt(ms + NORM_EPS) * g_ref[...]


def _residual_norm(h2, acc_t, gain, tm_pref=256):
    m, d = h2.shape
    tm = _divisor_tile(m, tm_pref, LANES)
    return pl.pallas_call(
        _residual_norm_body,
        grid=(m // tm,),
        in_specs=[
            pl.BlockSpec((tm, d), lambda i: (i, 0)),
            pl.BlockSpec((d, tm), lambda i: (0, i)),
            pl.BlockSpec((1, d), lambda i: (0, 0)),
        ],
        out_specs=pl.BlockSpec((tm, d), lambda i: (i, 0)),
        out_shape=jax.ShapeDtypeStruct((m, d), F32),
        compiler_params=_params("arbitrary"),
        name="residual_norm",
    )(h2, acc_t, gain.reshape(1, d))


def _split_w_in(w_in):
    d = w_in.shape[0]
    widths = (SSD_D_INNER, SSD_D_INNER + 2 * SSD_BC, SSD_HEADS, 3 * GDN_DQK, GDN_HEADS * GDN_DV,
              GDN_HEADS, GDN_HEADS, 2 * d)
    offs = [0]
    for wd in widths:
        offs.append(offs[-1] + wd)
    seg = [w_in[:, offs[i]:offs[i + 1]] for i in range(len(widths))]
    z, xbc, dt, qkv, gate, a, beta, merge = seg
    w_scan = jnp.concatenate([qkv, xbc], axis=1).astype(BF16)
    w_tok = jnp.concatenate([z, gate, merge], axis=1).astype(BF16)
    small = jnp.concatenate([dt, a, beta], axis=1)
    w_small = jnp.pad(small, ((0, 0), (0, LANES - small.shape[1]))).astype(BF16)
    return w_scan, w_tok, w_small


def _lane_vec(*parts):
    v = jnp.concatenate([p.astype(F32) for p in parts])
    return jnp.pad(v, (0, LANES - v.shape[0])).reshape(1, LANES)


def _mixer(x, meta_tokens, norm_w, w_in, ssd_conv_w, ssd_conv_b, ssd_dt_bias, ssd_A_log, ssd_D,
           ssd_norm_w, gdn_conv_w, gdn_dt_bias, gdn_A_log, gdn_norm_w, w_branch_ssd, w_branch_gdn, w_out):
    b, seq, d = x.shape
    pad = CHUNK - N_META
    lp = seq + CHUNK
    nc = lp // CHUNK
    w_scan, w_tok, w_small = _split_w_in(w_in)

    meta = jnp.broadcast_to(meta_tokens[None].astype(x.dtype), (b, N_META, d))
    hp = jnp.concatenate([jnp.zeros((b, pad, d), x.dtype), meta, x], axis=1).reshape(b * lp, d)
    x2d = x.reshape(b * seq, d)

    proj_scan = _norm_matmul(hp, norm_w, w_scan).reshape(b, lp, -1)
    raw_small = _norm_matmul(hp, norm_w, w_small, out_dtype=F32).reshape(b, lp, LANES)
    proj_tok = _norm_matmul(x2d, norm_w, w_tok, tm_pref=512)

    zeros_h = jnp.zeros((GDN_HEADS,), F32)
    bias_vec = _lane_vec(ssd_dt_bias, gdn_dt_bias, zeros_h)
    alog_vec = _lane_vec(ssd_A_log, gdn_A_log, zeros_h)
    sp, cs, bt = _gates(raw_small, bias_vec, alog_vec, pad)
    cs_c = cs.reshape(b, nc, CHUNK, LANES)
    ssd_rows = jnp.swapaxes(cs_c[..., :SSD_HEADS], 2, 3)
    gdn_rows = jnp.swapaxes(cs_c[..., SSD_HEADS:SSD_HEADS + GDN_HEADS], 2, 3)

    d_vec = jnp.repeat(ssd_D.astype(F32), SSD_HEAD_DIM).reshape(1, SSD_D_INNER)
    y = _ssd(proj_scan, ssd_conv_w, ssd_conv_b, sp, cs, ssd_rows, d_vec, seq, pad)
    o = _gdn(proj_scan, gdn_conv_w, cs, bt, gdn_rows, gdn_norm_w, seq, pad)

    m = _merge(y.reshape(b * seq, -1), o.reshape(b * seq, -1), proj_tok, ssd_norm_w,
               w_branch_ssd.astype(BF16), w_branch_gdn.astype(BF16))
    return _matmul_residual(m, w_out.astype(BF16), x2d)


def _peer(h2, norm_w, w_query, sub_keys, expert_down, expert_up, final_w):
    ntok = h2.shape[0]
    q = _norm_matmul(h2, norm_w, w_query.astype(BF16), tm_pref=512)
    s2, e2, thr, f1 = _peer_route(q, sub_keys, _divisor_tile(ntok, 256, LANES))
    hn_t = _norm_transpose(h2, norm_w)
    acc_t = _peer_experts(hn_t, s2, e2, thr, f1, expert_down.astype(BF16), expert_up.astype(BF16).T,
                          _divisor_tile(ntok, 512, LANES), keys_per_block=8)
    return _residual_norm(h2, acc_t, final_w)


def kernel(x, meta_tokens, norm_mix_w, w_in, ssd_conv_w, ssd_conv_b, ssd_dt_bias, ssd_A_log, ssd_D, ssd_norm_w, gdn_conv_w, gdn_dt_bias, gdn_A_log, gdn_norm_w, w_branch_ssd, w_branch_gdn, w_out, norm_ffn_w, peer_w_query, peer_sub_keys, peer_expert_down, peer_expert_up, final_norm_w):
    b, seq, d = x.shape
    assert norm_mix_w.shape[0] == 1, "single-layer trunk"
    h2 = _mixer(x, meta_tokens, norm_mix_w[0], w_in[0], ssd_conv_w[0], ssd_conv_b[0], ssd_dt_bias[0],
                ssd_A_log[0], ssd_D[0], ssd_norm_w[0], gdn_conv_w[0], gdn_dt_bias[0], gdn_A_log[0],
                gdn_norm_w[0], w_branch_ssd[0], w_branch_gdn[0], w_out[0])
    out = _peer(h2, norm_ffn_w[0], peer_w_query[0], peer_sub_keys[0], peer_expert_down[0],
                peer_expert_up[0], final_norm_w)
    return out.reshape(b, seq, d)
```

```python
import functools

import jax
import jax.numpy as jnp
from jax import lax
from jax.experimental import pallas as pl
from jax.experimental.pallas import tpu as pltpu

F32 = jnp.float32
BF16 = jnp.bfloat16

N_META = 16
CHUNK = 64
CONV_K = 4
NORM_EPS = 1e-6
L2_EPS = 1e-6

SSD_HEADS = 32
SSD_HEAD_DIM = 64
SSD_GROUPS = 4
SSD_STATE = 128
SSD_HPG = SSD_HEADS // SSD_GROUPS
SSD_D_INNER = SSD_HEADS * SSD_HEAD_DIM
SSD_BC = SSD_GROUPS * SSD_STATE

GDN_HEADS = 16
GDN_DK = 128
GDN_DV = 128
GDN_DQK = GDN_HEADS * GDN_DK

PEER_HEADS = 8
PEER_N_KEYS = 128
PEER_TOPK = 16
PEER_D_KEY = 128

CONV_HALO = 8
LANES = 128
VMEM_LIMIT = 56 * 1024 * 1024

_NT = (((1,), (1,)), ((), ()))
_TN = (((0,), (0,)), ((), ()))


def _dot(a, b):
    return jnp.dot(a, b, preferred_element_type=F32)


def _dot_nt(a, b):
    return lax.dot_general(a, b, _NT, preferred_element_type=F32)


def _dot_tn(a, b):
    return lax.dot_general(a, b, _TN, preferred_element_type=F32)


def _silu(x):
    return x / (1.0 + jnp.exp(-x))


def _sigmoid(x):
    return 1.0 / (1.0 + jnp.exp(-x))


def _softplus(x):
    return jnp.maximum(x, 0.0) + jnp.log(1.0 + jnp.exp(-jnp.abs(x)))


def _divisor_tile(n, preferred, multiple):
    t = min(preferred, n)
    t -= t % multiple
    while t >= multiple:
        if n % t == 0:
            return t
        t -= multiple
    return n


def _params(*sem):
    return pltpu.CompilerParams(dimension_semantics=sem, vmem_limit_bytes=VMEM_LIMIT)


def _norm_matmul_body(x_ref, g_ref, w_ref, o_ref, n_ref):
    @pl.when(pl.program_id(1) == 0)
    def _():
        x = x_ref[...]
        ms = jnp.mean(x * x, axis=-1, keepdims=True)
        n_ref[...] = (x * lax.rsqrt(ms + NORM_EPS) * g_ref[...]).astype(BF16)

    o_ref[...] = _dot(n_ref[...], w_ref[...]).astype(o_ref.dtype)


def _norm_matmul(x2d, gain, w, tm_pref=768, tn_pref=1024, out_dtype=BF16):
    m, k = x2d.shape
    n = w.shape[1]
    tm = _divisor_tile(m, tm_pref, 16)
    tn = _divisor_tile(n, tn_pref, LANES)
    return pl.pallas_call(
        _norm_matmul_body,
        grid=(m // tm, n // tn),
        in_specs=[
            pl.BlockSpec((tm, k), lambda i, j: (i, 0)),
            pl.BlockSpec((1, k), lambda i, j: (0, 0)),
            pl.BlockSpec((k, tn), lambda i, j: (0, j)),
        ],
        out_specs=pl.BlockSpec((tm, tn), lambda i, j: (i, j)),
        out_shape=jax.ShapeDtypeStruct((m, n), out_dtype),
        scratch_shapes=[pltpu.VMEM((tm, k), BF16)],
        compiler_params=_params("arbitrary", "arbitrary"),
        name="norm_matmul",
    )(x2d, gain.reshape(1, k), w)


def _gates_body(raw_ref, bias_ref, alog_ref, sp_ref, cs_ref, bt_ref, *, n_chunks, pad):
    ri = lax.broadcasted_iota(jnp.int32, (CHUNK, CHUNK), 0)
    ci = lax.broadcasted_iota(jnp.int32, (CHUNK, CHUNK), 1)
    tri = (ri >= ci).astype(F32)
    neg_a = -jnp.exp(alog_ref[...])
    bias = bias_ref[...]
    row = lax.broadcasted_iota(jnp.int32, (CHUNK, 1), 0)

    def body(c, carry):
        sl = pl.ds(pl.multiple_of(c * CHUNK, CHUNK), CHUNK)
        r = raw_ref[0, sl, :]
        sp = _softplus(r + bias)
        cs = jnp.dot(tri, sp * neg_a, precision=lax.Precision.HIGHEST, preferred_element_type=F32)
        valid = (row + c * CHUNK) >= pad
        sp_ref[0, sl, :] = sp
        cs_ref[0, sl, :] = cs
        bt_ref[0, sl, :] = jnp.where(valid, _sigmoid(r), 0.0)
        return carry

    lax.fori_loop(0, n_chunks, body, 0)


def _gates(raw, bias_vec, alog_vec, pad):
    b, lp, w = raw.shape
    spec = pl.BlockSpec((1, lp, w), lambda i: (i, 0, 0))
    vec = pl.BlockSpec((1, w), lambda i: (0, 0))
    shp = jax.ShapeDtypeStruct((b, lp, w), F32)
    return pl.pallas_call(
        functools.partial(_gates_body, n_chunks=lp // CHUNK, pad=pad),
        grid=(b,),
        in_specs=[spec, vec, vec],
        out_specs=(spec, spec, spec),
        out_shape=(shp, shp, shp),
        compiler_params=_params("arbitrary"),
        name="gates",
    )(raw, bias_vec, alog_vec)


def _causal_conv(xpad_ref, w_ref, lo, width, bias):
    acc = bias
    for j in range(CONV_K):
        start = CONV_HALO - (CONV_K - 1) + j
        term = w_ref[j:j + 1, lo:lo + width] * xpad_ref[start:start + CHUNK, lo:lo + width]
        acc = term if acc is None else acc + term
    return acc


def _ssd_body(x_ref, bc_ref, cw_ref, cb_ref, sp_ref, cs_ref, csr_ref, dvec_ref, expand_ref, y_ref,
              xpad_ref, st_ref, *, pad):
    c = pl.program_id(1)

    @pl.when(c == 0)
    def _():
        st_ref[...] = jnp.zeros_like(st_ref)
        xpad_ref[0:CONV_HALO, :] = jnp.zeros((CONV_HALO, xpad_ref.shape[1]), F32)

    xpad_ref[CONV_HALO:CONV_HALO + CHUNK, 0:SSD_D_INNER] = x_ref[0].astype(F32)
    xpad_ref[CONV_HALO:CONV_HALO + CHUNK, SSD_D_INNER:] = bc_ref[0].astype(F32)

    row = lax.broadcasted_iota(jnp.int32, (CHUNK, 1), 0)
    valid = (row + c * CHUNK) >= pad
    ri = lax.broadcasted_iota(jnp.int32, (CHUNK, CHUNK), 0)
    ci = lax.broadcasted_iota(jnp.int32, (CHUNK, CHUNK), 1)
    tril = ri >= ci

    sp = sp_ref[0]
    cs = cs_ref[0]
    csr = csr_ref[0, 0]

    def conv(lo, width):
        a = _causal_conv(xpad_ref, cw_ref, lo, width, cb_ref[:, lo:lo + width])
        return jnp.where(valid, _silu(a), 0.0)

    gw = SSD_HPG * SSD_HEAD_DIM
    groups = range(SSD_GROUPS)
    heads = range(SSD_HEADS)
    xg = [conv(g * gw, gw) for g in groups]
    bg = [conv(SSD_D_INNER + g * SSD_STATE, SSD_STATE).astype(BF16) for g in groups]
    cg = [conv(SSD_D_INNER + SSD_BC + g * SSD_STATE, SSD_STATE).astype(BF16) for g in groups]
    cb = [_dot_nt(cg[g], bg[g]) for g in groups]
    st = [st_ref[g] for g in groups]
    yoff = [_dot(cg[g], st[g].astype(BF16)) for g in groups]

    last_row = cs[CHUNK - 1:CHUNK, :]
    factors = (sp, sp * jnp.exp(last_row - cs), jnp.exp(cs))
    pieces = []
    for f in factors:
        hi = f.astype(BF16)
        pieces += [hi, (f - hi.astype(F32)).astype(BF16)]
    spread = _dot(jnp.concatenate(pieces, axis=0), expand_ref[...])
    e_dt, e_w, e_cs = (spread[(2 * k) * CHUNK:(2 * k + 1) * CHUNK] + spread[(2 * k + 1) * CHUNK:(2 * k + 2) * CHUNK]
                       for k in range(len(factors)))

    xdt = [(xg[g] * e_dt[:, g * gw:(g + 1) * gw]).astype(BF16) for g in groups]
    xw = [(xg[g] * e_w[:, g * gw:(g + 1) * gw]).astype(BF16) for g in groups]
    w_intra = []
    for h in heads:
        decay = jnp.where(tril, jnp.exp(cs[:, h:h + 1] - csr[h:h + 1, :]), 0.0)
        w_intra.append((cb[h // SSD_HPG] * decay).astype(BF16))
    ydiag = [_dot(w_intra[h], xdt[h // SSD_HPG][:, (h % SSD_HPG) * SSD_HEAD_DIM:(h % SSD_HPG + 1) * SSD_HEAD_DIM])
             for h in heads]
    snew = [_dot_tn(bg[g], xw[g]) for g in groups]
    for g in groups:
        sl = slice(g * gw, (g + 1) * gw)
        st_ref[g] = st[g] * e_cs[CHUNK - 1:CHUNK, sl] + snew[g]
        y_g = (jnp.concatenate(ydiag[g * SSD_HPG:(g + 1) * SSD_HPG], axis=1) + yoff[g] * e_cs[:, sl]
               + dvec_ref[:, sl] * xg[g])
        y_ref[0, :, sl] = y_g.astype(y_ref.dtype)

    xpad_ref[0:CONV_HALO, :] = xpad_ref[CHUNK:CHUNK + CONV_HALO, :]


def _ssd(proj, conv_w, conv_b, sp, cs, cs_rows, d_vec, seq, pad):
    b, lp, _ = proj.shape
    nc = lp // CHUNK
    conv_dim = SSD_D_INNER + 2 * SSD_BC
    x_blk = 3 * GDN_DQK // SSD_D_INNER
    bc_blk = (3 * GDN_DQK + SSD_D_INNER) // (2 * SSD_BC)
    expand = (jnp.arange(LANES)[:, None] == jnp.arange(SSD_D_INNER)[None, :] // SSD_HEAD_DIM).astype(BF16)
    return pl.pallas_call(
        functools.partial(_ssd_body, pad=pad),
        grid=(b, nc),
        in_specs=[
            pl.BlockSpec((1, CHUNK, SSD_D_INNER), lambda i, c: (i, c, x_blk)),
            pl.BlockSpec((1, CHUNK, 2 * SSD_BC), lambda i, c: (i, c, bc_blk)),
            pl.BlockSpec((CONV_K, conv_dim), lambda i, c: (0, 0)),
            pl.BlockSpec((1, conv_dim), lambda i, c: (0, 0)),
            pl.BlockSpec((1, CHUNK, LANES), lambda i, c: (i, c, 0)),
            pl.BlockSpec((1, CHUNK, LANES), lambda i, c: (i, c, 0)),
            pl.BlockSpec((1, 1, SSD_HEADS, CHUNK), lambda i, c: (i, c, 0, 0)),
            pl.BlockSpec((1, SSD_D_INNER), lambda i, c: (0, 0)),
            pl.BlockSpec((LANES, SSD_D_INNER), lambda i, c: (0, 0)),
        ],
        out_specs=pl.BlockSpec((1, CHUNK, SSD_D_INNER), lambda i, c: (i, jnp.maximum(c - 1, 0), 0)),
        out_shape=jax.ShapeDtypeStruct((b, seq, SSD_D_INNER), BF16),
        scratch_shapes=[
            pltpu.VMEM((CONV_HALO + CHUNK, conv_dim), F32),
            pltpu.VMEM((SSD_GROUPS, SSD_STATE, SSD_HPG * SSD_HEAD_DIM), F32),
        ],
        compiler_params=_params("arbitrary", "arbitrary"),
        name="ssd_scan",
    )(proj, proj, conv_w, conv_b.reshape(1, conv_dim), sp, cs, cs_rows, d_vec, expand)


def _unit_lower_inverses(strict_ls, eye):
    ps = [l.astype(BF16) for l in strict_ls]
    ts = [eye - l for l in strict_ls]
    n = 2
    while n < CHUNK:
        p2 = [_dot(p, p).astype(BF16) for p in ps]
        ts = [t + _dot(t.astype(BF16), q) for t, q in zip(ts, p2)]
        ps = p2
        n *= 2
    return ts


def _gdn_body(q_ref, k_ref, v_ref, cw_ref, cs_ref, bt_ref, gr_ref, nw_ref, o_ref,
              xpad_ref, st_ref, *, pad):
    c = pl.program_id(1)

    @pl.when(c == 0)
    def _():
        st_ref[...] = jnp.zeros_like(st_ref)
        xpad_ref[0:CONV_HALO, :] = jnp.zeros((CONV_HALO, xpad_ref.shape[1]), F32)

    xpad_ref[CONV_HALO:CONV_HALO + CHUNK, 0:GDN_DQK] = q_ref[0].astype(F32)
    xpad_ref[CONV_HALO:CONV_HALO + CHUNK, GDN_DQK:2 * GDN_DQK] = k_ref[0].astype(F32)
    xpad_ref[CONV_HALO:CONV_HALO + CHUNK, 2 * GDN_DQK:] = v_ref[0].astype(F32)

    row = lax.broadcasted_iota(jnp.int32, (CHUNK, 1), 0)
    valid = (row + c * CHUNK) >= pad
    ri = lax.broadcasted_iota(jnp.int32, (CHUNK, CHUNK), 0)
    ci = lax.broadcasted_iota(jnp.int32, (CHUNK, CHUNK), 1)
    tril = ri >= ci
    strict = ri > ci
    eye = (ri == ci).astype(F32)

    cs = cs_ref[0]
    bt = bt_ref[0]
    gr = gr_ref[0, 0]
    nw = nw_ref[...]
    g_off = SSD_HEADS
    b_off = SSD_HEADS + GDN_HEADS

    def conv(lo):
        a = _causal_conv(xpad_ref, cw_ref, lo, GDN_DK, None)
        return jnp.where(valid, _silu(a), 0.0)

    heads = range(GDN_HEADS)
    q16, k16, kb16, rhs16, qg, kd16, decay, elast = [], [], [], [], [], [], [], []
    for h in heads:
        q = conv(h * GDN_DK)
        k = conv(GDN_DQK + h * GDN_DK)
        v = conv(2 * GDN_DQK + h * GDN_DV)
        q = q * lax.rsqrt(jnp.sum(q * q, axis=-1, keepdims=True) + L2_EPS) * (GDN_DK ** -0.5)
        k = k * lax.rsqrt(jnp.sum(k * k, axis=-1, keepdims=True) + L2_EPS)
        col = cs[:, g_off + h:g_off + h + 1]
        rowv = gr[h:h + 1, :]
        last = rowv[:, CHUNK - 1:CHUNK]
        beta = bt[:, b_off + h:b_off + h + 1]
        ecol = jnp.exp(col)
        kb = k * beta
        decay.append(jnp.where(tril, jnp.exp(col - rowv), 0.0))
        q16.append(q.astype(BF16))
        k16.append(k.astype(BF16))
        kb16.append(kb.astype(BF16))
        rhs16.append(jnp.concatenate([v * beta, kb * ecol], axis=1).astype(BF16))
        qg.append(q * ecol)
        kd16.append((k * jnp.exp(last - col)).astype(BF16))
        elast.append(jnp.exp(last))
    kk = [_dot_nt(kb16[h], k16[h]) for h in heads]
    qk = [_dot_nt(q16[h], k16[h]) for h in heads]
    tinv = _unit_lower_inverses([jnp.where(strict, kk[h] * decay[h], 0.0) for h in heads], eye)
    uw = [_dot(tinv[h].astype(BF16), rhs16[h]) for h in heads]
    s = [st_ref[h] for h in heads]
    ws = [_dot(jnp.concatenate([uw[h][:, GDN_DV:], qg[h]], axis=0).astype(BF16), s[h].astype(BF16))
          for h in heads]
    v16 = [(uw[h][:, :GDN_DV] - ws[h][:CHUNK]).astype(BF16) for h in heads]
    o_intra = [_dot((qk[h] * decay[h]).astype(BF16), v16[h]) for h in heads]
    s_add = [_dot_tn(kd16[h], v16[h]) for h in heads]
    for h in heads:
        st_ref[h] = s[h] * elast[h] + s_add[h]
        o = ws[h][CHUNK:] + o_intra[h]
        o = o * lax.rsqrt(jnp.mean(o * o, axis=-1, keepdims=True) + NORM_EPS) * nw
        o_ref[0, :, h * GDN_DV:(h + 1) * GDN_DV] = o.astype(o_ref.dtype)

    xpad_ref[0:CONV_HALO, :] = xpad_ref[CHUNK:CHUNK + CONV_HALO, :]


def _gdn(proj, conv_w, cs, bt, g_rows, norm_w, seq, pad):
    b, lp, _ = proj.shape
    nc = lp // CHUNK
    conv_dim = 3 * GDN_DQK
    qkv = lambda blk: pl.BlockSpec((1, CHUNK, GDN_DQK), lambda i, c: (i, c, blk))
    return pl.pallas_call(
        functools.partial(_gdn_body, pad=pad),
        grid=(b, nc),
        in_specs=[
            qkv(0), qkv(1), qkv(2),
            pl.BlockSpec((CONV_K, conv_dim), lambda i, c: (0, 0)),
            pl.BlockSpec((1, CHUNK, LANES), lambda i, c: (i, c, 0)),
            pl.BlockSpec((1, CHUNK, LANES), lambda i, c: (i, c, 0)),
            pl.BlockSpec((1, 1, GDN_HEADS, CHUNK), lambda i, c: (i, c, 0, 0)),
            pl.BlockSpec((1, GDN_DV), lambda i, c: (0, 0)),
        ],
        out_specs=pl.BlockSpec((1, CHUNK, GDN_DQK), lambda i, c: (i, jnp.maximum(c - 1, 0), 0)),
        out_shape=jax.ShapeDtypeStruct((b, seq, GDN_HEADS * GDN_DV), BF16),
        scratch_shapes=[
            pltpu.VMEM((CONV_HALO + CHUNK, conv_dim), F32),
            pltpu.VMEM((GDN_HEADS, GDN_DK, GDN_DV), F32),
        ],
        compiler_params=_params("arbitrary", "arbitrary"),
        name="gdn_scan",
    )(proj, proj, proj, conv_w, cs, bt, g_rows, norm_w.reshape(1, GDN_DV))


def _merge_body(y_ref, z_ref, o_ref, gate_ref, g1_ref, g2_ref, nw_ref, w1_ref, w2_ref, m_ref,
                a1_ref, a2_ref):
    @pl.when(pl.program_id(1) == 0)
    def _():
        gw = SSD_D_INNER // SSD_GROUPS
        for g in range(SSD_GROUPS):
            sl = slice(g * gw, (g + 1) * gw)
            t = y_ref[:, sl].astype(F32) * _silu(z_ref[:, sl].astype(F32))
            ms = jnp.mean(t * t, axis=-1, keepdims=True)
            a1_ref[:, sl] = (t * lax.rsqrt(ms + NORM_EPS) * nw_ref[:, sl]).astype(BF16)
        a2_ref[...] = (o_ref[...].astype(F32) * _silu(gate_ref[...].astype(F32))).astype(BF16)

    m = (_sigmoid(g1_ref[...].astype(F32)) * _dot(a1_ref[...], w1_ref[...])
         + _sigmoid(g2_ref[...].astype(F32)) * _dot(a2_ref[...], w2_ref[...]))
    m_ref[...] = m.astype(m_ref.dtype)


def _merge(y, o, proj_tok, norm_w, w1, w2, tm_pref=512, tn_pref=1024):
    m, d = y.shape
    n = w1.shape[1]
    tm = _divisor_tile(m, tm_pref, 16)
    tn = _divisor_tile(n, tn_pref, LANES)
    nj = n // tn
    row_d = lambda blk: pl.BlockSpec((tm, d), lambda i, j: (i, blk))
    return pl.pallas_call(
        _merge_body,
        grid=(m // tm, nj),
        in_specs=[
            pl.BlockSpec((tm, d), lambda i, j: (i, 0)),
            row_d(0),
            pl.BlockSpec((tm, d), lambda i, j: (i, 0)),
            row_d(1),
            pl.BlockSpec((tm, tn), lambda i, j: (i, 2 * nj + j)),
            pl.BlockSpec((tm, tn), lambda i, j: (i, 3 * nj + j)),
            pl.BlockSpec((1, d), lambda i, j: (0, 0)),
            pl.BlockSpec((d, tn), lambda i, j: (0, j)),
            pl.BlockSpec((d, tn), lambda i, j: (0, j)),
        ],
        out_specs=pl.BlockSpec((tm, tn), lambda i, j: (i, j)),
        out_shape=jax.ShapeDtypeStruct((m, n), BF16),
        scratch_shapes=[pltpu.VMEM((tm, d), BF16), pltpu.VMEM((tm, d), BF16)],
        compiler_params=_params("arbitrary", "arbitrary"),
        name="merge",
    )(y, proj_tok, o, proj_tok, proj_tok, proj_tok, norm_w.reshape(1, d), w1, w2)


def _matmul_residual_body(a_ref, w_ref, r_ref, o_ref):
    o_ref[...] = r_ref[...] + _dot(a_ref[...], w_ref[...])


def _matmul_residual(a, w, res, tm_pref=512, tn_pref=1024):
    m, k = a.shape
    n = w.shape[1]
    tm = _divisor_tile(m, tm_pref, 8)
    tn = _divisor_tile(n, tn_pref, LANES)
    return pl.pallas_call(
        _matmul_residual_body,
        grid=(m // tm, n // tn),
        in_specs=[
            pl.BlockSpec((tm, k), lambda i, j: (i, 0)),
            pl.BlockSpec((k, tn), lambda i, j: (0, j)),
            pl.BlockSpec((tm, tn), lambda i, j: (i, j)),
        ],
        out_specs=pl.BlockSpec((tm, tn), lambda i, j: (i, j)),
        out_shape=jax.ShapeDtypeStruct((m, n), F32),
        compiler_params=_params("arbitrary", "arbitrary"),
        name="out_proj_residual",
    )(a, w, res)


def _top_values(x, count):
    out = []
    for _ in range(count):
        m = jnp.max(x, axis=0, keepdims=True)
        out.append(m)
        x = jnp.where(x == m, -jnp.inf, x)
    return out


def _peer_route_body(q_ref, keys_ref, s2_ref, e2_ref, thr_ref, f1_ref):
    k1 = keys_ref[0].astype(BF16)
    k2 = keys_ref[1].astype(BF16)
    kk = PEER_TOPK
    for h in range(PEER_HEADS):
        lo = 2 * h * PEER_D_KEY
        q1 = q_ref[:, lo:lo + PEER_D_KEY]
        q2 = q_ref[:, lo + PEER_D_KEY:lo + 2 * PEER_D_KEY]
        s1 = _dot_nt(k1, q1)
        s2 = _dot_nt(k2, q2)
        t1 = _top_values(s1, kk + 1)
        t2 = _top_values(s2, kk + 1)
        t1_blk = jnp.concatenate(t1[:kk], axis=0)
        t2_blk = jnp.concatenate(t2[:kk], axis=0)
        half = kk // 2
        cand = jnp.concatenate(
            [t1[0] + t2_blk]
            + [t1[a] + t2_blk[:half] for a in range(1, half)]
            + [t1_blk[half:] + t2[0]], axis=0)
        best = _top_values(cand, kk + 1)
        runner_up = jnp.maximum(best[kk], jnp.maximum(t1[kk] + t2[0], t1[0] + t2[kk]))
        tau = 0.5 * (best[kk - 1] + runner_up)
        m1, m2 = t1[0], t2[0]
        z = jnp.sum(jnp.where(cand >= tau, jnp.exp(cand - (m1 + m2)), 0.0), axis=0, keepdims=True)
        e2 = jnp.exp(s2 - m2)
        for s in range(s2_ref.shape[1]):
            ln = slice(s * LANES, (s + 1) * LANES)
            s2_ref[h, s] = s2[:, ln]
            e2_ref[h, s] = e2[:, ln]
        thr_ref[h] = tau - s1
        f1_ref[h] = jnp.exp(s1 - m1) / z


def _peer_route(q, keys, t_tok):
    ntok = q.shape[0]
    spec = pl.BlockSpec((PEER_HEADS, PEER_N_KEYS, t_tok), lambda t: (0, 0, t))
    shp = jax.ShapeDtypeStruct((PEER_HEADS, PEER_N_KEYS, ntok), F32)
    strip_spec = pl.BlockSpec((PEER_HEADS, t_tok // LANES, PEER_N_KEYS, LANES), lambda t: (0, t, 0, 0))
    strip_shp = jax.ShapeDtypeStruct((PEER_HEADS, ntok // LANES, PEER_N_KEYS, LANES), F32)
    return pl.pallas_call(
        _peer_route_body,
        grid=(ntok // t_tok,),
        in_specs=[
            pl.BlockSpec((t_tok, q.shape[1]), lambda t: (t, 0)),
            pl.BlockSpec(keys.shape, lambda t: (0, 0, 0)),
        ],
        out_specs=(strip_spec, strip_spec, spec, spec),
        out_shape=(strip_shp, strip_shp, shp, shp),
        compiler_params=_params("arbitrary"),
        name="peer_route",
    )(q, keys)


def _peer_expert_body(hnt_ref, s2_ref, e2_ref, thr_ref, f1_ref, down_ref, upt_ref, acc_ref,
                      w_ref, act_ref, p_ref):
    d, t_tok = hnt_ref.shape
    te = down_ref.shape[0]
    n_part = t_tok // LANES
    rows_a = te // n_part
    rows_o = d // n_part
    e = pl.program_id(1)
    last = pl.num_programs(1) - 1
    slot = lax.rem(e, 2)

    def run(build, project):
        for s in range(n_part):
            if project:
                ro = slice(s * rows_o, (s + 1) * rows_o)
                acc_ref[ro, :] += _dot(upt_ref[ro, :], p_ref[1 - slot])
            if build:
                ln = slice(s * LANES, (s + 1) * LANES)
                for i in range(te // PEER_N_KEYS):
                    w = None
                    for hd in range(PEER_HEADS):
                        thr = thr_ref[hd, i:i + 1, ln]
                        f1 = f1_ref[hd, i:i + 1, ln]
                        term = f1 * jnp.where(s2_ref[hd, s] >= thr, e2_ref[hd, s], 0.0)
                        w = term if w is None else w + term
                    w_ref[i * PEER_N_KEYS:(i + 1) * PEER_N_KEYS, ln] = w
                ra = slice(s * rows_a, (s + 1) * rows_a)
                act_ref[ra, :] = _dot(down_ref[ra, :], hnt_ref[...])
        if build:
            a = act_ref[...]
            gelu = 0.5 * a * (1.0 + lax.erf(a * (2.0 ** -0.5)))
            p_ref[slot] = (w_ref[...] * gelu).astype(BF16)

    @pl.when(e == 0)
    def _():
        acc_ref[...] = jnp.zeros_like(acc_ref)
        run(build=True, project=False)

    @pl.when(jnp.logical_and(e > 0, e < last))
    def _():
        run(build=True, project=True)

    @pl.when(e == last)
    def _():
        run(build=False, project=True)


def _peer_experts(hn_t, s2, e2, thr, f1, down, up_t, t_tok, keys_per_block):
    d, ntok = hn_t.shape
    n_exp = down.shape[0]
    te = keys_per_block * PEER_N_KEYS
    n_blk = n_exp // te
    cur = lambda e: jnp.minimum(e, n_blk - 1)
    prev = lambda e: jnp.maximum(e - 1, 0)
    full = pl.BlockSpec((PEER_HEADS, t_tok // LANES, PEER_N_KEYS, LANES), lambda t, e: (0, t, 0, 0))
    part = pl.BlockSpec((PEER_HEADS, keys_per_block, t_tok), lambda t, e: (0, cur(e), t))
    return pl.pallas_call(
        _peer_expert_body,
        grid=(ntok // t_tok, n_blk + 1),
        in_specs=[
            pl.BlockSpec((d, t_tok), lambda t, e: (0, t)),
            full, full, part, part,
            pl.BlockSpec((te, d), lambda t, e: (cur(e), 0)),
            pl.BlockSpec((d, te), lambda t, e: (0, prev(e))),
        ],
        out_specs=pl.BlockSpec((d, t_tok), lambda t, e: (0, t)),
        out_shape=jax.ShapeDtypeStruct((d, ntok), F32),
        scratch_shapes=[
            pltpu.VMEM((te, t_tok), F32),
            pltpu.VMEM((te, t_tok), F32),
            pltpu.VMEM((2, te, t_tok), BF16),
        ],
        compiler_params=_params("arbitrary", "arbitrary"),
        name="peer_experts",
    )(hn_t, s2, e2, thr, f1, down, up_t)


def _norm_transpose_body(x_ref, g_ref, o_ref):
    x = x_ref[...]
    ms = jnp.mean(x * x, axis=-1, keepdims=True)
    o_ref[...] = (x * lax.rsqrt(ms + NORM_EPS) * g_ref[...]).T.astype(BF16)


def _norm_transpose(x2d, gain, tm_pref=256):
    m, d = x2d.shape
    tm = _divisor_tile(m, tm_pref, LANES)
    return pl.pallas_call(
        _norm_transpose_body,
        grid=(m // tm,),
        in_specs=[pl.BlockSpec((tm, d), lambda i: (i, 0)), pl.BlockSpec((1, d), lambda i: (0, 0))],
        out_specs=pl.BlockSpec((d, tm), lambda i: (0, i)),
        out_shape=jax.ShapeDtypeStruct((d, m), BF16),
        compiler_params=_params("arbitrary"),
        name="norm_transpose",
    )(x2d, gain.reshape(1, d))


def _residual_norm_body(h_ref, acc_ref, g_ref, o_ref):
    y = h_ref[...] + acc_ref[...].T
    ms = jnp.mean(y * y, axis=-1, keepdims=True)
    o_ref[...] = y * lax.rsqrt(ms + NORM_EPS) * g_ref[...]


def _residual_norm(h2, acc_t, gain, tm_pref=256):
    m, d = h2.shape
    tm = _divisor_tile(m, tm_pref, LANES)
    return pl.pallas_call(
        _residual_norm_body,
        grid=(m // tm,),
        in_specs=[
            pl.BlockSpec((tm, d), lambda i: (i, 0)),
            pl.BlockSpec((d, tm), lambda i: (0, i)),
            pl.BlockSpec((1, d), lambda i: (0, 0)),
        ],
        out_specs=pl.BlockSpec((tm, d), lambda i: (i, 0)),
        out_shape=jax.ShapeDtypeStruct((m, d), F32),
        compiler_params=_params("arbitrary"),
        name="residual_norm",
    )(h2, acc_t, gain.reshape(1, d))


def _split_w_in(w_in):
    d = w_in.shape[0]
    widths = (SSD_D_INNER, SSD_D_INNER + 2 * SSD_BC, SSD_HEADS, 3 * GDN_DQK, GDN_HEADS * GDN_DV,
              GDN_HEADS, GDN_HEADS, 2 * d)
    offs = [0]
    for wd in widths:
        offs.append(offs[-1] + wd)
    seg = [w_in[:, offs[i]:offs[i + 1]] for i in range(len(widths))]
    z, xbc, dt, qkv, gate, a, beta, merge = seg
    w_scan = jnp.concatenate([qkv, xbc], axis=1).astype(BF16)
    w_tok = jnp.concatenate([z, gate, merge], axis=1).astype(BF16)
    small = jnp.concatenate([dt, a, beta], axis=1)
    w_small = jnp.pad(small, ((0, 0), (0, LANES - small.shape[1]))).astype(BF16)
    return w_scan, w_tok, w_small


def _lane_vec(*parts):
    v = jnp.concatenate([p.astype(F32) for p in parts])
    return jnp.pad(v, (0, LANES - v.shape[0])).reshape(1, LANES)


def _mixer(x, meta_tokens, norm_w, w_in, ssd_conv_w, ssd_conv_b, ssd_dt_bias, ssd_A_log, ssd_D,
           ssd_norm_w, gdn_conv_w, gdn_dt_bias, gdn_A_log, gdn_norm_w, w_branch_ssd, w_branch_gdn, w_out):
    b, seq, d = x.shape
    pad = CHUNK - N_META
    lp = seq + CHUNK
    nc = lp // CHUNK
    w_scan, w_tok, w_small = _split_w_in(w_in)

    meta = jnp.broadcast_to(meta_tokens[None].astype(x.dtype), (b, N_META, d))
    hp = jnp.concatenate([jnp.zeros((b, pad, d), x.dtype), meta, x], axis=1).reshape(b * lp, d)
    x2d = x.reshape(b * seq, d)

    proj_scan = _norm_matmul(hp, norm_w, w_scan).reshape(b, lp, -1)
    raw_small = _norm_matmul(hp, norm_w, w_small, out_dtype=F32).reshape(b, lp, LANES)
    proj_tok = _norm_matmul(x2d, norm_w, w_tok, tm_pref=512)

    zeros_h = jnp.zeros((GDN_HEADS,), F32)
    bias_vec = _lane_vec(ssd_dt_bias, gdn_dt_bias, zeros_h)
    alog_vec = _lane_vec(ssd_A_log, gdn_A_log, zeros_h)
    sp, cs, bt = _gates(raw_small, bias_vec, alog_vec, pad)
    cs_c = cs.reshape(b, nc, CHUNK, LANES)
    ssd_rows = jnp.swapaxes(cs_c[..., :SSD_HEADS], 2, 3)
    gdn_rows = jnp.swapaxes(cs_c[..., SSD_HEADS:SSD_HEADS + GDN_HEADS], 2, 3)

    d_vec = jnp.repeat(ssd_D.astype(F32), SSD_HEAD_DIM).reshape(1, SSD_D_INNER)
    y = _ssd(proj_scan, ssd_conv_w, ssd_conv_b, sp, cs, ssd_rows, d_vec, seq, pad)
    o = _gdn(proj_scan, gdn_conv_w, cs, bt, gdn_rows, gdn_norm_w, seq, pad)

    m = _merge(y.reshape(b * seq, -1), o.reshape(b * seq, -1), proj_tok, ssd_norm_w,
               w_branch_ssd.astype(BF16), w_branch_gdn.astype(BF16))
    return _matmul_residual(m, w_out.astype(BF16), x2d)


def _peer(h2, norm_w, w_query, sub_keys, expert_down, expert_up, final_w):
    ntok = h2.shape[0]
    q = _norm_matmul(h2, norm_w, w_query.astype(BF16), tm_pref=512)
    s2, e2, thr, f1 = _peer_route(q, sub_keys, _divisor_tile(ntok, 256, LANES))
    hn_t = _norm_transpose(h2, norm_w)
    acc_t = _peer_experts(hn_t, s2, e2, thr, f1, expert_down.astype(BF16), expert_up.astype(BF16).T,
                          _divisor_tile(ntok, 512, LANES), keys_per_block=8)
    return _residual_norm(h2, acc_t, final_w)


def kernel(x, meta_tokens, norm_mix_w, w_in, ssd_conv_w, ssd_conv_b, ssd_dt_bias, ssd_A_log, ssd_D, ssd_norm_w, gdn_conv_w, gdn_dt_bias, gdn_A_log, gdn_norm_w, w_branch_ssd, w_branch_gdn, w_out, norm_ffn_w, peer_w_query, peer_sub_keys, peer_expert_down, peer_expert_up, final_norm_w):
    b, seq, d = x.shape
    assert norm_mix_w.shape[0] == 1, "single-layer trunk"
    h2 = _mixer(x, meta_tokens, norm_mix_w[0], w_in[0], ssd_conv_w[0], ssd_conv_b[0], ssd_dt_bias[0],
                ssd_A_log[0], ssd_D[0], ssd_norm_w[0], gdn_conv_w[0], gdn_dt_bias[0], gdn_A_log[0],
                gdn_norm_w[0], w_branch_ssd[0], w_branch_gdn[0], w_out[0])
    out = _peer(h2, norm_ffn_w[0], peer_w_query[0], peer_sub_keys[0], peer_expert_down[0],
                peer_expert_up[0], final_norm_w)
    return out.reshape(b, seq, d)
```

```python
import functools

import jax
import jax.numpy as jnp
from jax import lax
from jax.experimental import pallas as pl
from jax.experimental.pallas import tpu as pltpu

F32 = jnp.float32
BF16 = jnp.bfloat16

N_META = 16
CHUNK = 64
CONV_K = 4
NORM_EPS = 1e-6
L2_EPS = 1e-6

SSD_HEADS = 32
SSD_HEAD_DIM = 64
SSD_GROUPS = 4
SSD_STATE = 128
SSD_HPG = SSD_HEADS // SSD_GROUPS
SSD_D_INNER = SSD_HEADS * SSD_HEAD_DIM
SSD_BC = SSD_GROUPS * SSD_STATE

GDN_HEADS = 16
GDN_DK = 128
GDN_DV = 128
GDN_DQK = GDN_HEADS * GDN_DK

PEER_HEADS = 8
PEER_N_KEYS = 128
PEER_TOPK = 16
PEER_D_KEY = 128

CONV_HALO = 16
LANES = 128
VMEM_LIMIT = 56 * 1024 * 1024

_NT = (((1,), (1,)), ((), ()))
_TN = (((0,), (0,)), ((), ()))


def _dot(a, b):
    return jnp.dot(a, b, preferred_element_type=F32)


def _dot_nt(a, b):
    return lax.dot_general(a, b, _NT, preferred_element_type=F32)


def _dot_tn(a, b):
    return lax.dot_general(a, b, _TN, preferred_element_type=F32)


def _silu(x):
    return x / (1.0 + jnp.exp(-x))


def _sigmoid(x):
    return 1.0 / (1.0 + jnp.exp(-x))


def _softplus(x):
    return jnp.maximum(x, 0.0) + jnp.log(1.0 + jnp.exp(-jnp.abs(x)))


def _divisor_tile(n, preferred, multiple):
    t = min(preferred, n)
    t -= t % multiple
    while t >= multiple:
        if n % t == 0:
            return t
        t -= multiple
    return n


def _params(*sem):
    return pltpu.CompilerParams(dimension_semantics=sem, vmem_limit_bytes=VMEM_LIMIT)


def _norm_matmul_body(x_ref, g_ref, w_ref, o_ref, *rest):
    n_ref = rest[-1]
    nt_ref = rest[0] if len(rest) == 2 else None

    @pl.when(pl.program_id(1) == 0)
    def _():
        x = x_ref[...]
        ms = jnp.mean(x * x, axis=-1, keepdims=True)
        n = x * lax.rsqrt(ms + NORM_EPS) * g_ref[...]
        n_ref[...] = n.astype(BF16)
        if nt_ref is not None:
            nt_ref[...] = n.T.astype(BF16)

    o_ref[...] = _dot(n_ref[...], w_ref[...]).astype(o_ref.dtype)


def _norm_matmul(x2d, gain, w, tm_pref=768, tn_pref=1024, out_dtype=BF16, with_transposed=False):
    m, k = x2d.shape
    n = w.shape[1]
    tm = _divisor_tile(m, tm_pref, LANES if with_transposed else 16)
    tn = _divisor_tile(n, tn_pref, LANES)
    out_specs = pl.BlockSpec((tm, tn), lambda i, j: (i, j))
    out_shape = jax.ShapeDtypeStruct((m, n), out_dtype)
    if with_transposed:
        out_specs = (out_specs, pl.BlockSpec((k, tm), lambda i, j: (0, i)))
        out_shape = (out_shape, jax.ShapeDtypeStruct((k, m), BF16))
    return pl.pallas_call(
        _norm_matmul_body,
        grid=(m // tm, n // tn),
        in_specs=[
            pl.BlockSpec((tm, k), lambda i, j: (i, 0)),
            pl.BlockSpec((1, k), lambda i, j: (0, 0)),
            pl.BlockSpec((k, tn), lambda i, j: (0, j)),
        ],
        out_specs=out_specs,
        out_shape=out_shape,
        scratch_shapes=[pltpu.VMEM((tm, k), BF16)],
        compiler_params=_params("arbitrary", "arbitrary"),
        name="norm_matmul",
    )(x2d, gain.reshape(1, k), w)


def _gates_body(raw_ref, bias_ref, alog_ref, sp_ref, cs_ref, bt_ref, *, n_chunks, pad):
    ri = lax.broadcasted_iota(jnp.int32, (CHUNK, CHUNK), 0)
    ci = lax.broadcasted_iota(jnp.int32, (CHUNK, CHUNK), 1)
    tri = (ri >= ci).astype(F32)
    neg_a = -jnp.exp(alog_ref[...])
    bias = bias_ref[...]
    row = lax.broadcasted_iota(jnp.int32, (CHUNK, 1), 0)

    def body(c, carry):
        sl = pl.ds(pl.multiple_of(c * CHUNK, CHUNK), CHUNK)
        r = raw_ref[0, sl, :]
        sp = _softplus(r + bias)
        cs = jnp.dot(tri, sp * neg_a, precision=lax.Precision.HIGHEST, preferred_element_type=F32)
        valid = (row + c * CHUNK) >= pad
        sp_ref[0, sl, :] = sp
        cs_ref[0, sl, :] = cs
        bt_ref[0, sl, :] = jnp.where(valid, _sigmoid(r), 0.0)
        return carry

    lax.fori_loop(0, n_chunks, body, 0)


def _gates(raw, bias_vec, alog_vec, pad):
    b, lp, w = raw.shape
    spec = pl.BlockSpec((1, lp, w), lambda i: (i, 0, 0))
    vec = pl.BlockSpec((1, w), lambda i: (0, 0))
    shp = jax.ShapeDtypeStruct((b, lp, w), F32)
    return pl.pallas_call(
        functools.partial(_gates_body, n_chunks=lp // CHUNK, pad=pad),
        grid=(b,),
        in_specs=[spec, vec, vec],
        out_specs=(spec, spec, spec),
        out_shape=(shp, shp, shp),
        compiler_params=_params("arbitrary"),
        name="gates",
    )(raw, bias_vec, alog_vec)


def _shift_matrix():
    r = jnp.arange((CONV_K - 1) * CHUNK)
    src = CONV_HALO + r % CHUNK - (CONV_K - 1 - r // CHUNK)
    return (jnp.arange(CONV_HALO + CHUNK)[None, :] == src[:, None]).astype(BF16)


def _shifted_taps(shift_ref, xpad_ref, taps_ref):
    taps_ref[...] = _dot(shift_ref[...], xpad_ref[...])


def _causal_conv(taps_ref, xpad_ref, w_ref, lo, width, bias):
    cols = slice(lo, lo + width)
    acc = w_ref[CONV_K - 1:CONV_K, cols] * xpad_ref[CONV_HALO:CONV_HALO + CHUNK, cols].astype(F32)
    if bias is not None:
        acc = acc + bias
    for j in range(CONV_K - 1):
        acc = acc + w_ref[j:j + 1, cols] * taps_ref[j * CHUNK:(j + 1) * CHUNK, cols]
    return acc


def _ssd_body(x_ref, bc_ref, cw_ref, cb_ref, sp_ref, cs_ref, csr_ref, dvec_ref, expand_ref, shift_ref,
              y_ref, xpad_ref, taps_ref, st_ref, *, pad):
    c = pl.program_id(1)

    @pl.when(c == 0)
    def _():
        st_ref[...] = jnp.zeros_like(st_ref)
        xpad_ref[0:CONV_HALO, :] = jnp.zeros((CONV_HALO, xpad_ref.shape[1]), BF16)

    xpad_ref[CONV_HALO:CONV_HALO + CHUNK, 0:SSD_D_INNER] = x_ref[0]
    xpad_ref[CONV_HALO:CONV_HALO + CHUNK, SSD_D_INNER:] = bc_ref[0]
    _shifted_taps(shift_ref, xpad_ref, taps_ref)

    row = lax.broadcasted_iota(jnp.int32, (CHUNK, 1), 0)
    valid = (row + c * CHUNK) >= pad
    ri = lax.broadcasted_iota(jnp.int32, (CHUNK, CHUNK), 0)
    ci = lax.broadcasted_iota(jnp.int32, (CHUNK, CHUNK), 1)
    tril = ri >= ci

    sp = sp_ref[0]
    cs = cs_ref[0]
    csr = csr_ref[0, 0]

    def conv(lo, width):
        a = _causal_conv(taps_ref, xpad_ref, cw_ref, lo, width, cb_ref[:, lo:lo + width])
        return jnp.where(valid, _silu(a), 0.0)

    gw = SSD_HPG * SSD_HEAD_DIM
    groups = range(SSD_GROUPS)
    heads = range(SSD_HEADS)
    xg = [conv(g * gw, gw) for g in groups]
    bg = [conv(SSD_D_INNER + g * SSD_STATE, SSD_STATE).astype(BF16) for g in groups]
    cg = [conv(SSD_D_INNER + SSD_BC + g * SSD_STATE, SSD_STATE).astype(BF16) for g in groups]
    cb = [_dot_nt(cg[g], bg[g]) for g in groups]
    st = [st_ref[g] for g in groups]
    yoff = [_dot(cg[g], st[g].astype(BF16)) for g in groups]

    last_row = cs[CHUNK - 1:CHUNK, :]
    factors = (sp, sp * jnp.exp(last_row - cs), jnp.exp(cs))
    pieces = []
    for f in factors:
        hi = f.astype(BF16)
        pieces += [hi, (f - hi.astype(F32)).astype(BF16)]
    spread = _dot(jnp.concatenate(pieces, axis=0), expand_ref[...])
    e_dt, e_w, e_cs = (spread[(2 * k) * CHUNK:(2 * k + 1) * CHUNK] + spread[(2 * k + 1) * CHUNK:(2 * k + 2) * CHUNK]
                       for k in range(len(factors)))

    xdt = [(xg[g] * e_dt[:, g * gw:(g + 1) * gw]).astype(BF16) for g in groups]
    xw = [(xg[g] * e_w[:, g * gw:(g + 1) * gw]).astype(BF16) for g in groups]
    w_intra = []
    for h in heads:
        decay = jnp.where(tril, jnp.exp(cs[:, h:h + 1] - csr[h:h + 1, :]), 0.0)
        w_intra.append((cb[h // SSD_HPG] * decay).astype(BF16))
    ydiag = [_dot(w_intra[h], xdt[h // SSD_HPG][:, (h % SSD_HPG) * SSD_HEAD_DIM:(h % SSD_HPG + 1) * SSD_HEAD_DIM])
             for h in heads]
    snew = [_dot_tn(bg[g], xw[g]) for g in groups]
    for g in groups:
        sl = slice(g * gw, (g + 1) * gw)
        st_ref[g] = st[g] * e_cs[CHUNK - 1:CHUNK, sl] + snew[g]
        y_g = (jnp.concatenate(ydiag[g * SSD_HPG:(g + 1) * SSD_HPG], axis=1) + yoff[g] * e_cs[:, sl]
               + dvec_ref[:, sl] * xg[g])
        y_ref[0, :, sl] = y_g.astype(y_ref.dtype)

    xpad_ref[0:CONV_HALO, :] = xpad_ref[CHUNK:CHUNK + CONV_HALO, :]


def _ssd(proj, conv_w, conv_b, sp, cs, cs_rows, d_vec, seq, pad):
    b, lp, _ = proj.shape
    nc = lp // CHUNK
    conv_dim = SSD_D_INNER + 2 * SSD_BC
    x_blk = 3 * GDN_DQK // SSD_D_INNER
    bc_blk = (3 * GDN_DQK + SSD_D_INNER) // (2 * SSD_BC)
    expand = (jnp.arange(LANES)[:, None] == jnp.arange(SSD_D_INNER)[None, :] // SSD_HEAD_DIM).astype(BF16)
    return pl.pallas_call(
        functools.partial(_ssd_body, pad=pad),
        grid=(b, nc),
        in_specs=[
            pl.BlockSpec((1, CHUNK, SSD_D_INNER), lambda i, c: (i, c, x_blk)),
            pl.BlockSpec((1, CHUNK, 2 * SSD_BC), lambda i, c: (i, c, bc_blk)),
            pl.BlockSpec((CONV_K, conv_dim), lambda i, c: (0, 0)),
            pl.BlockSpec((1, conv_dim), lambda i, c: (0, 0)),
            pl.BlockSpec((1, CHUNK, LANES), lambda i, c: (i, c, 0)),
            pl.BlockSpec((1, CHUNK, LANES), lambda i, c: (i, c, 0)),
            pl.BlockSpec((1, 1, SSD_HEADS, CHUNK), lambda i, c: (i, c, 0, 0)),
            pl.BlockSpec((1, SSD_D_INNER), lambda i, c: (0, 0)),
            pl.BlockSpec((LANES, SSD_D_INNER), lambda i, c: (0, 0)),
            pl.BlockSpec(((CONV_K - 1) * CHUNK, CONV_HALO + CHUNK), lambda i, c: (0, 0)),
        ],
        out_specs=pl.BlockSpec((1, CHUNK, SSD_D_INNER), lambda i, c: (i, jnp.maximum(c - 1, 0), 0)),
        out_shape=jax.ShapeDtypeStruct((b, seq, SSD_D_INNER), BF16),
        scratch_shapes=[
            pltpu.VMEM((CONV_HALO + CHUNK, conv_dim), BF16),
            pltpu.VMEM(((CONV_K - 1) * CHUNK, conv_dim), F32),
            pltpu.VMEM((SSD_GROUPS, SSD_STATE, SSD_HPG * SSD_HEAD_DIM), F32),
        ],
        compiler_params=_params("arbitrary", "arbitrary"),
        name="ssd_scan",
    )(proj, proj, conv_w, conv_b.reshape(1, conv_dim), sp, cs, cs_rows, d_vec, expand, _shift_matrix())


def _unit_lower_inverses(strict_ls, eye):
    ps = [l.astype(BF16) for l in strict_ls]
    ts = [eye - l for l in strict_ls]
    n = 2
    while n < CHUNK:
        p2 = [_dot(p, p).astype(BF16) for p in ps]
        ts = [t + _dot(t.astype(BF16), q) for t, q in zip(ts, p2)]
        ps = p2
        n *= 2
    return ts


def _gdn_body(q_ref, k_ref, v_ref, cw_ref, cs_ref, bt_ref, gr_ref, nw_ref, shift_ref, o_ref,
              xpad_ref, taps_ref, st_ref, *, pad):
    c = pl.program_id(1)

    @pl.when(c == 0)
    def _():
        st_ref[...] = jnp.zeros_like(st_ref)
        xpad_ref[0:CONV_HALO, :] = jnp.zeros((CONV_HALO, xpad_ref.shape[1]), BF16)

    xpad_ref[CONV_HALO:CONV_HALO + CHUNK, 0:GDN_DQK] = q_ref[0]
    xpad_ref[CONV_HALO:CONV_HALO + CHUNK, GDN_DQK:2 * GDN_DQK] = k_ref[0]
    xpad_ref[CONV_HALO:CONV_HALO + CHUNK, 2 * GDN_DQK:] = v_ref[0]
    _shifted_taps(shift_ref, xpad_ref, taps_ref)

    row = lax.broadcasted_iota(jnp.int32, (CHUNK, 1), 0)
    valid = (row + c * CHUNK) >= pad
    ri = lax.broadcasted_iota(jnp.int32, (CHUNK, CHUNK), 0)
    ci = lax.broadcasted_iota(jnp.int32, (CHUNK, CHUNK), 1)
    tril = ri >= ci
    strict = ri > ci
    eye = (ri == ci).astype(F32)

    cs = cs_ref[0]
    bt = bt_ref[0]
    gr = gr_ref[0, 0]
    nw = nw_ref[...]
    g_off = SSD_HEADS
    b_off = SSD_HEADS + GDN_HEADS

    def conv(lo):
        a = _causal_conv(taps_ref, xpad_ref, cw_ref, lo, GDN_DK, None)
        return jnp.where(valid, _silu(a), 0.0)

    heads = range(GDN_HEADS)
    q16, k16, kb16, rhs16, qg, kd16, decay, elast = [], [], [], [], [], [], [], []
    for h in heads:
        q = conv(h * GDN_DK)
        k = conv(GDN_DQK + h * GDN_DK)
        v = conv(2 * GDN_DQK + h * GDN_DV)
        q = q * lax.rsqrt(jnp.sum(q * q, axis=-1, keepdims=True) + L2_EPS) * (GDN_DK ** -0.5)
        k = k * lax.rsqrt(jnp.sum(k * k, axis=-1, keepdims=True) + L2_EPS)
        col = cs[:, g_off + h:g_off + h + 1]
        rowv = gr[h:h + 1, :]
        last = rowv[:, CHUNK - 1:CHUNK]
        beta = bt[:, b_off + h:b_off + h + 1]
        ecol = jnp.exp(col)
        kb = k * beta
        decay.append(jnp.where(tril, jnp.exp(col - rowv), 0.0))
        q16.append(q.astype(BF16))
        k16.append(k.astype(BF16))
        kb16.append(kb.astype(BF16))
        rhs16.append(jnp.concatenate([v * beta, kb * ecol], axis=1).astype(BF16))
        qg.append(q * ecol)
        kd16.append((k * jnp.exp(last - col)).astype(BF16))
        elast.append(jnp.exp(last))
    kk = [_dot_nt(kb16[h], k16[h]) for h in heads]
    qk = [_dot_nt(q16[h], k16[h]) for h in heads]
    tinv = _unit_lower_inverses([jnp.where(strict, kk[h] * decay[h], 0.0) for h in heads], eye)
    uw = [_dot(tinv[h].astype(BF16), rhs16[h]) for h in heads]
    s = [st_ref[h] for h in heads]
    ws = [_dot(jnp.concatenate([uw[h][:, GDN_DV:], qg[h]], axis=0).astype(BF16), s[h].astype(BF16))
          for h in heads]
    v16 = [(uw[h][:, :GDN_DV] - ws[h][:CHUNK]).astype(BF16) for h in heads]
    o_intra = [_dot((qk[h] * decay[h]).astype(BF16), v16[h]) for h in heads]
    s_add = [_dot_tn(kd16[h], v16[h]) for h in heads]
    for h in heads:
        st_ref[h] = s[h] * elast[h] + s_add[h]
        o = ws[h][CHUNK:] + o_intra[h]
        o = o * lax.rsqrt(jnp.mean(o * o, axis=-1, keepdims=True) + NORM_EPS) * nw
        o_ref[0, :, h * GDN_DV:(h + 1) * GDN_DV] = o.astype(o_ref.dtype)

    xpad_ref[0:CONV_HALO, :] = xpad_ref[CHUNK:CHUNK + CONV_HALO, :]


def _gdn(proj, conv_w, cs, bt, g_rows, norm_w, seq, pad):
    b, lp, _ = proj.shape
    nc = lp // CHUNK
    conv_dim = 3 * GDN_DQK
    qkv = lambda blk: pl.BlockSpec((1, CHUNK, GDN_DQK), lambda i, c: (i, c, blk))
    return pl.pallas_call(
        functools.partial(_gdn_body, pad=pad),
        grid=(b, nc),
        in_specs=[
            qkv(0), qkv(1), qkv(2),
            pl.BlockSpec((CONV_K, conv_dim), lambda i, c: (0, 0)),
            pl.BlockSpec((1, CHUNK, LANES), lambda i, c: (i, c, 0)),
            pl.BlockSpec((1, CHUNK, LANES), lambda i, c: (i, c, 0)),
            pl.BlockSpec((1, 1, GDN_HEADS, CHUNK), lambda i, c: (i, c, 0, 0)),
            pl.BlockSpec((1, GDN_DV), lambda i, c: (0, 0)),
            pl.BlockSpec(((CONV_K - 1) * CHUNK, CONV_HALO + CHUNK), lambda i, c: (0, 0)),
        ],
        out_specs=pl.BlockSpec((1, CHUNK, GDN_DQK), lambda i, c: (i, jnp.maximum(c - 1, 0), 0)),
        out_shape=jax.ShapeDtypeStruct((b, seq, GDN_HEADS * GDN_DV), BF16),
        scratch_shapes=[
            pltpu.VMEM((CONV_HALO + CHUNK, conv_dim), BF16),
            pltpu.VMEM(((CONV_K - 1) * CHUNK, conv_dim), F32),
            pltpu.VMEM((GDN_HEADS, GDN_DK, GDN_DV), F32),
        ],
        compiler_params=_params("arbitrary", "arbitrary"),
        name="gdn_scan",
    )(proj, proj, proj, conv_w, cs, bt, g_rows, norm_w.reshape(1, GDN_DV), _shift_matrix())


def _merge_body(y_ref, z_ref, o_ref, gate_ref, g1_ref, g2_ref, nw_ref, w1_ref, w2_ref, m_ref,
                a1_ref, a2_ref):
    @pl.when(pl.program_id(1) == 0)
    def _():
        gw = SSD_D_INNER // SSD_GROUPS
        for g in range(SSD_GROUPS):
            sl = slice(g * gw, (g + 1) * gw)
            t = y_ref[:, sl].astype(F32) * _silu(z_ref[:, sl].astype(F32))
            ms = jnp.mean(t * t, axis=-1, keepdims=True)
            a1_ref[:, sl] = (t * lax.rsqrt(ms + NORM_EPS) * nw_ref[:, sl]).astype(BF16)
        a2_ref[...] = (o_ref[...].astype(F32) * _silu(gate_ref[...].astype(F32))).astype(BF16)

    m = (_sigmoid(g1_ref[...].astype(F32)) * _dot(a1_ref[...], w1_ref[...])
         + _sigmoid(g2_ref[...].astype(F32)) * _dot(a2_ref[...], w2_ref[...]))
    m_ref[...] = m.astype(m_ref.dtype)


def _merge(y, o, proj_tok, norm_w, w1, w2, tm_pref=512, tn_pref=1024):
    m, d = y.shape
    n = w1.shape[1]
    tm = _divisor_tile(m, tm_pref, 16)
    tn = _divisor_tile(n, tn_pref, LANES)
    nj = n // tn
    row_d = lambda blk: pl.BlockSpec((tm, d), lambda i, j: (i, blk))
    return pl.pallas_call(
        _merge_body,
        grid=(m // tm, nj),
        in_specs=[
            pl.BlockSpec((tm, d), lambda i, j: (i, 0)),
            row_d(0),
            pl.BlockSpec((tm, d), lambda i, j: (i, 0)),
            row_d(1),
            pl.BlockSpec((tm, tn), lambda i, j: (i, 2 * nj + j)),
            pl.BlockSpec((tm, tn), lambda i, j: (i, 3 * nj + j)),
            pl.BlockSpec((1, d), lambda i, j: (0, 0)),
            pl.BlockSpec((d, tn), lambda i, j: (0, j)),
            pl.BlockSpec((d, tn), lambda i, j: (0, j)),
        ],
        out_specs=pl.BlockSpec((tm, tn), lambda i, j: (i, j)),
        out_shape=jax.ShapeDtypeStruct((m, n), BF16),
        scratch_shapes=[pltpu.VMEM((tm, d), BF16), pltpu.VMEM((tm, d), BF16)],
        compiler_params=_params("arbitrary", "arbitrary"),
        name="merge",
    )(y, proj_tok, o, proj_tok, proj_tok, proj_tok, norm_w.reshape(1, d), w1, w2)


def _matmul_residual_body(a_ref, w_ref, r_ref, o_ref):
    o_ref[...] = r_ref[...] + _dot(a_ref[...], w_ref[...])


def _matmul_residual(a, w, res, tm_pref=512, tn_pref=1024):
    m, k = a.shape
    n = w.shape[1]
    tm = _divisor_tile(m, tm_pref, 8)
    tn = _divisor_tile(n, tn_pref, LANES)
    return pl.pallas_call(
        _matmul_residual_body,
        grid=(m // tm, n // tn),
        in_specs=[
            pl.BlockSpec((tm, k), lambda i, j: (i, 0)),
            pl.BlockSpec((k, tn), lambda i, j: (0, j)),
            pl.BlockSpec((tm, tn), lambda i, j: (i, j)),
        ],
        out_specs=pl.BlockSpec((tm, tn), lambda i, j: (i, j)),
        out_shape=jax.ShapeDtypeStruct((m, n), F32),
        compiler_params=_params("arbitrary", "arbitrary"),
        name="out_proj_residual",
    )(a, w, res)


def _top_values(x, count):
    out = []
    for _ in range(count):
        m = jnp.max(x, axis=0, keepdims=True)
        out.append(m)
        x = jnp.where(x == m, -jnp.inf, x)
    return out


def _peer_route_body(q_ref, keys_ref, s2_ref, e2_ref, thr_ref, f1_ref):
    k1 = keys_ref[0].astype(BF16)
    k2 = keys_ref[1].astype(BF16)
    kk = PEER_TOPK
    for h in range(PEER_HEADS):
        lo = 2 * h * PEER_D_KEY
        q1 = q_ref[:, lo:lo + PEER_D_KEY]
        q2 = q_ref[:, lo + PEER_D_KEY:lo + 2 * PEER_D_KEY]
        s1 = _dot_nt(k1, q1)
        s2 = _dot_nt(k2, q2)
        t1 = _top_values(s1, kk + 1)
        t2 = _top_values(s2, kk + 1)
        t1_blk = jnp.concatenate(t1[:kk], axis=0)
        t2_blk = jnp.concatenate(t2[:kk], axis=0)
        half = kk // 2
        cand = jnp.concatenate(
            [t1[0] + t2_blk]
            + [t1[a] + t2_blk[:half] for a in range(1, half)]
            + [t1_blk[half:] + t2[0]], axis=0)
        best = _top_values(cand, kk + 1)
        runner_up = jnp.maximum(best[kk], jnp.maximum(t1[kk] + t2[0], t1[0] + t2[kk]))
        tau = 0.5 * (best[kk - 1] + runner_up)
        m1, m2 = t1[0], t2[0]
        z = jnp.sum(jnp.where(cand >= tau, jnp.exp(cand - (m1 + m2)), 0.0), axis=0, keepdims=True)
        e2 = jnp.exp(s2 - m2)
        for s in range(s2_ref.shape[1]):
            ln = slice(s * LANES, (s + 1) * LANES)
            s2_ref[h, s] = s2[:, ln]
            e2_ref[h, s] = e2[:, ln]
        thr_ref[h] = tau - s1
        f1_ref[h] = jnp.exp(s1 - m1) / z


def _peer_route(q, keys, t_tok):
    ntok = q.shape[0]
    spec = pl.BlockSpec((PEER_HEADS, PEER_N_KEYS, t_tok), lambda t: (0, 0, t))
    shp = jax.ShapeDtypeStruct((PEER_HEADS, PEER_N_KEYS, ntok), F32)
    strip_spec = pl.BlockSpec((PEER_HEADS, t_tok // LANES, PEER_N_KEYS, LANES), lambda t: (0, t, 0, 0))
    strip_shp = jax.ShapeDtypeStruct((PEER_HEADS, ntok // LANES, PEER_N_KEYS, LANES), F32)
    return pl.pallas_call(
        _peer_route_body,
        grid=(ntok // t_tok,),
        in_specs=[
            pl.BlockSpec((t_tok, q.shape[1]), lambda t: (t, 0)),
            pl.BlockSpec(keys.shape, lambda t: (0, 0, 0)),
        ],
        out_specs=(strip_spec, strip_spec, spec, spec),
        out_shape=(strip_shp, strip_shp, shp, shp),
        compiler_params=_params("arbitrary"),
        name="peer_route",
    )(q, keys)


def _peer_expert_body(hnt_ref, s2_ref, e2_ref, thr_ref, f1_ref, down_ref, upt_ref, acc_ref,
                      w_ref, act_ref, p_ref):
    d, t_tok = hnt_ref.shape
    te = down_ref.shape[0]
    n_part = t_tok // LANES
    rows_a = te // n_part
    rows_o = d // n_part
    e = pl.program_id(1)
    last = pl.num_programs(1) - 1
    slot = lax.rem(e, 2)

    def run(build, project):
        for s in range(n_part):
            if build:
                ln = slice(s * LANES, (s + 1) * LANES)
                for i in range(te // PEER_N_KEYS):
                    w = None
                    for hd in range(PEER_HEADS):
                        thr = thr_ref[hd, i:i + 1, ln]
                        f1 = f1_ref[hd, i:i + 1, ln]
                        term = f1 * jnp.where(s2_ref[hd, s] >= thr, e2_ref[hd, s], 0.0)
                        w = term if w is None else w + term
                    w_ref[i * PEER_N_KEYS:(i + 1) * PEER_N_KEYS, ln] = w
                ra = slice(s * rows_a, (s + 1) * rows_a)
                act_ref[ra, :] = _dot(down_ref[ra, :], hnt_ref[...])
            if project:
                ro = slice(s * rows_o, (s + 1) * rows_o)
                acc_ref[ro, :] += _dot(upt_ref[ro, :], p_ref[1 - slot])
        if build:
            a = act_ref[...]
            gelu = 0.5 * a * (1.0 + lax.erf(a * (2.0 ** -0.5)))
            p_ref[slot] = (w_ref[...] * gelu).astype(BF16)

    @pl.when(e == 0)
    def _():
        acc_ref[...] = jnp.zeros_like(acc_ref)
        run(build=True, project=False)

    @pl.when(jnp.logical_and(e > 0, e < last))
    def _():
        run(build=True, project=True)

    @pl.when(e == last)
    def _():
        run(build=False, project=True)


def _peer_experts(hn_t, s2, e2, thr, f1, down, up_t, t_tok, keys_per_block):
    d, ntok = hn_t.shape
    n_exp = down.shape[0]
    te = keys_per_block * PEER_N_KEYS
    n_blk = n_exp // te
    cur = lambda e: jnp.minimum(e, n_blk - 1)
    prev = lambda e: jnp.maximum(e - 1, 0)
    full = pl.BlockSpec((PEER_HEADS, t_tok // LANES, PEER_N_KEYS, LANES), lambda t, e: (0, t, 0, 0))
    part = pl.BlockSpec((PEER_HEADS, keys_per_block, t_tok), lambda t, e: (0, cur(e), t))
    return pl.pallas_call(
        _peer_expert_body,
        grid=(ntok // t_tok, n_blk + 1),
        in_specs=[
            pl.BlockSpec((d, t_tok), lambda t, e: (0, t)),
            full, full, part, part,
            pl.BlockSpec((te, d), lambda t, e: (cur(e), 0)),
            pl.BlockSpec((d, te), lambda t, e: (0, prev(e))),
        ],
        out_specs=pl.BlockSpec((d, t_tok), lambda t, e: (0, t)),
        out_shape=jax.ShapeDtypeStruct((d, ntok), F32),
        scratch_shapes=[
            pltpu.VMEM((te, t_tok), F32),
            pltpu.VMEM((te, t_tok), F32),
            pltpu.VMEM((2, te, t_tok), BF16),
        ],
        compiler_params=_params("arbitrary", "arbitrary"),
        name="peer_experts",
    )(hn_t, s2, e2, thr, f1, down, up_t)


def _residual_norm_body(h_ref, acc_ref, g_ref, o_ref):
    y = h_ref[...] + acc_ref[...].T
    ms = jnp.mean(y * y, axis=-1, keepdims=True)
    o_ref[...] = y * lax.rsqrt(ms + NORM_EPS) * g_ref[...]


def _residual_norm(h2, acc_t, gain, tm_pref=256):
    m, d = h2.shape
    tm = _divisor_tile(m, tm_pref, LANES)
    return pl.pallas_call(
        _residual_norm_body,
        grid=(m // tm,),
        in_specs=[
            pl.BlockSpec((tm, d), lambda i: (i, 0)),
            pl.BlockSpec((d, tm), lambda i: (0, i)),
            pl.BlockSpec((1, d), lambda i: (0, 0)),
        ],
        out_specs=pl.BlockSpec((tm, d), lambda i: (i, 0)),
        out_shape=jax.ShapeDtypeStruct((m, d), F32),
        compiler_params=_params("arbitrary"),
        name="residual_norm",
    )(h2, acc_t, gain.reshape(1, d))


def _cast_body(x_ref, o_ref, *, transpose):
    x = x_ref[...]
    o_ref[...] = (x.T if transpose else x).astype(o_ref.dtype)


def _cast_bf16(w, rows_pref, transpose=False):
    r, c = w.shape
    tr = _divisor_tile(r, rows_pref, LANES)
    out_spec = pl.BlockSpec((c, tr), lambda i: (0, i)) if transpose else pl.BlockSpec((tr, c), lambda i: (i, 0))
    return pl.pallas_call(
        functools.partial(_cast_body, transpose=transpose),
        grid=(r // tr,),
        in_specs=[pl.BlockSpec((tr, c), lambda i: (i, 0))],
        out_specs=out_spec,
        out_shape=jax.ShapeDtypeStruct((c, r) if transpose else (r, c), BF16),
        compiler_params=_params("arbitrary"),
        name="cast_bf16",
    )(w)


def _split_w_in(w_in):
    d = w_in.shape[0]
    widths = (SSD_D_INNER, SSD_D_INNER + 2 * SSD_BC, SSD_HEADS, 3 * GDN_DQK, GDN_HEADS * GDN_DV,
              GDN_HEADS, GDN_HEADS, 2 * d)
    offs = [0]
    for wd in widths:
        offs.append(offs[-1] + wd)
    w16 = _cast_bf16(w_in, rows_pref=128)
    seg = [w16[:, offs[i]:offs[i + 1]] for i in range(len(widths))]
    z, xbc, dt, qkv, gate, a, beta, merge = seg
    w_scan = jnp.concatenate([qkv, xbc], axis=1)
    w_tok = jnp.concatenate([z, gate, merge], axis=1)
    small = jnp.concatenate([dt, a, beta], axis=1)
    w_small = jnp.pad(small, ((0, 0), (0, LANES - small.shape[1])))
    return w_scan, w_tok, w_small


def _lane_vec(*parts):
    v = jnp.concatenate([p.astype(F32) for p in parts])
    return jnp.pad(v, (0, LANES - v.shape[0])).reshape(1, LANES)


def _mixer(x, meta_tokens, norm_w, w_in, ssd_conv_w, ssd_conv_b, ssd_dt_bias, ssd_A_log, ssd_D,
           ssd_norm_w, gdn_conv_w, gdn_dt_bias, gdn_A_log, gdn_norm_w, w_branch_ssd, w_branch_gdn, w_out):
    b, seq, d = x.shape
    pad = CHUNK - N_META
    lp = seq + CHUNK
    nc = lp // CHUNK
    w_scan, w_tok, w_small = _split_w_in(w_in)

    meta = jnp.broadcast_to(meta_tokens[None].astype(x.dtype), (b, N_META, d))
    hp = jnp.concatenate([jnp.zeros((b, pad, d), x.dtype), meta, x], axis=1).reshape(b * lp, d)
    x2d = x.reshape(b * seq, d)

    proj_scan = _norm_matmul(hp, norm_w, w_scan).reshape(b, lp, -1)
    raw_small = _norm_matmul(hp, norm_w, w_small, out_dtype=F32).reshape(b, lp, LANES)
    proj_tok = _norm_matmul(x2d, norm_w, w_tok, tm_pref=512)

    zeros_h = jnp.zeros((GDN_HEADS,), F32)
    bias_vec = _lane_vec(ssd_dt_bias, gdn_dt_bias, zeros_h)
    alog_vec = _lane_vec(ssd_A_log, gdn_A_log, zeros_h)
    sp, cs, bt = _gates(raw_small, bias_vec, alog_vec, pad)
    cs_c = cs.reshape(b, nc, CHUNK, LANES)
    ssd_rows = jnp.swapaxes(cs_c[..., :SSD_HEADS], 2, 3)
    gdn_rows = jnp.swapaxes(cs_c[..., SSD_HEADS:SSD_HEADS + GDN_HEADS], 2, 3)

    d_vec = jnp.repeat(ssd_D.astype(F32), SSD_HEAD_DIM).reshape(1, SSD_D_INNER)
    y = _ssd(proj_scan, ssd_conv_w, ssd_conv_b, sp, cs, ssd_rows, d_vec, seq, pad)
    o = _gdn(proj_scan, gdn_conv_w, cs, bt, gdn_rows, gdn_norm_w, seq, pad)

    m = _merge(y.reshape(b * seq, -1), o.reshape(b * seq, -1), proj_tok, ssd_norm_w,
               w_branch_ssd.astype(BF16), w_branch_gdn.astype(BF16))
    return _matmul_residual(m, w_out.astype(BF16), x2d)


def _peer(h2, norm_w, w_query, sub_keys, expert_down, expert_up, final_w):
    ntok = h2.shape[0]
    q, hn_t = _norm_matmul(h2, norm_w, w_query.astype(BF16), tm_pref=512, with_transposed=True)
    s2, e2, thr, f1 = _peer_route(q, sub_keys, _divisor_tile(ntok, 256, LANES))
    acc_t = _peer_experts(hn_t, s2, e2, thr, f1, _cast_bf16(expert_down, rows_pref=1024),
                          _cast_bf16(expert_up, rows_pref=512, transpose=True),
                          _divisor_tile(ntok, 512, LANES), keys_per_block=8)
    return _residual_norm(h2, acc_t, final_w)


def kernel(x, meta_tokens, norm_mix_w, w_in, ssd_conv_w, ssd_conv_b, ssd_dt_bias, ssd_A_log, ssd_D, ssd_norm_w, gdn_conv_w, gdn_dt_bias, gdn_A_log, gdn_norm_w, w_branch_ssd, w_branch_gdn, w_out, norm_ffn_w, peer_w_query, peer_sub_keys, peer_expert_down, peer_expert_up, final_norm_w):
    b, seq, d = x.shape
    assert norm_mix_w.shape[0] == 1, "single-layer trunk"
    h2 = _mixer(x, meta_tokens, norm_mix_w[0], w_in[0], ssd_conv_w[0], ssd_conv_b[0], ssd_dt_bias[0],
                ssd_A_log[0], ssd_D[0], ssd_norm_w[0], gdn_conv_w[0], gdn_dt_bias[0], gdn_A_log[0],
                gdn_norm_w[0], w_branch_ssd[0], w_branch_gdn[0], w_out[0])
    out = _peer(h2, norm_ffn_w[0], peer_w_query[0], peer_sub_keys[0], peer_expert_down[0],
                peer_expert_up[0], final_norm_w)
    return out.reshape(b, seq, d)
```

```python
import functools

import jax
import jax.numpy as jnp
from jax import lax
from jax.experimental import pallas as pl
from jax.experimental.pallas import tpu as pltpu

F32 = jnp.float32
BF16 = jnp.bfloat16

N_META = 16
CHUNK = 64
CONV_K = 4
NORM_EPS = 1e-6
L2_EPS = 1e-6

SSD_HEADS = 32
SSD_HEAD_DIM = 64
SSD_GROUPS = 4
SSD_STATE = 128
SSD_HPG = SSD_HEADS // SSD_GROUPS
SSD_D_INNER = SSD_HEADS * SSD_HEAD_DIM
SSD_BC = SSD_GROUPS * SSD_STATE

GDN_HEADS = 16
GDN_DK = 128
GDN_DV = 128
GDN_DQK = GDN_HEADS * GDN_DK

PEER_HEADS = 8
PEER_N_KEYS = 128
PEER_TOPK = 16
PEER_D_KEY = 128

CONV_HALO = 16
LANES = 128
VMEM_LIMIT = 56 * 1024 * 1024

_NT = (((1,), (1,)), ((), ()))
_TN = (((0,), (0,)), ((), ()))


def _dot(a, b):
    return jnp.dot(a, b, preferred_element_type=F32)


def _dot_nt(a, b):
    return lax.dot_general(a, b, _NT, preferred_element_type=F32)


def _dot_tn(a, b):
    return lax.dot_general(a, b, _TN, preferred_element_type=F32)


def _silu(x):
    return x / (1.0 + jnp.exp(-x))


def _sigmoid(x):
    return 1.0 / (1.0 + jnp.exp(-x))


def _softplus(x):
    return jnp.maximum(x, 0.0) + jnp.log(1.0 + jnp.exp(-jnp.abs(x)))


def _divisor_tile(n, preferred, multiple):
    t = min(preferred, n)
    t -= t % multiple
    while t >= multiple:
        if n % t == 0:
            return t
        t -= multiple
    return n


def _params(*sem):
    return pltpu.CompilerParams(dimension_semantics=sem, vmem_limit_bytes=VMEM_LIMIT)


def _norm_matmul_body(x_ref, g_ref, w_ref, o_ref, *rest, w_is_transposed):
    n_ref = rest[-1]
    nt_ref = rest[0] if len(rest) == 2 else None

    @pl.when(pl.program_id(1) == 0)
    def _():
        x = x_ref[...]
        ms = jnp.mean(x * x, axis=-1, keepdims=True)
        n = x * lax.rsqrt(ms + NORM_EPS) * g_ref[...]
        n_ref[...] = n.astype(BF16)
        if nt_ref is not None:
            nt_ref[...] = n.T.astype(BF16)

    dot = _dot_nt if w_is_transposed else _dot
    o_ref[...] = dot(n_ref[...], w_ref[...]).astype(o_ref.dtype)


def _norm_matmul(x2d, gain, w, tm_pref=768, tn_pref=1024, out_dtype=BF16, with_transposed=False,
                 w_is_transposed=False):
    m, k = x2d.shape
    n = w.shape[0] if w_is_transposed else w.shape[1]
    tm = _divisor_tile(m, tm_pref, LANES if with_transposed else 16)
    tn = _divisor_tile(n, tn_pref, LANES)
    w_spec = (pl.BlockSpec((tn, k), lambda i, j: (j, 0)) if w_is_transposed
              else pl.BlockSpec((k, tn), lambda i, j: (0, j)))
    out_specs = pl.BlockSpec((tm, tn), lambda i, j: (i, j))
    out_shape = jax.ShapeDtypeStruct((m, n), out_dtype)
    if with_transposed:
        out_specs = (out_specs, pl.BlockSpec((k, tm), lambda i, j: (0, i)))
        out_shape = (out_shape, jax.ShapeDtypeStruct((k, m), BF16))
    return pl.pallas_call(
        functools.partial(_norm_matmul_body, w_is_transposed=w_is_transposed),
        grid=(m // tm, n // tn),
        in_specs=[
            pl.BlockSpec((tm, k), lambda i, j: (i, 0)),
            pl.BlockSpec((1, k), lambda i, j: (0, 0)),
            w_spec,
        ],
        out_specs=out_specs,
        out_shape=out_shape,
        scratch_shapes=[pltpu.VMEM((tm, k), BF16)],
        compiler_params=_params("arbitrary", "arbitrary"),
        name="norm_matmul",
    )(x2d, gain.reshape(1, k), w)


def _gates_body(raw_ref, bias_ref, alog_ref, sp_ref, cs_ref, bt_ref, *, n_chunks, pad):
    ri = lax.broadcasted_iota(jnp.int32, (CHUNK, CHUNK), 0)
    ci = lax.broadcasted_iota(jnp.int32, (CHUNK, CHUNK), 1)
    tri = (ri >= ci).astype(F32)
    neg_a = -jnp.exp(alog_ref[...])
    bias = bias_ref[...]
    row = lax.broadcasted_iota(jnp.int32, (CHUNK, 1), 0)

    def body(c, carry):
        sl = pl.ds(pl.multiple_of(c * CHUNK, CHUNK), CHUNK)
        r = raw_ref[0, sl, :]
        sp = _softplus(r + bias)
        cs = jnp.dot(tri, sp * neg_a, precision=lax.Precision.HIGHEST, preferred_element_type=F32)
        valid = (row + c * CHUNK) >= pad
        sp_ref[0, sl, :] = sp
        cs_ref[0, sl, :] = cs
        bt_ref[0, sl, :] = jnp.where(valid, _sigmoid(r), 0.0)
        return carry

    lax.fori_loop(0, n_chunks, body, 0)


def _gates(raw, bias_vec, alog_vec, pad):
    b, lp, w = raw.shape
    spec = pl.BlockSpec((1, lp, w), lambda i: (i, 0, 0))
    vec = pl.BlockSpec((1, w), lambda i: (0, 0))
    shp = jax.ShapeDtypeStruct((b, lp, w), F32)
    return pl.pallas_call(
        functools.partial(_gates_body, n_chunks=lp // CHUNK, pad=pad),
        grid=(b,),
        in_specs=[spec, vec, vec],
        out_specs=(spec, spec, spec),
        out_shape=(shp, shp, shp),
        compiler_params=_params("arbitrary"),
        name="gates",
    )(raw, bias_vec, alog_vec)


def _shift_matrix():
    r = jnp.arange((CONV_K - 1) * CHUNK)
    src = CONV_HALO + r % CHUNK - (CONV_K - 1 - r // CHUNK)
    return (jnp.arange(CONV_HALO + CHUNK)[None, :] == src[:, None]).astype(BF16)


def _shifted_taps(shift_ref, xpad_ref, taps_ref):
    taps_ref[...] = _dot(shift_ref[...], xpad_ref[...])


def _causal_conv(taps_ref, xpad_ref, w_ref, lo, width, bias):
    cols = slice(lo, lo + width)
    acc = w_ref[CONV_K - 1:CONV_K, cols] * xpad_ref[CONV_HALO:CONV_HALO + CHUNK, cols].astype(F32)
    if bias is not None:
        acc = acc + bias
    for j in range(CONV_K - 1):
        acc = acc + w_ref[j:j + 1, cols] * taps_ref[j * CHUNK:(j + 1) * CHUNK, cols]
    return acc


def _ssd_body(x_ref, bc_ref, cw_ref, cb_ref, sp_ref, cs_ref, csr_ref, dvec_ref, expand_ref, shift_ref,
              y_ref, xpad_ref, taps_ref, st_ref, *, pad):
    c = pl.program_id(1)

    @pl.when(c == 0)
    def _():
        st_ref[...] = jnp.zeros_like(st_ref)
        xpad_ref[0:CONV_HALO, :] = jnp.zeros((CONV_HALO, xpad_ref.shape[1]), BF16)

    xpad_ref[CONV_HALO:CONV_HALO + CHUNK, 0:SSD_D_INNER] = x_ref[0]
    xpad_ref[CONV_HALO:CONV_HALO + CHUNK, SSD_D_INNER:] = bc_ref[0]
    _shifted_taps(shift_ref, xpad_ref, taps_ref)

    row = lax.broadcasted_iota(jnp.int32, (CHUNK, 1), 0)
    valid = (row + c * CHUNK) >= pad
    ri = lax.broadcasted_iota(jnp.int32, (CHUNK, CHUNK), 0)
    ci = lax.broadcasted_iota(jnp.int32, (CHUNK, CHUNK), 1)
    tril = ri >= ci

    sp = sp_ref[0]
    cs = cs_ref[0]
    csr = csr_ref[0, 0]

    def conv(lo, width):
        a = _causal_conv(taps_ref, xpad_ref, cw_ref, lo, width, cb_ref[:, lo:lo + width])
        return jnp.where(valid, _silu(a), 0.0)

    gw = SSD_HPG * SSD_HEAD_DIM
    groups = range(SSD_GROUPS)
    heads = range(SSD_HEADS)
    xg = [conv(g * gw, gw) for g in groups]
    bg = [conv(SSD_D_INNER + g * SSD_STATE, SSD_STATE).astype(BF16) for g in groups]
    cg = [conv(SSD_D_INNER + SSD_BC + g * SSD_STATE, SSD_STATE).astype(BF16) for g in groups]
    cb = [_dot_nt(cg[g], bg[g]) for g in groups]
    st = [st_ref[g] for g in groups]
    yoff = [_dot(cg[g], st[g].astype(BF16)) for g in groups]

    last_row = cs[CHUNK - 1:CHUNK, :]
    factors = (sp, sp * jnp.exp(last_row - cs), jnp.exp(cs))
    pieces = []
    for f in factors:
        hi = f.astype(BF16)
        pieces += [hi, (f - hi.astype(F32)).astype(BF16)]
    spread = _dot(jnp.concatenate(pieces, axis=0), expand_ref[...])
    e_dt, e_w, e_cs = (spread[(2 * k) * CHUNK:(2 * k + 1) * CHUNK] + spread[(2 * k + 1) * CHUNK:(2 * k + 2) * CHUNK]
                       for k in range(len(factors)))

    xdt = [(xg[g] * e_dt[:, g * gw:(g + 1) * gw]).astype(BF16) for g in groups]
    xw = [(xg[g] * e_w[:, g * gw:(g + 1) * gw]).astype(BF16) for g in groups]
    w_intra = []
    for h in heads:
        decay = jnp.where(tril, jnp.exp(cs[:, h:h + 1] - csr[h:h + 1, :]), 0.0)
        w_intra.append((cb[h // SSD_HPG] * decay).astype(BF16))
    ydiag = [_dot(w_intra[h], xdt[h // SSD_HPG][:, (h % SSD_HPG) * SSD_HEAD_DIM:(h % SSD_HPG + 1) * SSD_HEAD_DIM])
             for h in heads]
    snew = [_dot_tn(bg[g], xw[g]) for g in groups]
    for g in groups:
        sl = slice(g * gw, (g + 1) * gw)
        st_ref[g] = st[g] * e_cs[CHUNK - 1:CHUNK, sl] + snew[g]
        y_g = (jnp.concatenate(ydiag[g * SSD_HPG:(g + 1) * SSD_HPG], axis=1) + yoff[g] * e_cs[:, sl]
               + dvec_ref[:, sl] * xg[g])
        y_ref[0, :, sl] = y_g.astype(y_ref.dtype)

    xpad_ref[0:CONV_HALO, :] = xpad_ref[CHUNK:CHUNK + CONV_HALO, :]


def _ssd(proj, conv_w, conv_b, sp, cs, cs_rows, d_vec, seq, pad):
    b, lp, _ = proj.shape
    nc = lp // CHUNK
    conv_dim = SSD_D_INNER + 2 * SSD_BC
    x_blk = 3 * GDN_DQK // SSD_D_INNER
    bc_blk = (3 * GDN_DQK + SSD_D_INNER) // (2 * SSD_BC)
    expand = (jnp.arange(LANES)[:, None] == jnp.arange(SSD_D_INNER)[None, :] // SSD_HEAD_DIM).astype(BF16)
    return pl.pallas_call(
        functools.partial(_ssd_body, pad=pad),
        grid=(b, nc),
        in_specs=[
            pl.BlockSpec((1, CHUNK, SSD_D_INNER), lambda i, c: (i, c, x_blk)),
            pl.BlockSpec((1, CHUNK, 2 * SSD_BC), lambda i, c: (i, c, bc_blk)),
            pl.BlockSpec((CONV_K, conv_dim), lambda i, c: (0, 0)),
            pl.BlockSpec((1, conv_dim), lambda i, c: (0, 0)),
            pl.BlockSpec((1, CHUNK, LANES), lambda i, c: (i, c, 0)),
            pl.BlockSpec((1, CHUNK, LANES), lambda i, c: (i, c, 0)),
            pl.BlockSpec((1, 1, SSD_HEADS, CHUNK), lambda i, c: (i, c, 0, 0)),
            pl.BlockSpec((1, SSD_D_INNER), lambda i, c: (0, 0)),
            pl.BlockSpec((LANES, SSD_D_INNER), lambda i, c: (0, 0)),
            pl.BlockSpec(((CONV_K - 1) * CHUNK, CONV_HALO + CHUNK), lambda i, c: (0, 0)),
        ],
        out_specs=pl.BlockSpec((1, CHUNK, SSD_D_INNER), lambda i, c: (i, jnp.maximum(c - 1, 0), 0)),
        out_shape=jax.ShapeDtypeStruct((b, seq, SSD_D_INNER), BF16),
        scratch_shapes=[
            pltpu.VMEM((CONV_HALO + CHUNK, conv_dim), BF16),
            pltpu.VMEM(((CONV_K - 1) * CHUNK, conv_dim), F32),
            pltpu.VMEM((SSD_GROUPS, SSD_STATE, SSD_HPG * SSD_HEAD_DIM), F32),
        ],
        compiler_params=_params("arbitrary", "arbitrary"),
        name="ssd_scan",
    )(proj, proj, conv_w, conv_b.reshape(1, conv_dim), sp, cs, cs_rows, d_vec, expand, _shift_matrix())


def _unit_lower_inverses(strict_ls, eye):
    ps = [l.astype(BF16) for l in strict_ls]
    ts = [eye - l for l in strict_ls]
    n = 2
    while n < CHUNK:
        p2 = [_dot(p, p).astype(BF16) for p in ps]
        ts = [t + _dot(t.astype(BF16), q) for t, q in zip(ts, p2)]
        ps = p2
        n *= 2
    return ts


def _gdn_body(q_ref, k_ref, v_ref, cw_ref, cs_ref, bt_ref, gr_ref, nw_ref, shift_ref, o_ref,
              xpad_ref, taps_ref, st_ref, *, pad):
    c = pl.program_id(1)

    @pl.when(c == 0)
    def _():
        st_ref[...] = jnp.zeros_like(st_ref)
        xpad_ref[0:CONV_HALO, :] = jnp.zeros((CONV_HALO, xpad_ref.shape[1]), BF16)

    xpad_ref[CONV_HALO:CONV_HALO + CHUNK, 0:GDN_DQK] = q_ref[0]
    xpad_ref[CONV_HALO:CONV_HALO + CHUNK, GDN_DQK:2 * GDN_DQK] = k_ref[0]
    xpad_ref[CONV_HALO:CONV_HALO + CHUNK, 2 * GDN_DQK:] = v_ref[0]
    _shifted_taps(shift_ref, xpad_ref, taps_ref)

    row = lax.broadcasted_iota(jnp.int32, (CHUNK, 1), 0)
    valid = (row + c * CHUNK) >= pad
    ri = lax.broadcasted_iota(jnp.int32, (CHUNK, CHUNK), 0)
    ci = lax.broadcasted_iota(jnp.int32, (CHUNK, CHUNK), 1)
    tril = ri >= ci
    strict = ri > ci
    eye = (ri == ci).astype(F32)

    cs = cs_ref[0]
    bt = bt_ref[0]
    gr = gr_ref[0, 0]
    nw = nw_ref[...]
    g_off = SSD_HEADS
    b_off = SSD_HEADS + GDN_HEADS

    def conv(lo):
        a = _causal_conv(taps_ref, xpad_ref, cw_ref, lo, GDN_DK, None)
        return jnp.where(valid, _silu(a), 0.0)

    heads = range(GDN_HEADS)
    q16, k16, kb16, rhs16, qg, kd16, decay, elast = [], [], [], [], [], [], [], []
    for h in heads:
        q = conv(h * GDN_DK)
        k = conv(GDN_DQK + h * GDN_DK)
        v = conv(2 * GDN_DQK + h * GDN_DV)
        q = q * lax.rsqrt(jnp.sum(q * q, axis=-1, keepdims=True) + L2_EPS) * (GDN_DK ** -0.5)
        k = k * lax.rsqrt(jnp.sum(k * k, axis=-1, keepdims=True) + L2_EPS)
        col = cs[:, g_off + h:g_off + h + 1]
        rowv = gr[h:h + 1, :]
        last = rowv[:, CHUNK - 1:CHUNK]
        beta = bt[:, b_off + h:b_off + h + 1]
        ecol = jnp.exp(col)
        kb = k * beta
        decay.append(jnp.where(tril, jnp.exp(col - rowv), 0.0))
        q16.append(q.astype(BF16))
        k16.append(k.astype(BF16))
        kb16.append(kb.astype(BF16))
        rhs16.append(jnp.concatenate([v * beta, kb * ecol], axis=1).astype(BF16))
        qg.append(q * ecol)
        kd16.append((k * jnp.exp(last - col)).astype(BF16))
        elast.append(jnp.exp(last))
    kk = [_dot_nt(kb16[h], k16[h]) for h in heads]
    qk = [_dot_nt(q16[h], k16[h]) for h in heads]
    tinv = _unit_lower_inverses([jnp.where(strict, kk[h] * decay[h], 0.0) for h in heads], eye)
    uw = [_dot(tinv[h].astype(BF16), rhs16[h]) for h in heads]
    s = [st_ref[h] for h in heads]
    ws = [_dot(jnp.concatenate([uw[h][:, GDN_DV:], qg[h]], axis=0).astype(BF16), s[h].astype(BF16))
          for h in heads]
    v16 = [(uw[h][:, :GDN_DV] - ws[h][:CHUNK]).astype(BF16) for h in heads]
    o_intra = [_dot((qk[h] * decay[h]).astype(BF16), v16[h]) for h in heads]
    s_add = [_dot_tn(kd16[h], v16[h]) for h in heads]
    for h in heads:
        st_ref[h] = s[h] * elast[h] + s_add[h]
        o = ws[h][CHUNK:] + o_intra[h]
        o = o * lax.rsqrt(jnp.mean(o * o, axis=-1, keepdims=True) + NORM_EPS) * nw
        o_ref[0, :, h * GDN_DV:(h + 1) * GDN_DV] = o.astype(o_ref.dtype)

    xpad_ref[0:CONV_HALO, :] = xpad_ref[CHUNK:CHUNK + CONV_HALO, :]


def _gdn(proj, conv_w, cs, bt, g_rows, norm_w, seq, pad):
    b, lp, _ = proj.shape
    nc = lp // CHUNK
    conv_dim = 3 * GDN_DQK
    qkv = lambda blk: pl.BlockSpec((1, CHUNK, GDN_DQK), lambda i, c: (i, c, blk))
    return pl.pallas_call(
        functools.partial(_gdn_body, pad=pad),
        grid=(b, nc),
        in_specs=[
            qkv(0), qkv(1), qkv(2),
            pl.BlockSpec((CONV_K, conv_dim), lambda i, c: (0, 0)),
            pl.BlockSpec((1, CHUNK, LANES), lambda i, c: (i, c, 0)),
            pl.BlockSpec((1, CHUNK, LANES), lambda i, c: (i, c, 0)),
            pl.BlockSpec((1, 1, GDN_HEADS, CHUNK), lambda i, c: (i, c, 0, 0)),
            pl.BlockSpec((1, GDN_DV), lambda i, c: (0, 0)),
            pl.BlockSpec(((CONV_K - 1) * CHUNK, CONV_HALO + CHUNK), lambda i, c: (0, 0)),
        ],
        out_specs=pl.BlockSpec((1, CHUNK, GDN_DQK), lambda i, c: (i, jnp.maximum(c - 1, 0), 0)),
        out_shape=jax.ShapeDtypeStruct((b, seq, GDN_HEADS * GDN_DV), BF16),
        scratch_shapes=[
            pltpu.VMEM((CONV_HALO + CHUNK, conv_dim), BF16),
            pltpu.VMEM(((CONV_K - 1) * CHUNK, conv_dim), F32),
            pltpu.VMEM((GDN_HEADS, GDN_DK, GDN_DV), F32),
        ],
        compiler_params=_params("arbitrary", "arbitrary"),
        name="gdn_scan",
    )(proj, proj, proj, conv_w, cs, bt, g_rows, norm_w.reshape(1, GDN_DV), _shift_matrix())


def _merge_body(y_ref, z_ref, o_ref, gate_ref, g1_ref, g2_ref, nw_ref, w1_ref, w2_ref, m_ref,
                a1_ref, a2_ref):
    @pl.when(pl.program_id(1) == 0)
    def _():
        gw = SSD_D_INNER // SSD_GROUPS
        for g in range(SSD_GROUPS):
            sl = slice(g * gw, (g + 1) * gw)
            t = y_ref[:, sl].astype(F32) * _silu(z_ref[:, sl].astype(F32))
            ms = jnp.mean(t * t, axis=-1, keepdims=True)
            a1_ref[:, sl] = (t * lax.rsqrt(ms + NORM_EPS) * nw_ref[:, sl]).astype(BF16)
        a2_ref[...] = (o_ref[...].astype(F32) * _silu(gate_ref[...].astype(F32))).astype(BF16)

    m = (_sigmoid(g1_ref[...].astype(F32)) * _dot(a1_ref[...], w1_ref[...])
         + _sigmoid(g2_ref[...].astype(F32)) * _dot(a2_ref[...], w2_ref[...]))
    m_ref[...] = m.astype(m_ref.dtype)


def _merge(y, o, proj_tok, norm_w, w1, w2, tm_pref=512, tn_pref=1024):
    m, d = y.shape
    n = w1.shape[1]
    tm = _divisor_tile(m, tm_pref, 16)
    tn = _divisor_tile(n, tn_pref, LANES)
    nj = n // tn
    row_d = lambda blk: pl.BlockSpec((tm, d), lambda i, j: (i, blk))
    return pl.pallas_call(
        _merge_body,
        grid=(m // tm, nj),
        in_specs=[
            pl.BlockSpec((tm, d), lambda i, j: (i, 0)),
            row_d(0),
            pl.BlockSpec((tm, d), lambda i, j: (i, 0)),
            row_d(1),
            pl.BlockSpec((tm, tn), lambda i, j: (i, 2 * nj + j)),
            pl.BlockSpec((tm, tn), lambda i, j: (i, 3 * nj + j)),
            pl.BlockSpec((1, d), lambda i, j: (0, 0)),
            pl.BlockSpec((d, tn), lambda i, j: (0, j)),
            pl.BlockSpec((d, tn), lambda i, j: (0, j)),
        ],
        out_specs=pl.BlockSpec((tm, tn), lambda i, j: (i, j)),
        out_shape=jax.ShapeDtypeStruct((m, n), BF16),
        scratch_shapes=[pltpu.VMEM((tm, d), BF16), pltpu.VMEM((tm, d), BF16)],
        compiler_params=_params("arbitrary", "arbitrary"),
        name="merge",
    )(y, proj_tok, o, proj_tok, proj_tok, proj_tok, norm_w.reshape(1, d), w1, w2)


def _matmul_residual_body(a_ref, w_ref, r_ref, o_ref):
    o_ref[...] = r_ref[...] + _dot(a_ref[...], w_ref[...])


def _matmul_residual(a, w, res, tm_pref=512, tn_pref=1024):
    m, k = a.shape
    n = w.shape[1]
    tm = _divisor_tile(m, tm_pref, 8)
    tn = _divisor_tile(n, tn_pref, LANES)
    return pl.pallas_call(
        _matmul_residual_body,
        grid=(m // tm, n // tn),
        in_specs=[
            pl.BlockSpec((tm, k), lambda i, j: (i, 0)),
            pl.BlockSpec((k, tn), lambda i, j: (0, j)),
            pl.BlockSpec((tm, tn), lambda i, j: (i, j)),
        ],
        out_specs=pl.BlockSpec((tm, tn), lambda i, j: (i, j)),
        out_shape=jax.ShapeDtypeStruct((m, n), F32),
        compiler_params=_params("arbitrary", "arbitrary"),
        name="out_proj_residual",
    )(a, w, res)


def _top_values(x, count):
    out = []
    for _ in range(count):
        m = jnp.max(x, axis=0, keepdims=True)
        out.append(m)
        x = jnp.where(x == m, -jnp.inf, x)
    return out


def _peer_route_body(q_ref, keys_ref, s2_ref, e2_ref, thr_ref, f1_ref):
    k1 = keys_ref[0].astype(BF16)
    k2 = keys_ref[1].astype(BF16)
    kk = PEER_TOPK
    for h in range(PEER_HEADS):
        lo = 2 * h * PEER_D_KEY
        q1 = q_ref[:, lo:lo + PEER_D_KEY]
        q2 = q_ref[:, lo + PEER_D_KEY:lo + 2 * PEER_D_KEY]
        s1 = _dot_nt(k1, q1)
        s2 = _dot_nt(k2, q2)
        t1 = _top_values(s1, kk + 1)
        t2 = _top_values(s2, kk + 1)
        t1_blk = jnp.concatenate(t1[:kk], axis=0)
        t2_blk = jnp.concatenate(t2[:kk], axis=0)
        half = kk // 2
        cand = jnp.concatenate(
            [t1[0] + t2_blk]
            + [t1[a] + t2_blk[:half] for a in range(1, half)]
            + [t1_blk[half:] + t2[0]], axis=0)
        best = _top_values(cand, kk + 1)
        runner_up = jnp.maximum(best[kk], jnp.maximum(t1[kk] + t2[0], t1[0] + t2[kk]))
        tau = 0.5 * (best[kk - 1] + runner_up)
        m1, m2 = t1[0], t2[0]
        z = jnp.sum(jnp.where(cand >= tau, jnp.exp(cand - (m1 + m2)), 0.0), axis=0, keepdims=True)
        e2 = jnp.exp(s2 - m2)
        for s in range(s2_ref.shape[1]):
            ln = slice(s * LANES, (s + 1) * LANES)
            s2_ref[h, s] = s2[:, ln]
            e2_ref[h, s] = e2[:, ln]
        thr_ref[h] = tau - s1
        f1_ref[h] = jnp.exp(s1 - m1) / z


def _peer_route(q, keys, t_tok):
    ntok = q.shape[0]
    spec = pl.BlockSpec((PEER_HEADS, PEER_N_KEYS, t_tok), lambda t: (0, 0, t))
    shp = jax.ShapeDtypeStruct((PEER_HEADS, PEER_N_KEYS, ntok), F32)
    strip_spec = pl.BlockSpec((PEER_HEADS, t_tok // LANES, PEER_N_KEYS, LANES), lambda t: (0, t, 0, 0))
    strip_shp = jax.ShapeDtypeStruct((PEER_HEADS, ntok // LANES, PEER_N_KEYS, LANES), F32)
    return pl.pallas_call(
        _peer_route_body,
        grid=(ntok // t_tok,),
        in_specs=[
            pl.BlockSpec((t_tok, q.shape[1]), lambda t: (t, 0)),
            pl.BlockSpec(keys.shape, lambda t: (0, 0, 0)),
        ],
        out_specs=(strip_spec, strip_spec, spec, spec),
        out_shape=(strip_shp, strip_shp, shp, shp),
        compiler_params=_params("arbitrary"),
        name="peer_route",
    )(q, keys)


def _peer_expert_body(hnt_ref, s2_ref, e2_ref, thr_ref, f1_ref, down_ref, upt_ref, acc_ref,
                      w_ref, act_ref, p_ref):
    d, t_tok = hnt_ref.shape
    te = down_ref.shape[0]
    n_part = t_tok // LANES
    rows_a = te // n_part
    rows_o = d // n_part
    e = pl.program_id(1)
    last = pl.num_programs(1) - 1
    slot = lax.rem(e, 2)

    def run(build, project):
        for s in range(n_part):
            if build:
                ln = slice(s * LANES, (s + 1) * LANES)
                for i in range(te // PEER_N_KEYS):
                    w = None
                    for hd in range(PEER_HEADS):
                        thr = thr_ref[hd, i:i + 1, ln]
                        f1 = f1_ref[hd, i:i + 1, ln]
                        term = f1 * jnp.where(s2_ref[hd, s] >= thr, e2_ref[hd, s], 0.0)
                        w = term if w is None else w + term
                    w_ref[i * PEER_N_KEYS:(i + 1) * PEER_N_KEYS, ln] = w
                ra = slice(s * rows_a, (s + 1) * rows_a)
                act_ref[ra, :] = _dot(down_ref[ra, :], hnt_ref[...])
            if project:
                ro = slice(s * rows_o, (s + 1) * rows_o)
                acc_ref[ro, :] += _dot(upt_ref[ro, :], p_ref[1 - slot])
        if build:
            a = act_ref[...]
            gelu = 0.5 * a * (1.0 + lax.erf(a * (2.0 ** -0.5)))
            p_ref[slot] = (w_ref[...] * gelu).astype(BF16)

    @pl.when(e == 0)
    def _():
        acc_ref[...] = jnp.zeros_like(acc_ref)
        run(build=True, project=False)

    @pl.when(jnp.logical_and(e > 0, e < last))
    def _():
        run(build=True, project=True)

    @pl.when(e == last)
    def _():
        run(build=False, project=True)


def _peer_experts(hn_t, s2, e2, thr, f1, down, up_t, t_tok, keys_per_block):
    d, ntok = hn_t.shape
    n_exp = down.shape[0]
    te = keys_per_block * PEER_N_KEYS
    n_blk = n_exp // te
    cur = lambda e: jnp.minimum(e, n_blk - 1)
    prev = lambda e: jnp.maximum(e - 1, 0)
    full = pl.BlockSpec((PEER_HEADS, t_tok // LANES, PEER_N_KEYS, LANES), lambda t, e: (0, t, 0, 0))
    part = pl.BlockSpec((PEER_HEADS, keys_per_block, t_tok), lambda t, e: (0, cur(e), t))
    return pl.pallas_call(
        _peer_expert_body,
        grid=(ntok // t_tok, n_blk + 1),
        in_specs=[
            pl.BlockSpec((d, t_tok), lambda t, e: (0, t)),
            full, full, part, part,
            pl.BlockSpec((te, d), lambda t, e: (cur(e), 0)),
            pl.BlockSpec((d, te), lambda t, e: (0, prev(e))),
        ],
        out_specs=pl.BlockSpec((d, t_tok), lambda t, e: (0, t)),
        out_shape=jax.ShapeDtypeStruct((d, ntok), F32),
        scratch_shapes=[
            pltpu.VMEM((te, t_tok), F32),
            pltpu.VMEM((te, t_tok), F32),
            pltpu.VMEM((2, te, t_tok), BF16),
        ],
        compiler_params=_params("arbitrary", "arbitrary"),
        name="peer_experts",
    )(hn_t, s2, e2, thr, f1, down, up_t)


def _residual_norm_body(h_ref, acc_ref, g_ref, o_ref):
    y = h_ref[...] + acc_ref[...].T
    ms = jnp.mean(y * y, axis=-1, keepdims=True)
    o_ref[...] = y * lax.rsqrt(ms + NORM_EPS) * g_ref[...]


def _residual_norm(h2, acc_t, gain, tm_pref=256):
    m, d = h2.shape
    tm = _divisor_tile(m, tm_pref, LANES)
    return pl.pallas_call(
        _residual_norm_body,
        grid=(m // tm,),
        in_specs=[
            pl.BlockSpec((tm, d), lambda i: (i, 0)),
            pl.BlockSpec((d, tm), lambda i: (0, i)),
            pl.BlockSpec((1, d), lambda i: (0, 0)),
        ],
        out_specs=pl.BlockSpec((tm, d), lambda i: (i, 0)),
        out_shape=jax.ShapeDtypeStruct((m, d), F32),
        compiler_params=_params("arbitrary"),
        name="residual_norm",
    )(h2, acc_t, gain.reshape(1, d))


def _cast_body(x_ref, o_ref, *, transpose):
    x = x_ref[...]
    o_ref[...] = (x.T if transpose else x).astype(o_ref.dtype)


def _cast_bf16(w, rows_pref, transpose=False, row_multiple=LANES):
    r, c = w.shape
    tr = _divisor_tile(r, rows_pref, row_multiple)
    out_spec = pl.BlockSpec((c, tr), lambda i: (0, i)) if transpose else pl.BlockSpec((tr, c), lambda i: (i, 0))
    return pl.pallas_call(
        functools.partial(_cast_body, transpose=transpose),
        grid=(r // tr,),
        in_specs=[pl.BlockSpec((tr, c), lambda i: (i, 0))],
        out_specs=out_spec,
        out_shape=jax.ShapeDtypeStruct((c, r) if transpose else (r, c), BF16),
        compiler_params=_params("arbitrary"),
        name="cast_bf16",
    )(w)


def _w_in_segments(d):
    widths = (("z", SSD_D_INNER), ("xbc", SSD_D_INNER + 2 * SSD_BC), ("dt", SSD_HEADS),
              ("qkv", 3 * GDN_DQK), ("gate", GDN_HEADS * GDN_DV), ("a", GDN_HEADS),
              ("beta", GDN_HEADS), ("merge", 2 * d))
    seg, off = {}, 0
    for name, wd in widths:
        seg[name] = slice(off, off + wd)
        off += wd
    return seg


def _split_w_in(w_in):
    d, n_in = w_in.shape
    seg = _w_in_segments(d)
    w16_t = _cast_bf16(jnp.swapaxes(w_in, 0, 1), rows_pref=512, row_multiple=64)
    rows = lambda names: jnp.concatenate([w16_t[seg[n]] for n in names], axis=0)
    small = rows(("dt", "a", "beta"))
    return (rows(("qkv", "xbc")), rows(("z", "gate", "merge")),
            jnp.pad(small, ((0, LANES - small.shape[0]), (0, 0))))


def _lane_vec(*parts):
    v = jnp.concatenate([p.astype(F32) for p in parts])
    return jnp.pad(v, (0, LANES - v.shape[0])).reshape(1, LANES)


def _mixer(x, meta_tokens, norm_w, w_in, ssd_conv_w, ssd_conv_b, ssd_dt_bias, ssd_A_log, ssd_D,
           ssd_norm_w, gdn_conv_w, gdn_dt_bias, gdn_A_log, gdn_norm_w, w_branch_ssd, w_branch_gdn, w_out):
    b, seq, d = x.shape
    pad = CHUNK - N_META
    lp = seq + CHUNK
    nc = lp // CHUNK
    w_scan, w_tok, w_small = _split_w_in(w_in)

    meta = jnp.broadcast_to(meta_tokens[None].astype(x.dtype), (b, N_META, d))
    hp = jnp.concatenate([jnp.zeros((b, pad, d), x.dtype), meta, x], axis=1).reshape(b * lp, d)
    x2d = x.reshape(b * seq, d)

    proj_scan = _norm_matmul(hp, norm_w, w_scan, tm_pref=1056, w_is_transposed=True).reshape(b, lp, -1)
    raw_small = _norm_matmul(hp, norm_w, w_small, tm_pref=1056, out_dtype=F32,
                             w_is_transposed=True).reshape(b, lp, LANES)
    proj_tok = _norm_matmul(x2d, norm_w, w_tok, tm_pref=1024, w_is_transposed=True)

    zeros_h = jnp.zeros((GDN_HEADS,), F32)
    bias_vec = _lane_vec(ssd_dt_bias, gdn_dt_bias, zeros_h)
    alog_vec = _lane_vec(ssd_A_log, gdn_A_log, zeros_h)
    sp, cs, bt = _gates(raw_small, bias_vec, alog_vec, pad)
    cs_c = cs.reshape(b, nc, CHUNK, LANES)
    ssd_rows = jnp.swapaxes(cs_c[..., :SSD_HEADS], 2, 3)
    gdn_rows = jnp.swapaxes(cs_c[..., SSD_HEADS:SSD_HEADS + GDN_HEADS], 2, 3)

    d_vec = jnp.repeat(ssd_D.astype(F32), SSD_HEAD_DIM).reshape(1, SSD_D_INNER)
    y = _ssd(proj_scan, ssd_conv_w, ssd_conv_b, sp, cs, ssd_rows, d_vec, seq, pad)
    o = _gdn(proj_scan, gdn_conv_w, cs, bt, gdn_rows, gdn_norm_w, seq, pad)

    m = _merge(y.reshape(b * seq, -1), o.reshape(b * seq, -1), proj_tok, ssd_norm_w,
               w_branch_ssd.astype(BF16), w_branch_gdn.astype(BF16))
    return _matmul_residual(m, w_out.astype(BF16), x2d)


def _peer(h2, norm_w, w_query, sub_keys, expert_down, expert_up, final_w):
    ntok = h2.shape[0]
    q, hn_t = _norm_matmul(h2, norm_w, w_query.astype(BF16), tm_pref=512, with_transposed=True)
    s2, e2, thr, f1 = _peer_route(q, sub_keys, _divisor_tile(ntok, 256, LANES))
    acc_t = _peer_experts(hn_t, s2, e2, thr, f1, _cast_bf16(expert_down, rows_pref=1024),
                          _cast_bf16(expert_up, rows_pref=512, transpose=True),
                          _divisor_tile(ntok, 512, LANES), keys_per_block=8)
    return _residual_norm(h2, acc_t, final_w)


def kernel(x, meta_tokens, norm_mix_w, w_in, ssd_conv_w, ssd_conv_b, ssd_dt_bias, ssd_A_log, ssd_D, ssd_norm_w, gdn_conv_w, gdn_dt_bias, gdn_A_log, gdn_norm_w, w_branch_ssd, w_branch_gdn, w_out, norm_ffn_w, peer_w_query, peer_sub_keys, peer_expert_down, peer_expert_up, final_norm_w):
    b, seq, d = x.shape
    assert norm_mix_w.shape[0] == 1, "single-layer trunk"
    h2 = _mixer(x, meta_tokens, norm_mix_w[0], w_in[0], ssd_conv_w[0], ssd_conv_b[0], ssd_dt_bias[0],
                ssd_A_log[0], ssd_D[0], ssd_norm_w[0], gdn_conv_w[0], gdn_dt_bias[0], gdn_A_log[0],
                gdn_norm_w[0], w_branch_ssd[0], w_branch_gdn[0], w_out[0])
    out = _peer(h2, norm_ffn_w[0], peer_w_query[0], peer_sub_keys[0], peer_expert_down[0],
                peer_expert_up[0], final_norm_w)
    return out.reshape(b, seq, d)
```

```python
import functools

import jax
import jax.numpy as jnp
from jax import lax
from jax.experimental import pallas as pl
from jax.experimental.pallas import tpu as pltpu

F32 = jnp.float32
BF16 = jnp.bfloat16

N_META = 16
CHUNK = 64
CONV_K = 4
NORM_EPS = 1e-6
L2_EPS = 1e-6

SSD_HEADS = 32
SSD_HEAD_DIM = 64
SSD_GROUPS = 4
SSD_STATE = 128
SSD_HPG = SSD_HEADS // SSD_GROUPS
SSD_D_INNER = SSD_HEADS * SSD_HEAD_DIM
SSD_BC = SSD_GROUPS * SSD_STATE

GDN_HEADS = 16
GDN_DK = 128
GDN_DV = 128
GDN_DQK = GDN_HEADS * GDN_DK

PEER_HEADS = 8
PEER_N_KEYS = 128
PEER_TOPK = 16
PEER_D_KEY = 128

CONV_HALO = 16
LANES = 128
VMEM_LIMIT = 56 * 1024 * 1024

_NT = (((1,), (1,)), ((), ()))
_TN = (((0,), (0,)), ((), ()))


def _dot(a, b):
    return jnp.dot(a, b, preferred_element_type=F32)


def _dot_nt(a, b):
    return lax.dot_general(a, b, _NT, preferred_element_type=F32)


def _dot_tn(a, b):
    return lax.dot_general(a, b, _TN, preferred_element_type=F32)


def _silu(x):
    return x / (1.0 + jnp.exp(-x))


def _sigmoid(x):
    return 1.0 / (1.0 + jnp.exp(-x))


def _softplus(x):
    return jnp.maximum(x, 0.0) + jnp.log(1.0 + jnp.exp(-jnp.abs(x)))


def _divisor_tile(n, preferred, multiple):
    t = min(preferred, n)
    t -= t % multiple
    while t >= multiple:
        if n % t == 0:
            return t
        t -= multiple
    return n


def _params(*sem):
    return pltpu.CompilerParams(dimension_semantics=sem, vmem_limit_bytes=VMEM_LIMIT)


def _norm_matmul_body(x_ref, g_ref, w_ref, o_ref, *rest, w_is_transposed):
    n_ref = rest[-1]
    nt_ref = rest[0] if len(rest) == 2 else None

    @pl.when(pl.program_id(1) == 0)
    def _():
        x = x_ref[...]
        ms = jnp.mean(x * x, axis=-1, keepdims=True)
        n = x * lax.rsqrt(ms + NORM_EPS) * g_ref[...]
        n_ref[...] = n.astype(BF16)
        if nt_ref is not None:
            nt_ref[...] = n.T.astype(BF16)

    dot = _dot_nt if w_is_transposed else _dot
    o_ref[...] = dot(n_ref[...], w_ref[...]).astype(o_ref.dtype)


def _segment_row(j, segments, tn):
    unit = 16
    row, first_blk = None, 0
    for start, length in segments:
        assert start % unit == 0 and tn % unit == 0
        cand = start // unit + (j - first_blk) * (tn // unit)
        row = cand if row is None else jnp.where(j >= first_blk, cand, row)
        first_blk += length // tn
    return row * unit


def _norm_matmul(x2d, gain, w, tm_pref=768, tn_pref=1024, out_dtype=BF16, with_transposed=False,
                 w_row_segments=None):
    m, k = x2d.shape
    w_is_transposed = w_row_segments is not None
    n = sum(ln for _, ln in w_row_segments) if w_is_transposed else w.shape[1]
    tm = _divisor_tile(m, tm_pref, LANES if with_transposed else 16)
    tn = _divisor_tile(n, tn_pref, LANES)
    if w_is_transposed:
        assert all(ln % tn == 0 for _, ln in w_row_segments)
        w_spec = pl.BlockSpec((pl.Element(tn), pl.Element(k)),
                              lambda i, j: (_segment_row(j, w_row_segments, tn), 0))
    else:
        w_spec = pl.BlockSpec((k, tn), lambda i, j: (0, j))
    out_specs = pl.BlockSpec((tm, tn), lambda i, j: (i, j))
    out_shape = jax.ShapeDtypeStruct((m, n), out_dtype)
    if with_transposed:
        out_specs = (out_specs, pl.BlockSpec((k, tm), lambda i, j: (0, i)))
        out_shape = (out_shape, jax.ShapeDtypeStruct((k, m), BF16))
    return pl.pallas_call(
        functools.partial(_norm_matmul_body, w_is_transposed=w_is_transposed),
        grid=(m // tm, n // tn),
        in_specs=[
            pl.BlockSpec((tm, k), lambda i, j: (i, 0)),
            pl.BlockSpec((1, k), lambda i, j: (0, 0)),
            w_spec,
        ],
        out_specs=out_specs,
        out_shape=out_shape,
        scratch_shapes=[pltpu.VMEM((tm, k), BF16)],
        compiler_params=_params("arbitrary", "arbitrary"),
        name="norm_matmul",
    )(x2d, gain.reshape(1, k), w)


def _gates_body(raw_ref, bias_ref, alog_ref, sp_ref, cs_ref, bt_ref, *, n_chunks, pad):
    ri = lax.broadcasted_iota(jnp.int32, (CHUNK, CHUNK), 0)
    ci = lax.broadcasted_iota(jnp.int32, (CHUNK, CHUNK), 1)
    tri = (ri >= ci).astype(F32)
    neg_a = -jnp.exp(alog_ref[...])
    bias = bias_ref[...]
    row = lax.broadcasted_iota(jnp.int32, (CHUNK, 1), 0)

    def body(c, carry):
        sl = pl.ds(pl.multiple_of(c * CHUNK, CHUNK), CHUNK)
        r = raw_ref[0, sl, :]
        sp = _softplus(r + bias)
        cs = jnp.dot(tri, sp * neg_a, precision=lax.Precision.HIGHEST, preferred_element_type=F32)
        valid = (row + c * CHUNK) >= pad
        sp_ref[0, sl, :] = sp
        cs_ref[0, sl, :] = cs
        bt_ref[0, sl, :] = jnp.where(valid, _sigmoid(r), 0.0)
        return carry

    lax.fori_loop(0, n_chunks, body, 0)


def _gates(raw, bias_vec, alog_vec, pad):
    b, lp, w = raw.shape
    spec = pl.BlockSpec((1, lp, w), lambda i: (i, 0, 0))
    vec = pl.BlockSpec((1, w), lambda i: (0, 0))
    shp = jax.ShapeDtypeStruct((b, lp, w), F32)
    return pl.pallas_call(
        functools.partial(_gates_body, n_chunks=lp // CHUNK, pad=pad),
        grid=(b,),
        in_specs=[spec, vec, vec],
        out_specs=(spec, spec, spec),
        out_shape=(shp, shp, shp),
        compiler_params=_params("arbitrary"),
        name="gates",
    )(raw, bias_vec, alog_vec)


def _shift_matrix():
    r = jnp.arange((CONV_K - 1) * CHUNK)
    src = CONV_HALO + r % CHUNK - (CONV_K - 1 - r // CHUNK)
    return (jnp.arange(CONV_HALO + CHUNK)[None, :] == src[:, None]).astype(BF16)


def _shifted_taps(shift_ref, xpad_ref, taps_ref):
    taps_ref[...] = _dot(shift_ref[...], xpad_ref[...])


def _causal_conv(taps_ref, xpad_ref, w_ref, lo, width, bias):
    cols = slice(lo, lo + width)
    acc = w_ref[CONV_K - 1:CONV_K, cols] * xpad_ref[CONV_HALO:CONV_HALO + CHUNK, cols].astype(F32)
    if bias is not None:
        acc = acc + bias
    for j in range(CONV_K - 1):
        acc = acc + w_ref[j:j + 1, cols] * taps_ref[j * CHUNK:(j + 1) * CHUNK, cols]
    return acc


def _ssd_body(x_ref, bc_ref, cw_ref, cb_ref, sp_ref, cs_ref, csr_ref, dvec_ref, expand_ref, shift_ref,
              y_ref, xpad_ref, taps_ref, st_ref, *, pad):
    c = pl.program_id(1)

    @pl.when(c == 0)
    def _():
        st_ref[...] = jnp.zeros_like(st_ref)
        xpad_ref[0:CONV_HALO, :] = jnp.zeros((CONV_HALO, xpad_ref.shape[1]), BF16)

    xpad_ref[CONV_HALO:CONV_HALO + CHUNK, 0:SSD_D_INNER] = x_ref[0]
    xpad_ref[CONV_HALO:CONV_HALO + CHUNK, SSD_D_INNER:] = bc_ref[0]
    _shifted_taps(shift_ref, xpad_ref, taps_ref)

    row = lax.broadcasted_iota(jnp.int32, (CHUNK, 1), 0)
    valid = (row + c * CHUNK) >= pad
    ri = lax.broadcasted_iota(jnp.int32, (CHUNK, CHUNK), 0)
    ci = lax.broadcasted_iota(jnp.int32, (CHUNK, CHUNK), 1)
    tril = ri >= ci

    sp = sp_ref[0]
    cs = cs_ref[0]
    csr = csr_ref[0, 0]

    def conv(lo, width):
        a = _causal_conv(taps_ref, xpad_ref, cw_ref, lo, width, cb_ref[:, lo:lo + width])
        return jnp.where(valid, _silu(a), 0.0)

    gw = SSD_HPG * SSD_HEAD_DIM
    groups = range(SSD_GROUPS)
    heads = range(SSD_HEADS)
    xg = [conv(g * gw, gw) for g in groups]
    bg = [conv(SSD_D_INNER + g * SSD_STATE, SSD_STATE).astype(BF16) for g in groups]
    cg = [conv(SSD_D_INNER + SSD_BC + g * SSD_STATE, SSD_STATE).astype(BF16) for g in groups]
    cb = [_dot_nt(cg[g], bg[g]) for g in groups]
    st = [st_ref[g] for g in groups]
    yoff = [_dot(cg[g], st[g].astype(BF16)) for g in groups]

    last_row = cs[CHUNK - 1:CHUNK, :]
    factors = (sp, sp * jnp.exp(last_row - cs), jnp.exp(cs))
    pieces = []
    for f in factors:
        hi = f.astype(BF16)
        pieces += [hi, (f - hi.astype(F32)).astype(BF16)]
    spread = _dot(jnp.concatenate(pieces, axis=0), expand_ref[...])
    e_dt, e_w, e_cs = (spread[(2 * k) * CHUNK:(2 * k + 1) * CHUNK] + spread[(2 * k + 1) * CHUNK:(2 * k + 2) * CHUNK]
                       for k in range(len(factors)))

    xdt = [(xg[g] * e_dt[:, g * gw:(g + 1) * gw]).astype(BF16) for g in groups]
    xw = [(xg[g] * e_w[:, g * gw:(g + 1) * gw]).astype(BF16) for g in groups]
    w_intra = []
    for h in heads:
        decay = jnp.where(tril, jnp.exp(cs[:, h:h + 1] - csr[h:h + 1, :]), 0.0)
        w_intra.append((cb[h // SSD_HPG] * decay).astype(BF16))
    ydiag = [_dot(w_intra[h], xdt[h // SSD_HPG][:, (h % SSD_HPG) * SSD_HEAD_DIM:(h % SSD_HPG + 1) * SSD_HEAD_DIM])
             for h in heads]
    snew = [_dot_tn(bg[g], xw[g]) for g in groups]
    for g in groups:
        sl = slice(g * gw, (g + 1) * gw)
        st_ref[g] = st[g] * e_cs[CHUNK - 1:CHUNK, sl] + snew[g]
        y_g = (jnp.concatenate(ydiag[g * SSD_HPG:(g + 1) * SSD_HPG], axis=1) + yoff[g] * e_cs[:, sl]
               + dvec_ref[:, sl] * xg[g])
        y_ref[0, :, sl] = y_g.astype(y_ref.dtype)

    xpad_ref[0:CONV_HALO, :] = xpad_ref[CHUNK:CHUNK + CONV_HALO, :]


def _ssd(proj, conv_w, conv_b, sp, cs, cs_rows, d_vec, seq, pad):
    b, lp, _ = proj.shape
    nc = lp // CHUNK
    conv_dim = SSD_D_INNER + 2 * SSD_BC
    x_blk = 3 * GDN_DQK // SSD_D_INNER
    bc_blk = (3 * GDN_DQK + SSD_D_INNER) // (2 * SSD_BC)
    expand = (jnp.arange(LANES)[:, None] == jnp.arange(SSD_D_INNER)[None, :] // SSD_HEAD_DIM).astype(BF16)
    return pl.pallas_call(
        functools.partial(_ssd_body, pad=pad),
        grid=(b, nc),
        in_specs=[
            pl.BlockSpec((1, CHUNK, SSD_D_INNER), lambda i, c: (i, c, x_blk)),
            pl.BlockSpec((1, CHUNK, 2 * SSD_BC), lambda i, c: (i, c, bc_blk)),
            pl.BlockSpec((CONV_K, conv_dim), lambda i, c: (0, 0)),
            pl.BlockSpec((1, conv_dim), lambda i, c: (0, 0)),
            pl.BlockSpec((1, CHUNK, LANES), lambda i, c: (i, c, 0)),
            pl.BlockSpec((1, CHUNK, LANES), lambda i, c: (i, c, 0)),
            pl.BlockSpec((1, 1, SSD_HEADS, CHUNK), lambda i, c: (i, c, 0, 0)),
            pl.BlockSpec((1, SSD_D_INNER), lambda i, c: (0, 0)),
            pl.BlockSpec((LANES, SSD_D_INNER), lambda i, c: (0, 0)),
            pl.BlockSpec(((CONV_K - 1) * CHUNK, CONV_HALO + CHUNK), lambda i, c: (0, 0)),
        ],
        out_specs=pl.BlockSpec((1, CHUNK, SSD_D_INNER), lambda i, c: (i, jnp.maximum(c - 1, 0), 0)),
        out_shape=jax.ShapeDtypeStruct((b, seq, SSD_D_INNER), BF16),
        scratch_shapes=[
            pltpu.VMEM((CONV_HALO + CHUNK, conv_dim), BF16),
            pltpu.VMEM(((CONV_K - 1) * CHUNK, conv_dim), F32),
            pltpu.VMEM((SSD_GROUPS, SSD_STATE, SSD_HPG * SSD_HEAD_DIM), F32),
        ],
        compiler_params=_params("arbitrary", "arbitrary"),
        name="ssd_scan",
    )(proj, proj, conv_w, conv_b.reshape(1, conv_dim), sp, cs, cs_rows, d_vec, expand, _shift_matrix())


def _unit_lower_inverses(strict_ls, eye):
    ps = [l.astype(BF16) for l in strict_ls]
    ts = [eye - l for l in strict_ls]
    n = 2
    while n < CHUNK:
        p2 = [_dot(p, p).astype(BF16) for p in ps]
        ts = [t + _dot(t.astype(BF16), q) for t, q in zip(ts, p2)]
        ps = p2
        n *= 2
    return ts


def _gdn_body(q_ref, k_ref, v_ref, cw_ref, cs_ref, bt_ref, gr_ref, nw_ref, shift_ref, o_ref,
              xpad_ref, taps_ref, st_ref, *, pad):
    c = pl.program_id(1)

    @pl.when(c == 0)
    def _():
        st_ref[...] = jnp.zeros_like(st_ref)
        xpad_ref[0:CONV_HALO, :] = jnp.zeros((CONV_HALO, xpad_ref.shape[1]), BF16)

    xpad_ref[CONV_HALO:CONV_HALO + CHUNK, 0:GDN_DQK] = q_ref[0]
    xpad_ref[CONV_HALO:CONV_HALO + CHUNK, GDN_DQK:2 * GDN_DQK] = k_ref[0]
    xpad_ref[CONV_HALO:CONV_HALO + CHUNK, 2 * GDN_DQK:] = v_ref[0]
    _shifted_taps(shift_ref, xpad_ref, taps_ref)

    row = lax.broadcasted_iota(jnp.int32, (CHUNK, 1), 0)
    valid = (row + c * CHUNK) >= pad
    ri = lax.broadcasted_iota(jnp.int32, (CHUNK, CHUNK), 0)
    ci = lax.broadcasted_iota(jnp.int32, (CHUNK, CHUNK), 1)
    tril = ri >= ci
    strict = ri > ci
    eye = (ri == ci).astype(F32)

    cs = cs_ref[0]
    bt = bt_ref[0]
    gr = gr_ref[0, 0]
    nw = nw_ref[...]
    g_off = SSD_HEADS
    b_off = SSD_HEADS + GDN_HEADS

    def conv(lo):
        a = _causal_conv(taps_ref, xpad_ref, cw_ref, lo, GDN_DK, None)
        return jnp.where(valid, _silu(a), 0.0)

    heads = range(GDN_HEADS)
    q16, k16, kb16, rhs16, qg, kd16, decay, elast = [], [], [], [], [], [], [], []
    for h in heads:
        q = conv(h * GDN_DK)
        k = conv(GDN_DQK + h * GDN_DK)
        v = conv(2 * GDN_DQK + h * GDN_DV)
        q = q * lax.rsqrt(jnp.sum(q * q, axis=-1, keepdims=True) + L2_EPS) * (GDN_DK ** -0.5)
        k = k * lax.rsqrt(jnp.sum(k * k, axis=-1, keepdims=True) + L2_EPS)
        col = cs[:, g_off + h:g_off + h + 1]
        rowv = gr[h:h + 1, :]
        last = rowv[:, CHUNK - 1:CHUNK]
        beta = bt[:, b_off + h:b_off + h + 1]
        ecol = jnp.exp(col)
        kb = k * beta
        decay.append(jnp.where(tril, jnp.exp(col - rowv), 0.0))
        q16.append(q.astype(BF16))
        k16.append(k.astype(BF16))
        kb16.append(kb.astype(BF16))
        rhs16.append(jnp.concatenate([v * beta, kb * ecol], axis=1).astype(BF16))
        qg.append(q * ecol)
        kd16.append((k * jnp.exp(last - col)).astype(BF16))
        elast.append(jnp.exp(last))
    kk = [_dot_nt(kb16[h], k16[h]) for h in heads]
    qk = [_dot_nt(q16[h], k16[h]) for h in heads]
    tinv = _unit_lower_inverses([jnp.where(strict, kk[h] * decay[h], 0.0) for h in heads], eye)
    uw = [_dot(tinv[h].astype(BF16), rhs16[h]) for h in heads]
    s = [st_ref[h] for h in heads]
    ws = [_dot(jnp.concatenate([uw[h][:, GDN_DV:], qg[h]], axis=0).astype(BF16), s[h].astype(BF16))
          for h in heads]
    v16 = [(uw[h][:, :GDN_DV] - ws[h][:CHUNK]).astype(BF16) for h in heads]
    o_intra = [_dot((qk[h] * decay[h]).astype(BF16), v16[h]) for h in heads]
    s_add = [_dot_tn(kd16[h], v16[h]) for h in heads]
    for h in heads:
        st_ref[h] = s[h] * elast[h] + s_add[h]
        o = ws[h][CHUNK:] + o_intra[h]
        o = o * lax.rsqrt(jnp.mean(o * o, axis=-1, keepdims=True) + NORM_EPS) * nw
        o_ref[0, :, h * GDN_DV:(h + 1) * GDN_DV] = o.astype(o_ref.dtype)

    xpad_ref[0:CONV_HALO, :] = xpad_ref[CHUNK:CHUNK + CONV_HALO, :]


def _gdn(proj, conv_w, cs, bt, g_rows, norm_w, seq, pad):
    b, lp, _ = proj.shape
    nc = lp // CHUNK
    conv_dim = 3 * GDN_DQK
    qkv = lambda blk: pl.BlockSpec((1, CHUNK, GDN_DQK), lambda i, c: (i, c, blk))
    return pl.pallas_call(
        functools.partial(_gdn_body, pad=pad),
        grid=(b, nc),
        in_specs=[
            qkv(0), qkv(1), qkv(2),
            pl.BlockSpec((CONV_K, conv_dim), lambda i, c: (0, 0)),
            pl.BlockSpec((1, CHUNK, LANES), lambda i, c: (i, c, 0)),
            pl.BlockSpec((1, CHUNK, LANES), lambda i, c: (i, c, 0)),
            pl.BlockSpec((1, 1, GDN_HEADS, CHUNK), lambda i, c: (i, c, 0, 0)),
            pl.BlockSpec((1, GDN_DV), lambda i, c: (0, 0)),
            pl.BlockSpec(((CONV_K - 1) * CHUNK, CONV_HALO + CHUNK), lambda i, c: (0, 0)),
        ],
        out_specs=pl.BlockSpec((1, CHUNK, GDN_DQK), lambda i, c: (i, jnp.maximum(c - 1, 0), 0)),
        out_shape=jax.ShapeDtypeStruct((b, seq, GDN_HEADS * GDN_DV), BF16),
        scratch_shapes=[
            pltpu.VMEM((CONV_HALO + CHUNK, conv_dim), BF16),
            pltpu.VMEM(((CONV_K - 1) * CHUNK, conv_dim), F32),
            pltpu.VMEM((GDN_HEADS, GDN_DK, GDN_DV), F32),
        ],
        compiler_params=_params("arbitrary", "arbitrary"),
        name="gdn_scan",
    )(proj, proj, proj, conv_w, cs, bt, g_rows, norm_w.reshape(1, GDN_DV), _shift_matrix())


def _merge_body(y_ref, z_ref, o_ref, gate_ref, g1_ref, g2_ref, nw_ref, w1_ref, w2_ref, m_ref,
                a1_ref, a2_ref):
    @pl.when(pl.program_id(1) == 0)
    def _():
        gw = SSD_D_INNER // SSD_GROUPS
        for g in range(SSD_GROUPS):
            sl = slice(g * gw, (g + 1) * gw)
            t = y_ref[:, sl].astype(F32) * _silu(z_ref[:, sl].astype(F32))
            ms = jnp.mean(t * t, axis=-1, keepdims=True)
            a1_ref[:, sl] = (t * lax.rsqrt(ms + NORM_EPS) * nw_ref[:, sl]).astype(BF16)
        a2_ref[...] = (o_ref[...].astype(F32) * _silu(gate_ref[...].astype(F32))).astype(BF16)

    m = (_sigmoid(g1_ref[...].astype(F32)) * _dot(a1_ref[...], w1_ref[...])
         + _sigmoid(g2_ref[...].astype(F32)) * _dot(a2_ref[...], w2_ref[...]))
    m_ref[...] = m.astype(m_ref.dtype)


def _merge(y, o, proj_tok, norm_w, w1, w2, tm_pref=512, tn_pref=1024):
    m, d = y.shape
    n = w1.shape[1]
    tm = _divisor_tile(m, tm_pref, 16)
    tn = _divisor_tile(n, tn_pref, LANES)
    nj = n // tn
    row_d = lambda blk: pl.BlockSpec((tm, d), lambda i, j: (i, blk))
    return pl.pallas_call(
        _merge_body,
        grid=(m // tm, nj),
        in_specs=[
            pl.BlockSpec((tm, d), lambda i, j: (i, 0)),
            row_d(0),
            pl.BlockSpec((tm, d), lambda i, j: (i, 0)),
            row_d(1),
            pl.BlockSpec((tm, tn), lambda i, j: (i, 2 * nj + j)),
            pl.BlockSpec((tm, tn), lambda i, j: (i, 3 * nj + j)),
            pl.BlockSpec((1, d), lambda i, j: (0, 0)),
            pl.BlockSpec((d, tn), lambda i, j: (0, j)),
            pl.BlockSpec((d, tn), lambda i, j: (0, j)),
        ],
        out_specs=pl.BlockSpec((tm, tn), lambda i, j: (i, j)),
        out_shape=jax.ShapeDtypeStruct((m, n), BF16),
        scratch_shapes=[pltpu.VMEM((tm, d), BF16), pltpu.VMEM((tm, d), BF16)],
        compiler_params=_params("arbitrary", "arbitrary"),
        name="merge",
    )(y, proj_tok, o, proj_tok, proj_tok, proj_tok, norm_w.reshape(1, d), w1, w2)


def _matmul_residual_body(a_ref, w_ref, r_ref, o_ref):
    o_ref[...] = r_ref[...] + _dot(a_ref[...], w_ref[...])


def _matmul_residual(a, w, res, tm_pref=512, tn_pref=1024):
    m, k = a.shape
    n = w.shape[1]
    tm = _divisor_tile(m, tm_pref, 8)
    tn = _divisor_tile(n, tn_pref, LANES)
    return pl.pallas_call(
        _matmul_residual_body,
        grid=(m // tm, n // tn),
        in_specs=[
            pl.BlockSpec((tm, k), lambda i, j: (i, 0)),
            pl.BlockSpec((k, tn), lambda i, j: (0, j)),
            pl.BlockSpec((tm, tn), lambda i, j: (i, j)),
        ],
        out_specs=pl.BlockSpec((tm, tn), lambda i, j: (i, j)),
        out_shape=jax.ShapeDtypeStruct((m, n), F32),
        compiler_params=_params("arbitrary", "arbitrary"),
        name="out_proj_residual",
    )(a, w, res)


def _top_values(x, count):
    out = []
    for _ in range(count):
        m = jnp.max(x, axis=0, keepdims=True)
        out.append(m)
        x = jnp.where(x == m, -jnp.inf, x)
    return out


def _peer_route_body(q_ref, keys_ref, s2_ref, e2_ref, thr_ref, f1_ref):
    k1 = keys_ref[0].astype(BF16)
    k2 = keys_ref[1].astype(BF16)
    kk = PEER_TOPK
    for h in range(PEER_HEADS):
        lo = 2 * h * PEER_D_KEY
        q1 = q_ref[:, lo:lo + PEER_D_KEY]
        q2 = q_ref[:, lo + PEER_D_KEY:lo + 2 * PEER_D_KEY]
        s1 = _dot_nt(k1, q1)
        s2 = _dot_nt(k2, q2)
        t1 = _top_values(s1, kk + 1)
        t2 = _top_values(s2, kk + 1)
        t1_blk = jnp.concatenate(t1[:kk], axis=0)
        t2_blk = jnp.concatenate(t2[:kk], axis=0)
        half = kk // 2
        cand = jnp.concatenate(
            [t1[0] + t2_blk]
            + [t1[a] + t2_blk[:half] for a in range(1, half)]
            + [t1_blk[half:] + t2[0]], axis=0)
        best = _top_values(cand, kk + 1)
        runner_up = jnp.maximum(best[kk], jnp.maximum(t1[kk] + t2[0], t1[0] + t2[kk]))
        tau = 0.5 * (best[kk - 1] + runner_up)
        m1, m2 = t1[0], t2[0]
        z = jnp.sum(jnp.where(cand >= tau, jnp.exp(cand - (m1 + m2)), 0.0), axis=0, keepdims=True)
        e2 = jnp.exp(s2 - m2)
        for s in range(s2_ref.shape[1]):
            ln = slice(s * LANES, (s + 1) * LANES)
            s2_ref[h, s] = s2[:, ln]
            e2_ref[h, s] = e2[:, ln]
        thr_ref[h] = tau - s1
        f1_ref[h] = jnp.exp(s1 - m1) / z


def _peer_route(q, keys, t_tok):
    ntok = q.shape[0]
    spec = pl.BlockSpec((PEER_HEADS, PEER_N_KEYS, t_tok), lambda t: (0, 0, t))
    shp = jax.ShapeDtypeStruct((PEER_HEADS, PEER_N_KEYS, ntok), F32)
    strip_spec = pl.BlockSpec((PEER_HEADS, t_tok // LANES, PEER_N_KEYS, LANES), lambda t: (0, t, 0, 0))
    strip_shp = jax.ShapeDtypeStruct((PEER_HEADS, ntok // LANES, PEER_N_KEYS, LANES), F32)
    return pl.pallas_call(
        _peer_route_body,
        grid=(ntok // t_tok,),
        in_specs=[
            pl.BlockSpec((t_tok, q.shape[1]), lambda t: (t, 0)),
            pl.BlockSpec(keys.shape, lambda t: (0, 0, 0)),
        ],
        out_specs=(strip_spec, strip_spec, spec, spec),
        out_shape=(strip_shp, strip_shp, shp, shp),
        compiler_params=_params("arbitrary"),
        name="peer_route",
    )(q, keys)


def _peer_expert_body(hnt_ref, s2_ref, e2_ref, thr_ref, f1_ref, down_ref, upt_ref, acc_ref,
                      w_ref, act_ref, p_ref):
    d, t_tok = hnt_ref.shape
    te = down_ref.shape[0]
    n_part = t_tok // LANES
    rows_a = te // n_part
    rows_o = d // n_part
    e = pl.program_id(1)
    last = pl.num_programs(1) - 1
    slot = lax.rem(e, 2)

    def run(build, project):
        for s in range(n_part):
            if build:
                ln = slice(s * LANES, (s + 1) * LANES)
                for i in range(te // PEER_N_KEYS):
                    w = None
                    for hd in range(PEER_HEADS):
                        thr = thr_ref[hd, i:i + 1, ln]
                        f1 = f1_ref[hd, i:i + 1, ln]
                        term = f1 * jnp.where(s2_ref[hd, s] >= thr, e2_ref[hd, s], 0.0)
                        w = term if w is None else w + term
                    w_ref[i * PEER_N_KEYS:(i + 1) * PEER_N_KEYS, ln] = w
                ra = slice(s * rows_a, (s + 1) * rows_a)
                act_ref[ra, :] = _dot(down_ref[ra, :], hnt_ref[...])
            if project:
                ro = slice(s * rows_o, (s + 1) * rows_o)
                acc_ref[ro, :] += _dot(upt_ref[ro, :], p_ref[1 - slot])
        if build:
            a = act_ref[...]
            gelu = 0.5 * a * (1.0 + lax.erf(a * (2.0 ** -0.5)))
            p_ref[slot] = (w_ref[...] * gelu).astype(BF16)

    @pl.when(e == 0)
    def _():
        acc_ref[...] = jnp.zeros_like(acc_ref)
        run(build=True, project=False)

    @pl.when(jnp.logical_and(e > 0, e < last))
    def _():
        run(build=True, project=True)

    @pl.when(e == last)
    def _():
        run(build=False, project=True)


def _peer_experts(hn_t, s2, e2, thr, f1, down, up_t, t_tok, keys_per_block):
    d, ntok = hn_t.shape
    n_exp = down.shape[0]
    te = keys_per_block * PEER_N_KEYS
    n_blk = n_exp // te
    cur = lambda e: jnp.minimum(e, n_blk - 1)
    prev = lambda e: jnp.maximum(e - 1, 0)
    full = pl.BlockSpec((PEER_HEADS, t_tok // LANES, PEER_N_KEYS, LANES), lambda t, e: (0, t, 0, 0))
    part = pl.BlockSpec((PEER_HEADS, keys_per_block, t_tok), lambda t, e: (0, cur(e), t))
    return pl.pallas_call(
        _peer_expert_body,
        grid=(ntok // t_tok, n_blk + 1),
        in_specs=[
            pl.BlockSpec((d, t_tok), lambda t, e: (0, t)),
            full, full, part, part,
            pl.BlockSpec((te, d), lambda t, e: (cur(e), 0)),
            pl.BlockSpec((d, te), lambda t, e: (0, prev(e))),
        ],
        out_specs=pl.BlockSpec((d, t_tok), lambda t, e: (0, t)),
        out_shape=jax.ShapeDtypeStruct((d, ntok), F32),
        scratch_shapes=[
            pltpu.VMEM((te, t_tok), F32),
            pltpu.VMEM((te, t_tok), F32),
            pltpu.VMEM((2, te, t_tok), BF16),
        ],
        compiler_params=_params("arbitrary", "arbitrary"),
        name="peer_experts",
    )(hn_t, s2, e2, thr, f1, down, up_t)


def _residual_norm_body(h_ref, acc_ref, g_ref, o_ref):
    y = h_ref[...] + acc_ref[...].T
    ms = jnp.mean(y * y, axis=-1, keepdims=True)
    o_ref[...] = y * lax.rsqrt(ms + NORM_EPS) * g_ref[...]


def _residual_norm(h2, acc_t, gain, tm_pref=256):
    m, d = h2.shape
    tm = _divisor_tile(m, tm_pref, LANES)
    return pl.pallas_call(
        _residual_norm_body,
        grid=(m // tm,),
        in_specs=[
            pl.BlockSpec((tm, d), lambda i: (i, 0)),
            pl.BlockSpec((d, tm), lambda i: (0, i)),
            pl.BlockSpec((1, d), lambda i: (0, 0)),
        ],
        out_specs=pl.BlockSpec((tm, d), lambda i: (i, 0)),
        out_shape=jax.ShapeDtypeStruct((m, d), F32),
        compiler_params=_params("arbitrary"),
        name="residual_norm",
    )(h2, acc_t, gain.reshape(1, d))


def _cast_body(x_ref, o_ref, *, transpose):
    x = x_ref[...]
    o_ref[...] = (x.T if transpose else x).astype(o_ref.dtype)


def _cast_bf16(w, rows_pref, transpose=False, row_multiple=LANES):
    r, c = w.shape
    tr = _divisor_tile(r, rows_pref, row_multiple)
    out_spec = pl.BlockSpec((c, tr), lambda i: (0, i)) if transpose else pl.BlockSpec((tr, c), lambda i: (i, 0))
    return pl.pallas_call(
        functools.partial(_cast_body, transpose=transpose),
        grid=(r // tr,),
        in_specs=[pl.BlockSpec((tr, c), lambda i: (i, 0))],
        out_specs=out_spec,
        out_shape=jax.ShapeDtypeStruct((c, r) if transpose else (r, c), BF16),
        compiler_params=_params("arbitrary"),
        name="cast_bf16",
    )(w)


def _w_in_segments(d):
    widths = (("z", SSD_D_INNER), ("xbc", SSD_D_INNER + 2 * SSD_BC), ("dt", SSD_HEADS),
              ("qkv", 3 * GDN_DQK), ("gate", GDN_HEADS * GDN_DV), ("a", GDN_HEADS),
              ("beta", GDN_HEADS), ("merge", 2 * d))
    seg, off = {}, 0
    for name, wd in widths:
        seg[name] = slice(off, off + wd)
        off += wd
    return seg


def _row_segments(seg, names):
    return tuple((seg[n].start, seg[n].stop - seg[n].start) for n in names)


def _lane_vec(*parts):
    v = jnp.concatenate([p.astype(F32) for p in parts])
    return jnp.pad(v, (0, LANES - v.shape[0])).reshape(1, LANES)


def _mixer(x, meta_tokens, norm_w, w_in, ssd_conv_w, ssd_conv_b, ssd_dt_bias, ssd_A_log, ssd_D,
           ssd_norm_w, gdn_conv_w, gdn_dt_bias, gdn_A_log, gdn_norm_w, w_branch_ssd, w_branch_gdn, w_out):
    b, seq, d = x.shape
    pad = CHUNK - N_META
    lp = seq + CHUNK
    nc = lp // CHUNK
    seg = _w_in_segments(d)
    w16_t = _cast_bf16(jnp.swapaxes(w_in, 0, 1), rows_pref=512, row_multiple=64)
    small = jnp.concatenate([w16_t[seg[n]] for n in ("dt", "a", "beta")], axis=0)
    w_small = jnp.pad(small, ((0, LANES - small.shape[0]), (0, 0)))

    meta = jnp.broadcast_to(meta_tokens[None].astype(x.dtype), (b, N_META, d))
    hp = jnp.concatenate([jnp.zeros((b, pad, d), x.dtype), meta, x], axis=1).reshape(b * lp, d)
    x2d = x.reshape(b * seq, d)

    proj_scan = _norm_matmul(hp, norm_w, w16_t, tm_pref=1056,
                             w_row_segments=_row_segments(seg, ("qkv", "xbc"))).reshape(b, lp, -1)
    raw_small = _norm_matmul(hp, norm_w, w_small, tm_pref=1056, out_dtype=F32,
                             w_row_segments=((0, LANES),)).reshape(b, lp, LANES)
    proj_tok = _norm_matmul(x2d, norm_w, w16_t, tm_pref=1024,
                            w_row_segments=_row_segments(seg, ("z", "gate", "merge")))

    zeros_h = jnp.zeros((GDN_HEADS,), F32)
    bias_vec = _lane_vec(ssd_dt_bias, gdn_dt_bias, zeros_h)
    alog_vec = _lane_vec(ssd_A_log, gdn_A_log, zeros_h)
    sp, cs, bt = _gates(raw_small, bias_vec, alog_vec, pad)
    cs_c = cs.reshape(b, nc, CHUNK, LANES)
    ssd_rows = jnp.swapaxes(cs_c[..., :SSD_HEADS], 2, 3)
    gdn_rows = jnp.swapaxes(cs_c[..., SSD_HEADS:SSD_HEADS + GDN_HEADS], 2, 3)

    d_vec = jnp.repeat(ssd_D.astype(F32), SSD_HEAD_DIM).reshape(1, SSD_D_INNER)
    y = _ssd(proj_scan, ssd_conv_w, ssd_conv_b, sp, cs, ssd_rows, d_vec, seq, pad)
    o = _gdn(proj_scan, gdn_conv_w, cs, bt, gdn_rows, gdn_norm_w, seq, pad)

    m = _merge(y.reshape(b * seq, -1), o.reshape(b * seq, -1), proj_tok, ssd_norm_w,
               w_branch_ssd.astype(BF16), w_branch_gdn.astype(BF16))
    return _matmul_residual(m, w_out.astype(BF16), x2d)


def _peer(h2, norm_w, w_query, sub_keys, expert_down, expert_up, final_w):
    ntok = h2.shape[0]
    q, hn_t = _norm_matmul(h2, norm_w, w_query.astype(BF16), tm_pref=512, with_transposed=True)
    s2, e2, thr, f1 = _peer_route(q, sub_keys, _divisor_tile(ntok, 256, LANES))
    acc_t = _peer_experts(hn_t, s2, e2, thr, f1, _cast_bf16(expert_down, rows_pref=1024),
                          _cast_bf16(expert_up, rows_pref=512, transpose=True),
                          _divisor_tile(ntok, 512, LANES), keys_per_block=8)
    return _residual_norm(h2, acc_t, final_w)


def kernel(x, meta_tokens, norm_mix_w, w_in, ssd_conv_w, ssd_conv_b, ssd_dt_bias, ssd_A_log, ssd_D, ssd_norm_w, gdn_conv_w, gdn_dt_bias, gdn_A_log, gdn_norm_w, w_branch_ssd, w_branch_gdn, w_out, norm_ffn_w, peer_w_query, peer_sub_keys, peer_expert_down, peer_expert_up, final_norm_w):
    b, seq, d = x.shape
    assert norm_mix_w.shape[0] == 1, "single-layer trunk"
    h2 = _mixer(x, meta_tokens, norm_mix_w[0], w_in[0], ssd_conv_w[0], ssd_conv_b[0], ssd_dt_bias[0],
                ssd_A_log[0], ssd_D[0], ssd_norm_w[0], gdn_conv_w[0], gdn_dt_bias[0], gdn_A_log[0],
                gdn_norm_w[0], w_branch_ssd[0], w_branch_gdn[0], w_out[0])
    out = _peer(h2, norm_ffn_w[0], peer_w_query[0], peer_sub_keys[0], peer_expert_down[0],
                peer_expert_up[0], final_norm_w)
    return out.reshape(b, seq, d)
```

```python
import functools

import jax
import jax.numpy as jnp
from jax import lax
from jax.experimental import pallas as pl
from jax.experimental.pallas import tpu as pltpu

F32 = jnp.float32
BF16 = jnp.bfloat16

N_META = 16
CHUNK = 64
CONV_K = 4
NORM_EPS = 1e-6
L2_EPS = 1e-6

SSD_HEADS = 32
SSD_HEAD_DIM = 64
SSD_GROUPS = 4
SSD_STATE = 128
SSD_HPG = SSD_HEADS // SSD_GROUPS
SSD_D_INNER = SSD_HEADS * SSD_HEAD_DIM
SSD_BC = SSD_GROUPS * SSD_STATE

GDN_HEADS = 16
GDN_DK = 128
GDN_DV = 128
GDN_DQK = GDN_HEADS * GDN_DK

PEER_HEADS = 8
PEER_N_KEYS = 128
PEER_TOPK = 16
PEER_D_KEY = 128

CONV_HALO = 16
LANES = 128
VMEM_LIMIT = 56 * 1024 * 1024

_NT = (((1,), (1,)), ((), ()))
_TN = (((0,), (0,)), ((), ()))


def _dot(a, b):
    return jnp.dot(a, b, preferred_element_type=F32)


def _dot_nt(a, b):
    return lax.dot_general(a, b, _NT, preferred_element_type=F32)


def _dot_tn(a, b):
    return lax.dot_general(a, b, _TN, preferred_element_type=F32)


def _silu(x):
    return x / (1.0 + jnp.exp(-x))


def _sigmoid(x):
    return 1.0 / (1.0 + jnp.exp(-x))


def _softplus(x):
    return jnp.maximum(x, 0.0) + jnp.log(1.0 + jnp.exp(-jnp.abs(x)))


def _divisor_tile(n, preferred, multiple):
    t = min(preferred, n)
    t -= t % multiple
    while t >= multiple:
        if n % t == 0:
            return t
        t -= multiple
    return n


def _params(*sem):
    return pltpu.CompilerParams(dimension_semantics=sem, vmem_limit_bytes=VMEM_LIMIT)


def _norm_matmul_body(x_ref, g_ref, w_ref, o_ref, *rest, w_is_transposed):
    n_ref = rest[-1]
    nt_ref = rest[0] if len(rest) == 2 else None

    @pl.when(pl.program_id(1) == 0)
    def _():
        x = x_ref[...]
        ms = jnp.mean(x * x, axis=-1, keepdims=True)
        n = x * lax.rsqrt(ms + NORM_EPS) * g_ref[...]
        n_ref[...] = n.astype(BF16)
        if nt_ref is not None:
            nt_ref[...] = n.T.astype(BF16)

    dot = _dot_nt if w_is_transposed else _dot
    o_ref[...] = dot(n_ref[...], w_ref[...]).astype(o_ref.dtype)


def _segment_row(j, segments, tn):
    unit = 16
    row, first_blk = None, 0
    for start, length in segments:
        assert start % unit == 0 and tn % unit == 0
        cand = start // unit + (j - first_blk) * (tn // unit)
        row = cand if row is None else jnp.where(j >= first_blk, cand, row)
        first_blk += length // tn
    return row * unit


def _norm_matmul(x2d, gain, w, tm_pref=768, tn_pref=1024, out_dtype=BF16, with_transposed=False,
                 w_row_segments=None):
    m, k = x2d.shape
    w_is_transposed = w_row_segments is not None
    n = sum(ln for _, ln in w_row_segments) if w_is_transposed else w.shape[1]
    tm = _divisor_tile(m, tm_pref, LANES if with_transposed else 16)
    tn = _divisor_tile(n, tn_pref, LANES)
    if w_is_transposed:
        assert all(ln % tn == 0 for _, ln in w_row_segments)
        w_spec = pl.BlockSpec((pl.Element(tn), pl.Element(k)),
                              lambda i, j: (_segment_row(j, w_row_segments, tn), 0))
    else:
        w_spec = pl.BlockSpec((k, tn), lambda i, j: (0, j))
    out_specs = pl.BlockSpec((tm, tn), lambda i, j: (i, j))
    out_shape = jax.ShapeDtypeStruct((m, n), out_dtype)
    if with_transposed:
        out_specs = (out_specs, pl.BlockSpec((k, tm), lambda i, j: (0, i)))
        out_shape = (out_shape, jax.ShapeDtypeStruct((k, m), BF16))
    return pl.pallas_call(
        functools.partial(_norm_matmul_body, w_is_transposed=w_is_transposed),
        grid=(m // tm, n // tn),
        in_specs=[
            pl.BlockSpec((tm, k), lambda i, j: (i, 0)),
            pl.BlockSpec((1, k), lambda i, j: (0, 0)),
            w_spec,
        ],
        out_specs=out_specs,
        out_shape=out_shape,
        scratch_shapes=[pltpu.VMEM((tm, k), BF16)],
        compiler_params=_params("arbitrary", "arbitrary"),
        name="norm_matmul",
    )(x2d, gain.reshape(1, k), w)


def _gates_body(raw_ref, bias_ref, alog_ref, sp_ref, cs_ref, bt_ref, *, n_chunks, pad):
    ri = lax.broadcasted_iota(jnp.int32, (CHUNK, CHUNK), 0)
    ci = lax.broadcasted_iota(jnp.int32, (CHUNK, CHUNK), 1)
    tri = (ri >= ci).astype(F32)
    neg_a = -jnp.exp(alog_ref[...])
    bias = bias_ref[...]
    row = lax.broadcasted_iota(jnp.int32, (CHUNK, 1), 0)

    def body(c, carry):
        sl = pl.ds(pl.multiple_of(c * CHUNK, CHUNK), CHUNK)
        r = raw_ref[0, sl, :]
        sp = _softplus(r + bias)
        cs = jnp.dot(tri, sp * neg_a, precision=lax.Precision.HIGHEST, preferred_element_type=F32)
        valid = (row + c * CHUNK) >= pad
        sp_ref[0, sl, :] = sp
        cs_ref[0, sl, :] = cs
        bt_ref[0, sl, :] = jnp.where(valid, _sigmoid(r), 0.0)
        return carry

    lax.fori_loop(0, n_chunks, body, 0)


def _gates(raw, bias_vec, alog_vec, pad):
    b, lp, w = raw.shape
    spec = pl.BlockSpec((1, lp, w), lambda i: (i, 0, 0))
    vec = pl.BlockSpec((1, w), lambda i: (0, 0))
    shp = jax.ShapeDtypeStruct((b, lp, w), F32)
    return pl.pallas_call(
        functools.partial(_gates_body, n_chunks=lp // CHUNK, pad=pad),
        grid=(b,),
        in_specs=[spec, vec, vec],
        out_specs=(spec, spec, spec),
        out_shape=(shp, shp, shp),
        compiler_params=_params("arbitrary"),
        name="gates",
    )(raw, bias_vec, alog_vec)


def _shift_matrix():
    r = jnp.arange((CONV_K - 1) * CHUNK)
    src = CONV_HALO + r % CHUNK - (CONV_K - 1 - r // CHUNK)
    return (jnp.arange(CONV_HALO + CHUNK)[None, :] == src[:, None]).astype(BF16)


def _shifted_taps(shift_ref, xpad_ref, taps_ref):
    taps_ref[...] = _dot(shift_ref[...], xpad_ref[...])


def _causal_conv(taps_ref, xpad_ref, w_ref, lo, width, bias):
    cols = slice(lo, lo + width)
    acc = w_ref[CONV_K - 1:CONV_K, cols] * xpad_ref[CONV_HALO:CONV_HALO + CHUNK, cols].astype(F32)
    if bias is not None:
        acc = acc + bias
    for j in range(CONV_K - 1):
        acc = acc + w_ref[j:j + 1, cols] * taps_ref[j * CHUNK:(j + 1) * CHUNK, cols]
    return acc


def _ssd_body(x_ref, bc_ref, cw_ref, cb_ref, sp_ref, cs_ref, csr_ref, dvec_ref, expand_ref, shift_ref,
              y_ref, xpad_ref, taps_ref, st_ref, *, pad):
    c = pl.program_id(1)

    @pl.when(c == 0)
    def _():
        st_ref[...] = jnp.zeros_like(st_ref)
        xpad_ref[0:CONV_HALO, :] = jnp.zeros((CONV_HALO, xpad_ref.shape[1]), BF16)

    xpad_ref[CONV_HALO:CONV_HALO + CHUNK, 0:SSD_D_INNER] = x_ref[0]
    xpad_ref[CONV_HALO:CONV_HALO + CHUNK, SSD_D_INNER:] = bc_ref[0]
    _shifted_taps(shift_ref, xpad_ref, taps_ref)

    row = lax.broadcasted_iota(jnp.int32, (CHUNK, 1), 0)
    valid = (row + c * CHUNK) >= pad
    ri = lax.broadcasted_iota(jnp.int32, (CHUNK, CHUNK), 0)
    ci = lax.broadcasted_iota(jnp.int32, (CHUNK, CHUNK), 1)
    tril = ri >= ci

    sp = sp_ref[0]
    cs = cs_ref[0]
    csr = csr_ref[0, 0]

    def conv(lo, width):
        a = _causal_conv(taps_ref, xpad_ref, cw_ref, lo, width, cb_ref[:, lo:lo + width])
        return jnp.where(valid, _silu(a), 0.0)

    gw = SSD_HPG * SSD_HEAD_DIM
    groups = range(SSD_GROUPS)
    heads = range(SSD_HEADS)
    xg = [conv(g * gw, gw) for g in groups]
    bg = [conv(SSD_D_INNER + g * SSD_STATE, SSD_STATE).astype(BF16) for g in groups]
    cg = [conv(SSD_D_INNER + SSD_BC + g * SSD_STATE, SSD_STATE).astype(BF16) for g in groups]
    cb = [_dot_nt(cg[g], bg[g]) for g in groups]
    st = [st_ref[g] for g in groups]
    yoff = [_dot(cg[g], st[g].astype(BF16)) for g in groups]

    last_row = cs[CHUNK - 1:CHUNK, :]
    factors = (sp, sp * jnp.exp(last_row - cs), jnp.exp(cs))
    pieces = []
    for f in factors:
        hi = f.astype(BF16)
        pieces += [hi, (f - hi.astype(F32)).astype(BF16)]
    spread = _dot(jnp.concatenate(pieces, axis=0), expand_ref[...])
    e_dt, e_w, e_cs = (spread[(2 * k) * CHUNK:(2 * k + 1) * CHUNK] + spread[(2 * k + 1) * CHUNK:(2 * k + 2) * CHUNK]
                       for k in range(len(factors)))

    xdt = [(xg[g] * e_dt[:, g * gw:(g + 1) * gw]).astype(BF16) for g in groups]
    xw = [(xg[g] * e_w[:, g * gw:(g + 1) * gw]).astype(BF16) for g in groups]
    w_intra = []
    for h in heads:
        decay = jnp.where(tril, jnp.exp(cs[:, h:h + 1] - csr[h:h + 1, :]), 0.0)
        w_intra.append((cb[h // SSD_HPG] * decay).astype(BF16))
    ydiag = [_dot(w_intra[h], xdt[h // SSD_HPG][:, (h % SSD_HPG) * SSD_HEAD_DIM:(h % SSD_HPG + 1) * SSD_HEAD_DIM])
             for h in heads]
    snew = [_dot_tn(bg[g], xw[g]) for g in groups]
    for g in groups:
        sl = slice(g * gw, (g + 1) * gw)
        st_ref[g] = st[g] * e_cs[CHUNK - 1:CHUNK, sl] + snew[g]
        y_g = (jnp.concatenate(ydiag[g * SSD_HPG:(g + 1) * SSD_HPG], axis=1) + yoff[g] * e_cs[:, sl]
               + dvec_ref[:, sl] * xg[g])
        y_ref[0, :, sl] = y_g.astype(y_ref.dtype)

    xpad_ref[0:CONV_HALO, :] = xpad_ref[CHUNK:CHUNK + CONV_HALO, :]


def _ssd(proj, conv_w, conv_b, sp, cs, cs_rows, d_vec, seq, pad):
    b, lp, _ = proj.shape
    nc = lp // CHUNK
    conv_dim = SSD_D_INNER + 2 * SSD_BC
    x_blk = 3 * GDN_DQK // SSD_D_INNER
    bc_blk = (3 * GDN_DQK + SSD_D_INNER) // (2 * SSD_BC)
    expand = (jnp.arange(LANES)[:, None] == jnp.arange(SSD_D_INNER)[None, :] // SSD_HEAD_DIM).astype(BF16)
    return pl.pallas_call(
        functools.partial(_ssd_body, pad=pad),
        grid=(b, nc),
        in_specs=[
            pl.BlockSpec((1, CHUNK, SSD_D_INNER), lambda i, c: (i, c, x_blk)),
            pl.BlockSpec((1, CHUNK, 2 * SSD_BC), lambda i, c: (i, c, bc_blk)),
            pl.BlockSpec((CONV_K, conv_dim), lambda i, c: (0, 0)),
            pl.BlockSpec((1, conv_dim), lambda i, c: (0, 0)),
            pl.BlockSpec((1, CHUNK, LANES), lambda i, c: (i, c, 0)),
            pl.BlockSpec((1, CHUNK, LANES), lambda i, c: (i, c, 0)),
            pl.BlockSpec((1, 1, SSD_HEADS, CHUNK), lambda i, c: (i, c, 0, 0)),
            pl.BlockSpec((1, SSD_D_INNER), lambda i, c: (0, 0)),
            pl.BlockSpec((LANES, SSD_D_INNER), lambda i, c: (0, 0)),
            pl.BlockSpec(((CONV_K - 1) * CHUNK, CONV_HALO + CHUNK), lambda i, c: (0, 0)),
        ],
        out_specs=pl.BlockSpec((1, CHUNK, SSD_D_INNER), lambda i, c: (i, jnp.maximum(c - 1, 0), 0)),
        out_shape=jax.ShapeDtypeStruct((b, seq, SSD_D_INNER), BF16),
        scratch_shapes=[
            pltpu.VMEM((CONV_HALO + CHUNK, conv_dim), BF16),
            pltpu.VMEM(((CONV_K - 1) * CHUNK, conv_dim), F32),
            pltpu.VMEM((SSD_GROUPS, SSD_STATE, SSD_HPG * SSD_HEAD_DIM), F32),
        ],
        compiler_params=_params("arbitrary", "arbitrary"),
        name="ssd_scan",
    )(proj, proj, conv_w, conv_b.reshape(1, conv_dim), sp, cs, cs_rows, d_vec, expand, _shift_matrix())


def _unit_lower_inverses(strict_ls, eye):
    ps = [l.astype(BF16) for l in strict_ls]
    ts = [eye - l for l in strict_ls]
    n = 2
    while n < CHUNK:
        p2 = [_dot(p, p).astype(BF16) for p in ps]
        ts = [t + _dot(t.astype(BF16), q) for t, q in zip(ts, p2)]
        ps = p2
        n *= 2
    return ts


def _gdn_body(q_ref, k_ref, v_ref, cw_ref, cs_ref, bt_ref, gr_ref, nw_ref, shift_ref, o_ref,
              xpad_ref, taps_ref, st_ref, *, pad):
    c = pl.program_id(1)

    @pl.when(c == 0)
    def _():
        st_ref[...] = jnp.zeros_like(st_ref)
        xpad_ref[0:CONV_HALO, :] = jnp.zeros((CONV_HALO, xpad_ref.shape[1]), BF16)

    xpad_ref[CONV_HALO:CONV_HALO + CHUNK, 0:GDN_DQK] = q_ref[0]
    xpad_ref[CONV_HALO:CONV_HALO + CHUNK, GDN_DQK:2 * GDN_DQK] = k_ref[0]
    xpad_ref[CONV_HALO:CONV_HALO + CHUNK, 2 * GDN_DQK:] = v_ref[0]
    _shifted_taps(shift_ref, xpad_ref, taps_ref)

    row = lax.broadcasted_iota(jnp.int32, (CHUNK, 1), 0)
    valid = (row + c * CHUNK) >= pad
    ri = lax.broadcasted_iota(jnp.int32, (CHUNK, CHUNK), 0)
    ci = lax.broadcasted_iota(jnp.int32, (CHUNK, CHUNK), 1)
    tril = ri >= ci
    strict = ri > ci
    eye = (ri == ci).astype(F32)

    cs = cs_ref[0]
    bt = bt_ref[0]
    gr = gr_ref[0, 0]
    nw = nw_ref[...]
    g_off = SSD_HEADS
    b_off = SSD_HEADS + GDN_HEADS

    def conv(lo):
        a = _causal_conv(taps_ref, xpad_ref, cw_ref, lo, GDN_DK, None)
        return jnp.where(valid, _silu(a), 0.0)

    heads = range(GDN_HEADS)
    q16, k16, kb16, rhs16, qg, kd16, decay, elast = [], [], [], [], [], [], [], []
    for h in heads:
        q = conv(h * GDN_DK)
        k = conv(GDN_DQK + h * GDN_DK)
        v = conv(2 * GDN_DQK + h * GDN_DV)
        q = q * lax.rsqrt(jnp.sum(q * q, axis=-1, keepdims=True) + L2_EPS) * (GDN_DK ** -0.5)
        k = k * lax.rsqrt(jnp.sum(k * k, axis=-1, keepdims=True) + L2_EPS)
        col = cs[:, g_off + h:g_off + h + 1]
        rowv = gr[h:h + 1, :]
        last = rowv[:, CHUNK - 1:CHUNK]
        beta = bt[:, b_off + h:b_off + h + 1]
        ecol = jnp.exp(col)
        kb = k * beta
        decay.append(jnp.where(tril, jnp.exp(col - rowv), 0.0))
        q16.append(q.astype(BF16))
        k16.append(k.astype(BF16))
        kb16.append(kb.astype(BF16))
        rhs16.append(jnp.concatenate([v * beta, kb * ecol], axis=1).astype(BF16))
        qg.append(q * ecol)
        kd16.append((k * jnp.exp(last - col)).astype(BF16))
        elast.append(jnp.exp(last))
    kk = [_dot_nt(kb16[h], k16[h]) for h in heads]
    qk = [_dot_nt(q16[h], k16[h]) for h in heads]
    tinv = _unit_lower_inverses([jnp.where(strict, kk[h] * decay[h], 0.0) for h in heads], eye)
    uw = [_dot(tinv[h].astype(BF16), rhs16[h]) for h in heads]
    s = [st_ref[h] for h in heads]
    ws = [_dot(jnp.concatenate([uw[h][:, GDN_DV:], qg[h]], axis=0).astype(BF16), s[h].astype(BF16))
          for h in heads]
    v16 = [(uw[h][:, :GDN_DV] - ws[h][:CHUNK]).astype(BF16) for h in heads]
    o_intra = [_dot((qk[h] * decay[h]).astype(BF16), v16[h]) for h in heads]
    s_add = [_dot_tn(kd16[h], v16[h]) for h in heads]
    for h in heads:
        st_ref[h] = s[h] * elast[h] + s_add[h]
        o = ws[h][CHUNK:] + o_intra[h]
        o = o * lax.rsqrt(jnp.mean(o * o, axis=-1, keepdims=True) + NORM_EPS) * nw
        o_ref[0, :, h * GDN_DV:(h + 1) * GDN_DV] = o.astype(o_ref.dtype)

    xpad_ref[0:CONV_HALO, :] = xpad_ref[CHUNK:CHUNK + CONV_HALO, :]


def _gdn(proj, conv_w, cs, bt, g_rows, norm_w, seq, pad):
    b, lp, _ = proj.shape
    nc = lp // CHUNK
    conv_dim = 3 * GDN_DQK
    qkv = lambda blk: pl.BlockSpec((1, CHUNK, GDN_DQK), lambda i, c: (i, c, blk))
    return pl.pallas_call(
        functools.partial(_gdn_body, pad=pad),
        grid=(b, nc),
        in_specs=[
            qkv(0), qkv(1), qkv(2),
            pl.BlockSpec((CONV_K, conv_dim), lambda i, c: (0, 0)),
            pl.BlockSpec((1, CHUNK, LANES), lambda i, c: (i, c, 0)),
            pl.BlockSpec((1, CHUNK, LANES), lambda i, c: (i, c, 0)),
            pl.BlockSpec((1, 1, GDN_HEADS, CHUNK), lambda i, c: (i, c, 0, 0)),
            pl.BlockSpec((1, GDN_DV), lambda i, c: (0, 0)),
            pl.BlockSpec(((CONV_K - 1) * CHUNK, CONV_HALO + CHUNK), lambda i, c: (0, 0)),
        ],
        out_specs=pl.BlockSpec((1, CHUNK, GDN_DQK), lambda i, c: (i, jnp.maximum(c - 1, 0), 0)),
        out_shape=jax.ShapeDtypeStruct((b, seq, GDN_HEADS * GDN_DV), BF16),
        scratch_shapes=[
            pltpu.VMEM((CONV_HALO + CHUNK, conv_dim), BF16),
            pltpu.VMEM(((CONV_K - 1) * CHUNK, conv_dim), F32),
            pltpu.VMEM((GDN_HEADS, GDN_DK, GDN_DV), F32),
        ],
        compiler_params=_params("arbitrary", "arbitrary"),
        name="gdn_scan",
    )(proj, proj, proj, conv_w, cs, bt, g_rows, norm_w.reshape(1, GDN_DV), _shift_matrix())


def _merge_body(y_ref, z_ref, o_ref, gate_ref, g1_ref, g2_ref, nw_ref, w1_ref, w2_ref, m_ref,
                a1_ref, a2_ref):
    @pl.when(pl.program_id(1) == 0)
    def _():
        gw = SSD_D_INNER // SSD_GROUPS
        for g in range(SSD_GROUPS):
            sl = slice(g * gw, (g + 1) * gw)
            t = y_ref[:, sl].astype(F32) * _silu(z_ref[:, sl].astype(F32))
            ms = jnp.mean(t * t, axis=-1, keepdims=True)
            a1_ref[:, sl] = (t * lax.rsqrt(ms + NORM_EPS) * nw_ref[:, sl]).astype(BF16)
        a2_ref[...] = (o_ref[...].astype(F32) * _silu(gate_ref[...].astype(F32))).astype(BF16)

    m = (_sigmoid(g1_ref[...].astype(F32)) * _dot(a1_ref[...], w1_ref[...])
         + _sigmoid(g2_ref[...].astype(F32)) * _dot(a2_ref[...], w2_ref[...]))
    m_ref[...] = m.astype(m_ref.dtype)


def _merge(y, o, proj_tok, norm_w, w1, w2, tm_pref=512, tn_pref=1024):
    m, d = y.shape
    n = w1.shape[1]
    tm = _divisor_tile(m, tm_pref, 16)
    tn = _divisor_tile(n, tn_pref, LANES)
    nj = n // tn
    row_d = lambda blk: pl.BlockSpec((tm, d), lambda i, j: (i, blk))
    return pl.pallas_call(
        _merge_body,
        grid=(m // tm, nj),
        in_specs=[
            pl.BlockSpec((tm, d), lambda i, j: (i, 0)),
            row_d(0),
            pl.BlockSpec((tm, d), lambda i, j: (i, 0)),
            row_d(1),
            pl.BlockSpec((tm, tn), lambda i, j: (i, 2 * nj + j)),
            pl.BlockSpec((tm, tn), lambda i, j: (i, 3 * nj + j)),
            pl.BlockSpec((1, d), lambda i, j: (0, 0)),
            pl.BlockSpec((d, tn), lambda i, j: (0, j)),
            pl.BlockSpec((d, tn), lambda i, j: (0, j)),
        ],
        out_specs=pl.BlockSpec((tm, tn), lambda i, j: (i, j)),
        out_shape=jax.ShapeDtypeStruct((m, n), BF16),
        scratch_shapes=[pltpu.VMEM((tm, d), BF16), pltpu.VMEM((tm, d), BF16)],
        compiler_params=_params("arbitrary", "arbitrary"),
        name="merge",
    )(y, proj_tok, o, proj_tok, proj_tok, proj_tok, norm_w.reshape(1, d), w1, w2)


def _matmul_residual_body(a_ref, w_ref, r_ref, o_ref):
    o_ref[...] = r_ref[...] + _dot(a_ref[...], w_ref[...])


def _matmul_residual(a, w, res, tm_pref=512, tn_pref=1024):
    m, k = a.shape
    n = w.shape[1]
    tm = _divisor_tile(m, tm_pref, 8)
    tn = _divisor_tile(n, tn_pref, LANES)
    return pl.pallas_call(
        _matmul_residual_body,
        grid=(m // tm, n // tn),
        in_specs=[
            pl.BlockSpec((tm, k), lambda i, j: (i, 0)),
            pl.BlockSpec((k, tn), lambda i, j: (0, j)),
            pl.BlockSpec((tm, tn), lambda i, j: (i, j)),
        ],
        out_specs=pl.BlockSpec((tm, tn), lambda i, j: (i, j)),
        out_shape=jax.ShapeDtypeStruct((m, n), F32),
        compiler_params=_params("arbitrary", "arbitrary"),
        name="out_proj_residual",
    )(a, w, res)


def _top_values(x, count):
    out = []
    for _ in range(count):
        m = jnp.max(x, axis=0, keepdims=True)
        out.append(m)
        x = jnp.where(x == m, -jnp.inf, x)
    return out


def _peer_route_body(q_ref, keys_ref, e2_ref, thr_ref, f1_ref):
    k1 = keys_ref[0].astype(BF16)
    k2 = keys_ref[1].astype(BF16)
    kk = PEER_TOPK
    for h in range(PEER_HEADS):
        lo = 2 * h * PEER_D_KEY
        q1 = q_ref[:, lo:lo + PEER_D_KEY]
        q2 = q_ref[:, lo + PEER_D_KEY:lo + 2 * PEER_D_KEY]
        s1 = _dot_nt(k1, q1)
        s2 = _dot_nt(k2, q2)
        t1 = _top_values(s1, kk + 1)
        t2 = _top_values(s2, kk + 1)
        t1_blk = jnp.concatenate(t1[:kk], axis=0)
        t2_blk = jnp.concatenate(t2[:kk], axis=0)
        half = kk // 2
        cand = jnp.concatenate(
            [t1[0] + t2_blk]
            + [t1[a] + t2_blk[:half] for a in range(1, half)]
            + [t1_blk[half:] + t2[0]], axis=0)
        best = _top_values(cand, kk + 1)
        runner_up = jnp.maximum(best[kk], jnp.maximum(t1[kk] + t2[0], t1[0] + t2[kk]))
        tau = 0.5 * (best[kk - 1] + runner_up)
        m1, m2 = t1[0], t2[0]
        z = jnp.sum(jnp.where(cand >= tau, jnp.exp(cand - (m1 + m2)), 0.0), axis=0, keepdims=True)
        e2 = jnp.exp(s2 - m2)
        for s in range(e2_ref.shape[1]):
            e2_ref[h, s] = e2[:, s * LANES:(s + 1) * LANES]
        thr_ref[h] = jnp.exp(tau - s1 - m2)
        f1_ref[h] = jnp.exp(s1 - m1) / z


def _peer_route(q, keys, t_tok):
    ntok = q.shape[0]
    spec = pl.BlockSpec((PEER_HEADS, PEER_N_KEYS, t_tok), lambda t: (0, 0, t))
    shp = jax.ShapeDtypeStruct((PEER_HEADS, PEER_N_KEYS, ntok), F32)
    strip_spec = pl.BlockSpec((PEER_HEADS, t_tok // LANES, PEER_N_KEYS, LANES), lambda t: (0, t, 0, 0))
    strip_shp = jax.ShapeDtypeStruct((PEER_HEADS, ntok // LANES, PEER_N_KEYS, LANES), F32)
    return pl.pallas_call(
        _peer_route_body,
        grid=(ntok // t_tok,),
        in_specs=[
            pl.BlockSpec((t_tok, q.shape[1]), lambda t: (t, 0)),
            pl.BlockSpec(keys.shape, lambda t: (0, 0, 0)),
        ],
        out_specs=(strip_spec, spec, spec),
        out_shape=(strip_shp, shp, shp),
        compiler_params=_params("arbitrary"),
        name="peer_route",
    )(q, keys)


def _peer_expert_body(hnt_ref, e2_ref, thr_ref, f1_ref, down_ref, upt_ref, acc_ref,
                      w_ref, act_ref, p_ref):
    d, t_tok = hnt_ref.shape
    te = down_ref.shape[0]
    n_part = t_tok // LANES
    rows_a = te // n_part
    rows_o = d // n_part
    e = pl.program_id(1)
    last = pl.num_programs(1) - 1
    slot = lax.rem(e, 2)

    def run(build, project):
        for s in range(n_part):
            if build:
                ln = slice(s * LANES, (s + 1) * LANES)
                for i in range(te // PEER_N_KEYS):
                    w = None
                    for hd in range(PEER_HEADS):
                        thr = thr_ref[hd, i:i + 1, ln]
                        f1 = f1_ref[hd, i:i + 1, ln]
                        e2 = e2_ref[hd, s]
                        term = f1 * jnp.where(e2 >= thr, e2, 0.0)
                        w = term if w is None else w + term
                    w_ref[i * PEER_N_KEYS:(i + 1) * PEER_N_KEYS, ln] = w
                ra = slice(s * rows_a, (s + 1) * rows_a)
                act_ref[ra, :] = _dot(down_ref[ra, :], hnt_ref[...])
            if project:
                ro = slice(s * rows_o, (s + 1) * rows_o)
                acc_ref[ro, :] += _dot(upt_ref[ro, :], p_ref[1 - slot])
        if build:
            a = act_ref[...]
            gelu = 0.5 * a * (1.0 + lax.erf(a * (2.0 ** -0.5)))
            p_ref[slot] = (w_ref[...] * gelu).astype(BF16)

    @pl.when(e == 0)
    def _():
        acc_ref[...] = jnp.zeros_like(acc_ref)
        run(build=True, project=False)

    @pl.when(jnp.logical_and(e > 0, e < last))
    def _():
        run(build=True, project=True)

    @pl.when(e == last)
    def _():
        run(build=False, project=True)


def _peer_experts(hn_t, e2, thr, f1, down, up_t, t_tok, keys_per_block):
    d, ntok = hn_t.shape
    n_exp = down.shape[0]
    te = keys_per_block * PEER_N_KEYS
    n_blk = n_exp // te
    cur = lambda e: jnp.minimum(e, n_blk - 1)
    prev = lambda e: jnp.maximum(e - 1, 0)
    full = pl.BlockSpec((PEER_HEADS, t_tok // LANES, PEER_N_KEYS, LANES), lambda t, e: (0, t, 0, 0))
    part = pl.BlockSpec((PEER_HEADS, keys_per_block, t_tok), lambda t, e: (0, cur(e), t))
    return pl.pallas_call(
        _peer_expert_body,
        grid=(ntok // t_tok, n_blk + 1),
        in_specs=[
            pl.BlockSpec((d, t_tok), lambda t, e: (0, t)),
            full, part, part,
            pl.BlockSpec((te, d), lambda t, e: (cur(e), 0)),
            pl.BlockSpec((d, te), lambda t, e: (0, prev(e))),
        ],
        out_specs=pl.BlockSpec((d, t_tok), lambda t, e: (0, t)),
        out_shape=jax.ShapeDtypeStruct((d, ntok), F32),
        scratch_shapes=[
            pltpu.VMEM((te, t_tok), F32),
            pltpu.VMEM((te, t_tok), F32),
            pltpu.VMEM((2, te, t_tok), BF16),
        ],
        compiler_params=_params("arbitrary", "arbitrary"),
        name="peer_experts",
    )(hn_t, e2, thr, f1, down, up_t)


def _residual_norm_body(h_ref, acc_ref, g_ref, o_ref):
    y = h_ref[...] + acc_ref[...].T
    ms = jnp.mean(y * y, axis=-1, keepdims=True)
    o_ref[...] = y * lax.rsqrt(ms + NORM_EPS) * g_ref[...]


def _residual_norm(h2, acc_t, gain, tm_pref=256):
    m, d = h2.shape
    tm = _divisor_tile(m, tm_pref, LANES)
    return pl.pallas_call(
        _residual_norm_body,
        grid=(m // tm,),
        in_specs=[
            pl.BlockSpec((tm, d), lambda i: (i, 0)),
            pl.BlockSpec((d, tm), lambda i: (0, i)),
            pl.BlockSpec((1, d), lambda i: (0, 0)),
        ],
        out_specs=pl.BlockSpec((tm, d), lambda i: (i, 0)),
        out_shape=jax.ShapeDtypeStruct((m, d), F32),
        compiler_params=_params("arbitrary"),
        name="residual_norm",
    )(h2, acc_t, gain.reshape(1, d))


def _cast_body(x_ref, o_ref, *, transpose):
    x = x_ref[...]
    o_ref[...] = (x.T if transpose else x).astype(o_ref.dtype)


def _cast_bf16(w, rows_pref, transpose=False, row_multiple=LANES):
    r, c = w.shape
    tr = _divisor_tile(r, rows_pref, row_multiple)
    out_spec = pl.BlockSpec((c, tr), lambda i: (0, i)) if transpose else pl.BlockSpec((tr, c), lambda i: (i, 0))
    return pl.pallas_call(
        functools.partial(_cast_body, transpose=transpose),
        grid=(r // tr,),
        in_specs=[pl.BlockSpec((tr, c), lambda i: (i, 0))],
        out_specs=out_spec,
        out_shape=jax.ShapeDtypeStruct((c, r) if transpose else (r, c), BF16),
        compiler_params=_params("arbitrary"),
        name="cast_bf16",
    )(w)


def _w_in_segments(d):
    widths = (("z", SSD_D_INNER), ("xbc", SSD_D_INNER + 2 * SSD_BC), ("dt", SSD_HEADS),
              ("qkv", 3 * GDN_DQK), ("gate", GDN_HEADS * GDN_DV), ("a", GDN_HEADS),
              ("beta", GDN_HEADS), ("merge", 2 * d))
    seg, off = {}, 0
    for name, wd in widths:
        seg[name] = slice(off, off + wd)
        off += wd
    return seg


def _row_segments(seg, names):
    return tuple((seg[n].start, seg[n].stop - seg[n].start) for n in names)


def _lane_vec(*parts):
    v = jnp.concatenate([p.astype(F32) for p in parts])
    return jnp.pad(v, (0, LANES - v.shape[0])).reshape(1, LANES)


def _mixer(x, meta_tokens, norm_w, w_in, ssd_conv_w, ssd_conv_b, ssd_dt_bias, ssd_A_log, ssd_D,
           ssd_norm_w, gdn_conv_w, gdn_dt_bias, gdn_A_log, gdn_norm_w, w_branch_ssd, w_branch_gdn, w_out):
    b, seq, d = x.shape
    pad = CHUNK - N_META
    lp = seq + CHUNK
    nc = lp // CHUNK
    seg = _w_in_segments(d)
    w16_t = _cast_bf16(jnp.swapaxes(w_in, 0, 1), rows_pref=512, row_multiple=64)
    small = jnp.concatenate([w16_t[seg[n]] for n in ("dt", "a", "beta")], axis=0)
    w_small = jnp.pad(small, ((0, LANES - small.shape[0]), (0, 0)))

    meta = jnp.broadcast_to(meta_tokens[None].astype(x.dtype), (b, N_META, d))
    hp = jnp.concatenate([jnp.zeros((b, pad, d), x.dtype), meta, x], axis=1).reshape(b * lp, d)
    x2d = x.reshape(b * seq, d)

    proj_scan = _norm_matmul(hp, norm_w, w16_t, tm_pref=1056,
                             w_row_segments=_row_segments(seg, ("qkv", "xbc"))).reshape(b, lp, -1)
    raw_small = _norm_matmul(hp, norm_w, w_small, tm_pref=1056, out_dtype=F32,
                             w_row_segments=((0, LANES),)).reshape(b, lp, LANES)
    proj_tok = _norm_matmul(x2d, norm_w, w16_t, tm_pref=1024,
                            w_row_segments=_row_segments(seg, ("z", "gate", "merge")))

    zeros_h = jnp.zeros((GDN_HEADS,), F32)
    bias_vec = _lane_vec(ssd_dt_bias, gdn_dt_bias, zeros_h)
    alog_vec = _lane_vec(ssd_A_log, gdn_A_log, zeros_h)
    sp, cs, bt = _gates(raw_small, bias_vec, alog_vec, pad)
    cs_c = cs.reshape(b, nc, CHUNK, LANES)
    ssd_rows = jnp.swapaxes(cs_c[..., :SSD_HEADS], 2, 3)
    gdn_rows = jnp.swapaxes(cs_c[..., SSD_HEADS:SSD_HEADS + GDN_HEADS], 2, 3)

    d_vec = jnp.repeat(ssd_D.astype(F32), SSD_HEAD_DIM).reshape(1, SSD_D_INNER)
    y = _ssd(proj_scan, ssd_conv_w, ssd_conv_b, sp, cs, ssd_rows, d_vec, seq, pad)
    o = _gdn(proj_scan, gdn_conv_w, cs, bt, gdn_rows, gdn_norm_w, seq, pad)

    m = _merge(y.reshape(b * seq, -1), o.reshape(b * seq, -1), proj_tok, ssd_norm_w,
               w_branch_ssd.astype(BF16), w_branch_gdn.astype(BF16))
    return _matmul_residual(m, w_out.astype(BF16), x2d)


def _peer(h2, norm_w, w_query, sub_keys, expert_down, expert_up, final_w):
    ntok = h2.shape[0]
    q, hn_t = _norm_matmul(h2, norm_w, w_query.astype(BF16), tm_pref=512, with_transposed=True)
    e2, thr, f1 = _peer_route(q, sub_keys, _divisor_tile(ntok, 256, LANES))
    acc_t = _peer_experts(hn_t, e2, thr, f1, _cast_bf16(expert_down, rows_pref=1024),
                          _cast_bf16(expert_up, rows_pref=512, transpose=True),
                          _divisor_tile(ntok, 512, LANES), keys_per_block=8)
    return _residual_norm(h2, acc_t, final_w)


def kernel(x, meta_tokens, norm_mix_w, w_in, ssd_conv_w, ssd_conv_b, ssd_dt_bias, ssd_A_log, ssd_D, ssd_norm_w, gdn_conv_w, gdn_dt_bias, gdn_A_log, gdn_norm_w, w_branch_ssd, w_branch_gdn, w_out, norm_ffn_w, peer_w_query, peer_sub_keys, peer_expert_down, peer_expert_up, final_norm_w):
    b, seq, d = x.shape
    assert norm_mix_w.shape[0] == 1, "single-layer trunk"
    h2 = _mixer(x, meta_tokens, norm_mix_w[0], w_in[0], ssd_conv_w[0], ssd_conv_b[0], ssd_dt_bias[0],
                ssd_A_log[0], ssd_D[0], ssd_norm_w[0], gdn_conv_w[0], gdn_dt_bias[0], gdn_A_log[0],
                gdn_norm_w[0], w_branch_ssd[0], w_branch_gdn[0], w_out[0])
    out = _peer(h2, norm_ffn_w[0], peer_w_query[0], peer_sub_keys[0], peer_expert_down[0],
                peer_expert_up[0], final_norm_w)
    return out.reshape(b, seq, d)
```

```python
import functools

import jax
import jax.numpy as jnp
from jax import lax
from jax.experimental import pallas as pl
from jax.experimental.pallas import tpu as pltpu

F32 = jnp.float32
BF16 = jnp.bfloat16

N_META = 16
CHUNK = 64
CONV_K = 4
NORM_EPS = 1e-6
L2_EPS = 1e-6

SSD_HEADS = 32
SSD_HEAD_DIM = 64
SSD_GROUPS = 4
SSD_STATE = 128
SSD_HPG = SSD_HEADS // SSD_GROUPS
SSD_D_INNER = SSD_HEADS * SSD_HEAD_DIM
SSD_BC = SSD_GROUPS * SSD_STATE

GDN_HEADS = 16
GDN_DK = 128
GDN_DV = 128
GDN_DQK = GDN_HEADS * GDN_DK

PEER_HEADS = 8
PEER_N_KEYS = 128
PEER_TOPK = 16
PEER_D_KEY = 128
PEER_MATMUL_PARTS = 2

CONV_HALO = 16
LANES = 128
VMEM_LIMIT = 56 * 1024 * 1024

_NT = (((1,), (1,)), ((), ()))
_TN = (((0,), (0,)), ((), ()))


def _dot(a, b):
    return jnp.dot(a, b, preferred_element_type=F32)


def _dot_nt(a, b):
    return lax.dot_general(a, b, _NT, preferred_element_type=F32)


def _dot_tn(a, b):
    return lax.dot_general(a, b, _TN, preferred_element_type=F32)


def _silu(x):
    return x / (1.0 + jnp.exp(-x))


def _sigmoid(x):
    return 1.0 / (1.0 + jnp.exp(-x))


def _softplus(x):
    return jnp.maximum(x, 0.0) + jnp.log(1.0 + jnp.exp(-jnp.abs(x)))


def _divisor_tile(n, preferred, multiple):
    t = min(preferred, n)
    t -= t % multiple
    while t >= multiple:
        if n % t == 0:
            return t
        t -= multiple
    return n


def _params(*sem):
    return pltpu.CompilerParams(dimension_semantics=sem, vmem_limit_bytes=VMEM_LIMIT)


def _norm_matmul_body(x_ref, g_ref, w_ref, o_ref, *rest, w_is_transposed):
    n_ref = rest[-1]
    nt_ref = rest[0] if len(rest) == 2 else None

    @pl.when(pl.program_id(1) == 0)
    def _():
        x = x_ref[...]
        ms = jnp.mean(x * x, axis=-1, keepdims=True)
        n = x * lax.rsqrt(ms + NORM_EPS) * g_ref[...]
        n_ref[...] = n.astype(BF16)
        if nt_ref is not None:
            nt_ref[...] = n.T.astype(BF16)

    dot = _dot_nt if w_is_transposed else _dot
    o_ref[...] = dot(n_ref[...], w_ref[...]).astype(o_ref.dtype)


def _segment_row(j, segments, tn):
    unit = 16
    row, first_blk = None, 0
    for start, length in segments:
        assert start % unit == 0 and tn % unit == 0
        cand = start // unit + (j - first_blk) * (tn // unit)
        row = cand if row is None else jnp.where(j >= first_blk, cand, row)
        first_blk += length // tn
    return row * unit


def _norm_matmul(x2d, gain, w, tm_pref=768, tn_pref=1024, out_dtype=BF16, with_transposed=False,
                 w_row_segments=None):
    m, k = x2d.shape
    w_is_transposed = w_row_segments is not None
    n = sum(ln for _, ln in w_row_segments) if w_is_transposed else w.shape[1]
    tm = _divisor_tile(m, tm_pref, LANES if with_transposed else 16)
    tn = _divisor_tile(n, tn_pref, LANES)
    if w_is_transposed:
        assert all(ln % tn == 0 for _, ln in w_row_segments)
        w_spec = pl.BlockSpec((pl.Element(tn), pl.Element(k)),
                              lambda i, j: (_segment_row(j, w_row_segments, tn), 0))
    else:
        w_spec = pl.BlockSpec((k, tn), lambda i, j: (0, j))
    out_specs = pl.BlockSpec((tm, tn), lambda i, j: (i, j))
    out_shape = jax.ShapeDtypeStruct((m, n), out_dtype)
    if with_transposed:
        out_specs = (out_specs, pl.BlockSpec((k, tm), lambda i, j: (0, i)))
        out_shape = (out_shape, jax.ShapeDtypeStruct((k, m), BF16))
    return pl.pallas_call(
        functools.partial(_norm_matmul_body, w_is_transposed=w_is_transposed),
        grid=(m // tm, n // tn),
        in_specs=[
            pl.BlockSpec((tm, k), lambda i, j: (i, 0)),
            pl.BlockSpec((1, k), lambda i, j: (0, 0)),
            w_spec,
        ],
        out_specs=out_specs,
        out_shape=out_shape,
        scratch_shapes=[pltpu.VMEM((tm, k), BF16)],
        compiler_params=_params("arbitrary", "arbitrary"),
        name="norm_matmul",
    )(x2d, gain.reshape(1, k), w)


def _gates_body(raw_ref, bias_ref, alog_ref, sp_ref, cs_ref, bt_ref, *, n_chunks, pad):
    ri = lax.broadcasted_iota(jnp.int32, (CHUNK, CHUNK), 0)
    ci = lax.broadcasted_iota(jnp.int32, (CHUNK, CHUNK), 1)
    tri = (ri >= ci).astype(F32)
    neg_a = -jnp.exp(alog_ref[...])
    bias = bias_ref[...]
    row = lax.broadcasted_iota(jnp.int32, (CHUNK, 1), 0)

    def body(c, carry):
        sl = pl.ds(pl.multiple_of(c * CHUNK, CHUNK), CHUNK)
        r = raw_ref[0, sl, :]
        sp = _softplus(r + bias)
        cs = jnp.dot(tri, sp * neg_a, precision=lax.Precision.HIGHEST, preferred_element_type=F32)
        valid = (row + c * CHUNK) >= pad
        sp_ref[0, sl, :] = sp
        cs_ref[0, sl, :] = cs
        bt_ref[0, sl, :] = jnp.where(valid, _sigmoid(r), 0.0)
        return carry

    lax.fori_loop(0, n_chunks, body, 0)


def _gates(raw, bias_vec, alog_vec, pad):
    b, lp, w = raw.shape
    spec = pl.BlockSpec((1, lp, w), lambda i: (i, 0, 0))
    vec = pl.BlockSpec((1, w), lambda i: (0, 0))
    shp = jax.ShapeDtypeStruct((b, lp, w), F32)
    return pl.pallas_call(
        functools.partial(_gates_body, n_chunks=lp // CHUNK, pad=pad),
        grid=(b,),
        in_specs=[spec, vec, vec],
        out_specs=(spec, spec, spec),
        out_shape=(shp, shp, shp),
        compiler_params=_params("arbitrary"),
        name="gates",
    )(raw, bias_vec, alog_vec)


def _shift_matrix():
    r = jnp.arange((CONV_K - 1) * CHUNK)
    src = CONV_HALO + r % CHUNK - (CONV_K - 1 - r // CHUNK)
    return (jnp.arange(CONV_HALO + CHUNK)[None, :] == src[:, None]).astype(BF16)


def _shifted_taps(shift_ref, xpad_ref, taps_ref):
    taps_ref[...] = _dot(shift_ref[...], xpad_ref[...])


def _causal_conv(taps_ref, xpad_ref, w_ref, lo, width, bias):
    cols = slice(lo, lo + width)
    acc = w_ref[CONV_K - 1:CONV_K, cols] * xpad_ref[CONV_HALO:CONV_HALO + CHUNK, cols].astype(F32)
    if bias is not None:
        acc = acc + bias
    for j in range(CONV_K - 1):
        acc = acc + w_ref[j:j + 1, cols] * taps_ref[j * CHUNK:(j + 1) * CHUNK, cols]
    return acc


def _ssd_body(x_ref, bc_ref, cw_ref, cb_ref, sp_ref, cs_ref, csr_ref, dvec_ref, expand_ref, shift_ref,
              y_ref, xpad_ref, taps_ref, st_ref, *, pad):
    c = pl.program_id(1)

    @pl.when(c == 0)
    def _():
        st_ref[...] = jnp.zeros_like(st_ref)
        xpad_ref[0:CONV_HALO, :] = jnp.zeros((CONV_HALO, xpad_ref.shape[1]), BF16)

    xpad_ref[CONV_HALO:CONV_HALO + CHUNK, 0:SSD_D_INNER] = x_ref[0]
    xpad_ref[CONV_HALO:CONV_HALO + CHUNK, SSD_D_INNER:] = bc_ref[0]
    _shifted_taps(shift_ref, xpad_ref, taps_ref)

    row = lax.broadcasted_iota(jnp.int32, (CHUNK, 1), 0)
    valid = (row + c * CHUNK) >= pad
    ri = lax.broadcasted_iota(jnp.int32, (CHUNK, CHUNK), 0)
    ci = lax.broadcasted_iota(jnp.int32, (CHUNK, CHUNK), 1)
    tril = ri >= ci

    sp = sp_ref[0]
    cs = cs_ref[0]
    csr = csr_ref[0, 0]

    def conv(lo, width):
        a = _causal_conv(taps_ref, xpad_ref, cw_ref, lo, width, cb_ref[:, lo:lo + width])
        return jnp.where(valid, _silu(a), 0.0)

    gw = SSD_HPG * SSD_HEAD_DIM
    groups = range(SSD_GROUPS)
    heads = range(SSD_HEADS)
    xg = [conv(g * gw, gw) for g in groups]
    bg = [conv(SSD_D_INNER + g * SSD_STATE, SSD_STATE).astype(BF16) for g in groups]
    cg = [conv(SSD_D_INNER + SSD_BC + g * SSD_STATE, SSD_STATE).astype(BF16) for g in groups]
    cb = [_dot_nt(cg[g], bg[g]) for g in groups]
    st = [st_ref[g] for g in groups]
    yoff = [_dot(cg[g], st[g].astype(BF16)) for g in groups]

    last_row = cs[CHUNK - 1:CHUNK, :]
    factors = (sp, sp * jnp.exp(last_row - cs), jnp.exp(cs))
    pieces = []
    for f in factors:
        hi = f.astype(BF16)
        pieces += [hi, (f - hi.astype(F32)).astype(BF16)]
    spread = _dot(jnp.concatenate(pieces, axis=0), expand_ref[...])
    e_dt, e_w, e_cs = (spread[(2 * k) * CHUNK:(2 * k + 1) * CHUNK] + spread[(2 * k + 1) * CHUNK:(2 * k + 2) * CHUNK]
                       for k in range(len(factors)))

    xdt = [(xg[g] * e_dt[:, g * gw:(g + 1) * gw]).astype(BF16) for g in groups]
    xw = [(xg[g] * e_w[:, g * gw:(g + 1) * gw]).astype(BF16) for g in groups]
    w_intra = []
    for h in heads:
        decay = jnp.where(tril, jnp.exp(cs[:, h:h + 1] - csr[h:h + 1, :]), 0.0)
        w_intra.append((cb[h // SSD_HPG] * decay).astype(BF16))
    ydiag = [_dot(w_intra[h], xdt[h // SSD_HPG][:, (h % SSD_HPG) * SSD_HEAD_DIM:(h % SSD_HPG + 1) * SSD_HEAD_DIM])
             for h in heads]
    snew = [_dot_tn(bg[g], xw[g]) for g in groups]
    for g in groups:
        sl = slice(g * gw, (g + 1) * gw)
        st_ref[g] = st[g] * e_cs[CHUNK - 1:CHUNK, sl] + snew[g]
        y_g = (jnp.concatenate(ydiag[g * SSD_HPG:(g + 1) * SSD_HPG], axis=1) + yoff[g] * e_cs[:, sl]
               + dvec_ref[:, sl] * xg[g])
        y_ref[0, :, sl] = y_g.astype(y_ref.dtype)

    xpad_ref[0:CONV_HALO, :] = xpad_ref[CHUNK:CHUNK + CONV_HALO, :]


def _ssd(proj, conv_w, conv_b, sp, cs, cs_rows, d_vec, seq, pad):
    b, lp, _ = proj.shape
    nc = lp // CHUNK
    conv_dim = SSD_D_INNER + 2 * SSD_BC
    x_blk = 3 * GDN_DQK // SSD_D_INNER
    bc_blk = (3 * GDN_DQK + SSD_D_INNER) // (2 * SSD_BC)
    expand = (jnp.arange(LANES)[:, None] == jnp.arange(SSD_D_INNER)[None, :] // SSD_HEAD_DIM).astype(BF16)
    return pl.pallas_call(
        functools.partial(_ssd_body, pad=pad),
        grid=(b, nc),
        in_specs=[
            pl.BlockSpec((1, CHUNK, SSD_D_INNER), lambda i, c: (i, c, x_blk)),
            pl.BlockSpec((1, CHUNK, 2 * SSD_BC), lambda i, c: (i, c, bc_blk)),
            pl.BlockSpec((CONV_K, conv_dim), lambda i, c: (0, 0)),
            pl.BlockSpec((1, conv_dim), lambda i, c: (0, 0)),
            pl.BlockSpec((1, CHUNK, LANES), lambda i, c: (i, c, 0)),
            pl.BlockSpec((1, CHUNK, LANES), lambda i, c: (i, c, 0)),
            pl.BlockSpec((1, 1, SSD_HEADS, CHUNK), lambda i, c: (i, c, 0, 0)),
            pl.BlockSpec((1, SSD_D_INNER), lambda i, c: (0, 0)),
            pl.BlockSpec((LANES, SSD_D_INNER), lambda i, c: (0, 0)),
            pl.BlockSpec(((CONV_K - 1) * CHUNK, CONV_HALO + CHUNK), lambda i, c: (0, 0)),
        ],
        out_specs=pl.BlockSpec((1, CHUNK, SSD_D_INNER), lambda i, c: (i, jnp.maximum(c - 1, 0), 0)),
        out_shape=jax.ShapeDtypeStruct((b, seq, SSD_D_INNER), BF16),
        scratch_shapes=[
            pltpu.VMEM((CONV_HALO + CHUNK, conv_dim), BF16),
            pltpu.VMEM(((CONV_K - 1) * CHUNK, conv_dim), F32),
            pltpu.VMEM((SSD_GROUPS, SSD_STATE, SSD_HPG * SSD_HEAD_DIM), F32),
        ],
        compiler_params=_params("arbitrary", "arbitrary"),
        name="ssd_scan",
    )(proj, proj, conv_w, conv_b.reshape(1, conv_dim), sp, cs, cs_rows, d_vec, expand, _shift_matrix())


def _unit_lower_inverses(strict_ls, eye):
    ps = [l.astype(BF16) for l in strict_ls]
    ts = [eye - l for l in strict_ls]
    n = 2
    while n < CHUNK:
        p2 = [_dot(p, p).astype(BF16) for p in ps]
        ts = [t + _dot(t.astype(BF16), q) for t, q in zip(ts, p2)]
        ps = p2
        n *= 2
    return ts


def _gdn_body(q_ref, k_ref, v_ref, cw_ref, cs_ref, bt_ref, gr_ref, nw_ref, shift_ref, o_ref,
              xpad_ref, taps_ref, st_ref, *, pad):
    c = pl.program_id(1)

    @pl.when(c == 0)
    def _():
        st_ref[...] = jnp.zeros_like(st_ref)
        xpad_ref[0:CONV_HALO, :] = jnp.zeros((CONV_HALO, xpad_ref.shape[1]), BF16)

    xpad_ref[CONV_HALO:CONV_HALO + CHUNK, 0:GDN_DQK] = q_ref[0]
    xpad_ref[CONV_HALO:CONV_HALO + CHUNK, GDN_DQK:2 * GDN_DQK] = k_ref[0]
    xpad_ref[CONV_HALO:CONV_HALO + CHUNK, 2 * GDN_DQK:] = v_ref[0]
    _shifted_taps(shift_ref, xpad_ref, taps_ref)

    row = lax.broadcasted_iota(jnp.int32, (CHUNK, 1), 0)
    valid = (row + c * CHUNK) >= pad
    ri = lax.broadcasted_iota(jnp.int32, (CHUNK, CHUNK), 0)
    ci = lax.broadcasted_iota(jnp.int32, (CHUNK, CHUNK), 1)
    tril = ri >= ci
    strict = ri > ci
    eye = (ri == ci).astype(F32)

    cs = cs_ref[0]
    bt = bt_ref[0]
    gr = gr_ref[0, 0]
    nw = nw_ref[...]
    g_off = SSD_HEADS
    b_off = SSD_HEADS + GDN_HEADS

    def conv(lo):
        a = _causal_conv(taps_ref, xpad_ref, cw_ref, lo, GDN_DK, None)
        return jnp.where(valid, _silu(a), 0.0)

    heads = range(GDN_HEADS)
    q16, k16, kb16, rhs16, qg, kd16, decay, elast = [], [], [], [], [], [], [], []
    for h in heads:
        q = conv(h * GDN_DK)
        k = conv(GDN_DQK + h * GDN_DK)
        v = conv(2 * GDN_DQK + h * GDN_DV)
        q = q * lax.rsqrt(jnp.sum(q * q, axis=-1, keepdims=True) + L2_EPS) * (GDN_DK ** -0.5)
        k = k * lax.rsqrt(jnp.sum(k * k, axis=-1, keepdims=True) + L2_EPS)
        col = cs[:, g_off + h:g_off + h + 1]
        rowv = gr[h:h + 1, :]
        last = rowv[:, CHUNK - 1:CHUNK]
        beta = bt[:, b_off + h:b_off + h + 1]
        ecol = jnp.exp(col)
        kb = k * beta
        decay.append(jnp.where(tril, jnp.exp(col - rowv), 0.0))
        q16.append(q.astype(BF16))
        k16.append(k.astype(BF16))
        kb16.append(kb.astype(BF16))
        rhs16.append(jnp.concatenate([v * beta, kb * ecol], axis=1).astype(BF16))
        qg.append(q * ecol)
        kd16.append((k * jnp.exp(last - col)).astype(BF16))
        elast.append(jnp.exp(last))
    kk = [_dot_nt(kb16[h], k16[h]) for h in heads]
    qk = [_dot_nt(q16[h], k16[h]) for h in heads]
    tinv = _unit_lower_inverses([jnp.where(strict, kk[h] * decay[h], 0.0) for h in heads], eye)
    uw = [_dot(tinv[h].astype(BF16), rhs16[h]) for h in heads]
    s = [st_ref[h] for h in heads]
    ws = [_dot(jnp.concatenate([uw[h][:, GDN_DV:], qg[h]], axis=0).astype(BF16), s[h].astype(BF16))
          for h in heads]
    v16 = [(uw[h][:, :GDN_DV] - ws[h][:CHUNK]).astype(BF16) for h in heads]
    o_intra = [_dot((qk[h] * decay[h]).astype(BF16), v16[h]) for h in heads]
    s_add = [_dot_tn(kd16[h], v16[h]) for h in heads]
    for h in heads:
        st_ref[h] = s[h] * elast[h] + s_add[h]
        o = ws[h][CHUNK:] + o_intra[h]
        o = o * lax.rsqrt(jnp.mean(o * o, axis=-1, keepdims=True) + NORM_EPS) * nw
        o_ref[0, :, h * GDN_DV:(h + 1) * GDN_DV] = o.astype(o_ref.dtype)

    xpad_ref[0:CONV_HALO, :] = xpad_ref[CHUNK:CHUNK + CONV_HALO, :]


def _gdn(proj, conv_w, cs, bt, g_rows, norm_w, seq, pad):
    b, lp, _ = proj.shape
    nc = lp // CHUNK
    conv_dim = 3 * GDN_DQK
    qkv = lambda blk: pl.BlockSpec((1, CHUNK, GDN_DQK), lambda i, c: (i, c, blk))
    return pl.pallas_call(
        functools.partial(_gdn_body, pad=pad),
        grid=(b, nc),
        in_specs=[
            qkv(0), qkv(1), qkv(2),
            pl.BlockSpec((CONV_K, conv_dim), lambda i, c: (0, 0)),
            pl.BlockSpec((1, CHUNK, LANES), lambda i, c: (i, c, 0)),
            pl.BlockSpec((1, CHUNK, LANES), lambda i, c: (i, c, 0)),
            pl.BlockSpec((1, 1, GDN_HEADS, CHUNK), lambda i, c: (i, c, 0, 0)),
            pl.BlockSpec((1, GDN_DV), lambda i, c: (0, 0)),
            pl.BlockSpec(((CONV_K - 1) * CHUNK, CONV_HALO + CHUNK), lambda i, c: (0, 0)),
        ],
        out_specs=pl.BlockSpec((1, CHUNK, GDN_DQK), lambda i, c: (i, jnp.maximum(c - 1, 0), 0)),
        out_shape=jax.ShapeDtypeStruct((b, seq, GDN_HEADS * GDN_DV), BF16),
        scratch_shapes=[
            pltpu.VMEM((CONV_HALO + CHUNK, conv_dim), BF16),
            pltpu.VMEM(((CONV_K - 1) * CHUNK, conv_dim), F32),
            pltpu.VMEM((GDN_HEADS, GDN_DK, GDN_DV), F32),
        ],
        compiler_params=_params("arbitrary", "arbitrary"),
        name="gdn_scan",
    )(proj, proj, proj, conv_w, cs, bt, g_rows, norm_w.reshape(1, GDN_DV), _shift_matrix())


def _merge_body(y_ref, z_ref, o_ref, gate_ref, g1_ref, g2_ref, nw_ref, w1_ref, w2_ref, m_ref,
                a1_ref, a2_ref):
    @pl.when(pl.program_id(1) == 0)
    def _():
        gw = SSD_D_INNER // SSD_GROUPS
        for g in range(SSD_GROUPS):
            sl = slice(g * gw, (g + 1) * gw)
            t = y_ref[:, sl].astype(F32) * _silu(z_ref[:, sl].astype(F32))
            ms = jnp.mean(t * t, axis=-1, keepdims=True)
            a1_ref[:, sl] = (t * lax.rsqrt(ms + NORM_EPS) * nw_ref[:, sl]).astype(BF16)
        a2_ref[...] = (o_ref[...].astype(F32) * _silu(gate_ref[...].astype(F32))).astype(BF16)

    m = (_sigmoid(g1_ref[...].astype(F32)) * _dot(a1_ref[...], w1_ref[...])
         + _sigmoid(g2_ref[...].astype(F32)) * _dot(a2_ref[...], w2_ref[...]))
    m_ref[...] = m.astype(m_ref.dtype)


def _merge(y, o, proj_tok, norm_w, w1, w2, tm_pref=512, tn_pref=1024):
    m, d = y.shape
    n = w1.shape[1]
    tm = _divisor_tile(m, tm_pref, 16)
    tn = _divisor_tile(n, tn_pref, LANES)
    nj = n // tn
    row_d = lambda blk: pl.BlockSpec((tm, d), lambda i, j: (i, blk))
    return pl.pallas_call(
        _merge_body,
        grid=(m // tm, nj),
        in_specs=[
            pl.BlockSpec((tm, d), lambda i, j: (i, 0)),
            row_d(0),
            pl.BlockSpec((tm, d), lambda i, j: (i, 0)),
            row_d(1),
            pl.BlockSpec((tm, tn), lambda i, j: (i, 2 * nj + j)),
            pl.BlockSpec((tm, tn), lambda i, j: (i, 3 * nj + j)),
            pl.BlockSpec((1, d), lambda i, j: (0, 0)),
            pl.BlockSpec((d, tn), lambda i, j: (0, j)),
            pl.BlockSpec((d, tn), lambda i, j: (0, j)),
        ],
        out_specs=pl.BlockSpec((tm, tn), lambda i, j: (i, j)),
        out_shape=jax.ShapeDtypeStruct((m, n), BF16),
        scratch_shapes=[pltpu.VMEM((tm, d), BF16), pltpu.VMEM((tm, d), BF16)],
        compiler_params=_params("arbitrary", "arbitrary"),
        name="merge",
    )(y, proj_tok, o, proj_tok, proj_tok, proj_tok, norm_w.reshape(1, d), w1, w2)


def _matmul_residual_body(a_ref, w_ref, r_ref, o_ref):
    o_ref[...] = r_ref[...] + _dot(a_ref[...], w_ref[...])


def _matmul_residual(a, w, res, tm_pref=512, tn_pref=1024):
    m, k = a.shape
    n = w.shape[1]
    tm = _divisor_tile(m, tm_pref, 8)
    tn = _divisor_tile(n, tn_pref, LANES)
    return pl.pallas_call(
        _matmul_residual_body,
        grid=(m // tm, n // tn),
        in_specs=[
            pl.BlockSpec((tm, k), lambda i, j: (i, 0)),
            pl.BlockSpec((k, tn), lambda i, j: (0, j)),
            pl.BlockSpec((tm, tn), lambda i, j: (i, j)),
        ],
        out_specs=pl.BlockSpec((tm, tn), lambda i, j: (i, j)),
        out_shape=jax.ShapeDtypeStruct((m, n), F32),
        compiler_params=_params("arbitrary", "arbitrary"),
        name="out_proj_residual",
    )(a, w, res)


def _top_values(x, count):
    out = []
    for _ in range(count):
        m = jnp.max(x, axis=0, keepdims=True)
        out.append(m)
        x = jnp.where(x == m, -jnp.inf, x)
    return out


def _peer_route_body(q_ref, keys_ref, e2_ref, thr_ref, f1_ref):
    k1 = keys_ref[0].astype(BF16)
    k2 = keys_ref[1].astype(BF16)
    kk = PEER_TOPK
    for h in range(PEER_HEADS):
        lo = 2 * h * PEER_D_KEY
        q1 = q_ref[:, lo:lo + PEER_D_KEY]
        q2 = q_ref[:, lo + PEER_D_KEY:lo + 2 * PEER_D_KEY]
        s1 = _dot_nt(k1, q1)
        s2 = _dot_nt(k2, q2)
        t1 = _top_values(s1, kk + 1)
        t2 = _top_values(s2, kk + 1)
        t1_blk = jnp.concatenate(t1[:kk], axis=0)
        t2_blk = jnp.concatenate(t2[:kk], axis=0)
        half = kk // 2
        cand = jnp.concatenate(
            [t1[0] + t2_blk]
            + [t1[a] + t2_blk[:half] for a in range(1, half)]
            + [t1_blk[half:] + t2[0]], axis=0)
        best = _top_values(cand, kk + 1)
        runner_up = jnp.maximum(best[kk], jnp.maximum(t1[kk] + t2[0], t1[0] + t2[kk]))
        tau = 0.5 * (best[kk - 1] + runner_up)
        m1, m2 = t1[0], t2[0]
        z = jnp.sum(jnp.where(cand >= tau, jnp.exp(cand - (m1 + m2)), 0.0), axis=0, keepdims=True)
        e2 = jnp.exp(s2 - m2)
        for s in range(e2_ref.shape[1]):
            e2_ref[h, s] = e2[:, s * LANES:(s + 1) * LANES]
        thr_ref[h] = jnp.exp(tau - s1 - m2)
        f1_ref[h] = jnp.exp(s1 - m1) / z


def _peer_route(q, keys, t_tok):
    ntok = q.shape[0]
    spec = pl.BlockSpec((PEER_HEADS, PEER_N_KEYS, t_tok), lambda t: (0, 0, t))
    shp = jax.ShapeDtypeStruct((PEER_HEADS, PEER_N_KEYS, ntok), F32)
    strip_spec = pl.BlockSpec((PEER_HEADS, t_tok // LANES, PEER_N_KEYS, LANES), lambda t: (0, t, 0, 0))
    strip_shp = jax.ShapeDtypeStruct((PEER_HEADS, ntok // LANES, PEER_N_KEYS, LANES), F32)
    return pl.pallas_call(
        _peer_route_body,
        grid=(ntok // t_tok,),
        in_specs=[
            pl.BlockSpec((t_tok, q.shape[1]), lambda t: (t, 0)),
            pl.BlockSpec(keys.shape, lambda t: (0, 0, 0)),
        ],
        out_specs=(strip_spec, spec, spec),
        out_shape=(strip_shp, shp, shp),
        compiler_params=_params("arbitrary"),
        name="peer_route",
    )(q, keys)


def _peer_expert_body(hnt_ref, e2_ref, thr_ref, f1_ref, down_ref, upt_ref, acc_ref,
                      w_ref, act_ref, p_ref):
    d, t_tok = hnt_ref.shape
    te = down_ref.shape[0]
    n_strip = t_tok // LANES
    n_part = min(PEER_MATMUL_PARTS, n_strip)
    rows_a = te // n_part
    rows_o = d // n_part
    e = pl.program_id(1)
    last = pl.num_programs(1) - 1
    slot = lax.rem(e, 2)

    def run(build, project):
        for part in range(n_part):
            if build:
                for s in range(part * n_strip // n_part, (part + 1) * n_strip // n_part):
                    ln = slice(s * LANES, (s + 1) * LANES)
                    for i in range(te // PEER_N_KEYS):
                        w = None
                        for hd in range(PEER_HEADS):
                            thr = thr_ref[hd, i:i + 1, ln]
                            f1 = f1_ref[hd, i:i + 1, ln]
                            e2 = e2_ref[hd, s]
                            term = f1 * jnp.where(e2 >= thr, e2, 0.0)
                            w = term if w is None else w + term
                        w_ref[i * PEER_N_KEYS:(i + 1) * PEER_N_KEYS, ln] = w
                ra = slice(part * rows_a, (part + 1) * rows_a)
                a = _dot(down_ref[ra, :], hnt_ref[...])
                act_ref[ra, :] = 0.5 * a * (1.0 + lax.erf(a * (2.0 ** -0.5)))
            if project:
                ro = slice(part * rows_o, (part + 1) * rows_o)
                acc_ref[ro, :] += _dot(upt_ref[ro, :], p_ref[1 - slot])
        if build:
            p_ref[slot] = (w_ref[...] * act_ref[...]).astype(BF16)

    @pl.when(e == 0)
    def _():
        acc_ref[...] = jnp.zeros_like(acc_ref)
        run(build=True, project=False)

    @pl.when(jnp.logical_and(e > 0, e < last))
    def _():
        run(build=True, project=True)

    @pl.when(e == last)
    def _():
        run(build=False, project=True)


def _peer_experts(hn_t, e2, thr, f1, down, up_t, t_tok, keys_per_block):
    d, ntok = hn_t.shape
    n_exp = down.shape[0]
    te = keys_per_block * PEER_N_KEYS
    n_blk = n_exp // te
    cur = lambda e: jnp.minimum(e, n_blk - 1)
    prev = lambda e: jnp.maximum(e - 1, 0)
    full = pl.BlockSpec((PEER_HEADS, t_tok // LANES, PEER_N_KEYS, LANES), lambda t, e: (0, t, 0, 0))
    part = pl.BlockSpec((PEER_HEADS, keys_per_block, t_tok), lambda t, e: (0, cur(e), t))
    return pl.pallas_call(
        _peer_expert_body,
        grid=(ntok // t_tok, n_blk + 1),
        in_specs=[
            pl.BlockSpec((d, t_tok), lambda t, e: (0, t)),
            full, part, part,
            pl.BlockSpec((te, d), lambda t, e: (cur(e), 0)),
            pl.BlockSpec((d, te), lambda t, e: (0, prev(e))),
        ],
        out_specs=pl.BlockSpec((d, t_tok), lambda t, e: (0, t)),
        out_shape=jax.ShapeDtypeStruct((d, ntok), F32),
        scratch_shapes=[
            pltpu.VMEM((te, t_tok), F32),
            pltpu.VMEM((te, t_tok), F32),
            pltpu.VMEM((2, te, t_tok), BF16),
        ],
        compiler_params=_params("arbitrary", "arbitrary"),
        name="peer_experts",
    )(hn_t, e2, thr, f1, down, up_t)


def _residual_norm_body(h_ref, acc_ref, g_ref, o_ref):
    y = h_ref[...] + acc_ref[...].T
    ms = jnp.mean(y * y, axis=-1, keepdims=True)
    o_ref[...] = y * lax.rsqrt(ms + NORM_EPS) * g_ref[...]


def _residual_norm(h2, acc_t, gain, tm_pref=256):
    m, d = h2.shape
    tm = _divisor_tile(m, tm_pref, LANES)
    return pl.pallas_call(
        _residual_norm_body,
        grid=(m // tm,),
        in_specs=[
            pl.BlockSpec((tm, d), lambda i: (i, 0)),
            pl.BlockSpec((d, tm), lambda i: (0, i)),
            pl.BlockSpec((1, d), lambda i: (0, 0)),
        ],
        out_specs=pl.BlockSpec((tm, d), lambda i: (i, 0)),
        out_shape=jax.ShapeDtypeStruct((m, d), F32),
        compiler_params=_params("arbitrary"),
        name="residual_norm",
    )(h2, acc_t, gain.reshape(1, d))


def _cast_body(x_ref, o_ref, *, transpose):
    x = x_ref[...]
    o_ref[...] = (x.T if transpose else x).astype(o_ref.dtype)


def _cast_bf16(w, rows_pref, transpose=False, row_multiple=LANES):
    r, c = w.shape
    tr = _divisor_tile(r, rows_pref, row_multiple)
    out_spec = pl.BlockSpec((c, tr), lambda i: (0, i)) if transpose else pl.BlockSpec((tr, c), lambda i: (i, 0))
    return pl.pallas_call(
        functools.partial(_cast_body, transpose=transpose),
        grid=(r // tr,),
        in_specs=[pl.BlockSpec((tr, c), lambda i: (i, 0))],
        out_specs=out_spec,
        out_shape=jax.ShapeDtypeStruct((c, r) if transpose else (r, c), BF16),
        compiler_params=_params("arbitrary"),
        name="cast_bf16",
    )(w)


def _w_in_segments(d):
    widths = (("z", SSD_D_INNER), ("xbc", SSD_D_INNER + 2 * SSD_BC), ("dt", SSD_HEADS),
              ("qkv", 3 * GDN_DQK), ("gate", GDN_HEADS * GDN_DV), ("a", GDN_HEADS),
              ("beta", GDN_HEADS), ("merge", 2 * d))
    seg, off = {}, 0
    for name, wd in widths:
        seg[name] = slice(off, off + wd)
        off += wd
    return seg


def _row_segments(seg, names):
    return tuple((seg[n].start, seg[n].stop - seg[n].start) for n in names)


def _lane_vec(*parts):
    v = jnp.concatenate([p.astype(F32) for p in parts])
    return jnp.pad(v, (0, LANES - v.shape[0])).reshape(1, LANES)


def _mixer(x, meta_tokens, norm_w, w_in, ssd_conv_w, ssd_conv_b, ssd_dt_bias, ssd_A_log, ssd_D,
           ssd_norm_w, gdn_conv_w, gdn_dt_bias, gdn_A_log, gdn_norm_w, w_branch_ssd, w_branch_gdn, w_out):
    b, seq, d = x.shape
    pad = CHUNK - N_META
    lp = seq + CHUNK
    nc = lp // CHUNK
    seg = _w_in_segments(d)
    w16_t = _cast_bf16(jnp.swapaxes(w_in, 0, 1), rows_pref=512, row_multiple=64)
    small = jnp.concatenate([w16_t[seg[n]] for n in ("dt", "a", "beta")], axis=0)
    w_small = jnp.pad(small, ((0, LANES - small.shape[0]), (0, 0)))

    meta = jnp.broadcast_to(meta_tokens[None].astype(x.dtype), (b, N_META, d))
    hp = jnp.concatenate([jnp.zeros((b, pad, d), x.dtype), meta, x], axis=1).reshape(b * lp, d)
    x2d = x.reshape(b * seq, d)

    proj_scan = _norm_matmul(hp, norm_w, w16_t, tm_pref=1056,
                             w_row_segments=_row_segments(seg, ("qkv", "xbc"))).reshape(b, lp, -1)
    raw_small = _norm_matmul(hp, norm_w, w_small, tm_pref=1056, out_dtype=F32,
                             w_row_segments=((0, LANES),)).reshape(b, lp, LANES)
    proj_tok = _norm_matmul(x2d, norm_w, w16_t, tm_pref=1024,
                            w_row_segments=_row_segments(seg, ("z", "gate", "merge")))

    zeros_h = jnp.zeros((GDN_HEADS,), F32)
    bias_vec = _lane_vec(ssd_dt_bias, gdn_dt_bias, zeros_h)
    alog_vec = _lane_vec(ssd_A_log, gdn_A_log, zeros_h)
    sp, cs, bt = _gates(raw_small, bias_vec, alog_vec, pad)
    cs_c = cs.reshape(b, nc, CHUNK, LANES)
    ssd_rows = jnp.swapaxes(cs_c[..., :SSD_HEADS], 2, 3)
    gdn_rows = jnp.swapaxes(cs_c[..., SSD_HEADS:SSD_HEADS + GDN_HEADS], 2, 3)

    d_vec = jnp.repeat(ssd_D.astype(F32), SSD_HEAD_DIM).reshape(1, SSD_D_INNER)
    y = _ssd(proj_scan, ssd_conv_w, ssd_conv_b, sp, cs, ssd_rows, d_vec, seq, pad)
    o = _gdn(proj_scan, gdn_conv_w, cs, bt, gdn_rows, gdn_norm_w, seq, pad)

    m = _merge(y.reshape(b * seq, -1), o.reshape(b * seq, -1), proj_tok, ssd_norm_w,
               w_branch_ssd.astype(BF16), w_branch_gdn.astype(BF16))
    return _matmul_residual(m, w_out.astype(BF16), x2d)


def _peer(h2, norm_w, w_query, sub_keys, expert_down, expert_up, final_w):
    ntok = h2.shape[0]
    q, hn_t = _norm_matmul(h2, norm_w, w_query.astype(BF16), tm_pref=512, with_transposed=True)
    e2, thr, f1 = _peer_route(q, sub_keys, _divisor_tile(ntok, 256, LANES))
    acc_t = _peer_experts(hn_t, e2, thr, f1, _cast_bf16(expert_down, rows_pref=1024),
                          _cast_bf16(expert_up, rows_pref=512, transpose=True),
                          _divisor_tile(ntok, 512, LANES), keys_per_block=8)
    return _residual_norm(h2, acc_t, final_w)


def kernel(x, meta_tokens, norm_mix_w, w_in, ssd_conv_w, ssd_conv_b, ssd_dt_bias, ssd_A_log, ssd_D, ssd_norm_w, gdn_conv_w, gdn_dt_bias, gdn_A_log, gdn_norm_w, w_branch_ssd, w_branch_gdn, w_out, norm_ffn_w, peer_w_query, peer_sub_keys, peer_expert_down, peer_expert_up, final_norm_w):
    b, seq, d = x.shape
    assert norm_mix_w.shape[0] == 1, "single-layer trunk"
    h2 = _mixer(x, meta_tokens, norm_mix_w[0], w_in[0], ssd_conv_w[0], ssd_conv_b[0], ssd_dt_bias[0],
                ssd_A_log[0], ssd_D[0], ssd_norm_w[0], gdn_conv_w[0], gdn_dt_bias[0], gdn_A_log[0],
                gdn_norm_w[0], w_branch_ssd[0], w_branch_gdn[0], w_out[0])
    out = _peer(h2, norm_ffn_w[0], peer_w_query[0], peer_sub_keys[0], peer_expert_down[0],
                peer_expert_up[0], final_norm_w)
    return out.reshape(b, seq, d)
```

```python
import functools

import jax
import jax.numpy as jnp
from jax import lax
from jax.experimental import pallas as pl
from jax.experimental.pallas import tpu as pltpu

F32 = jnp.float32
BF16 = jnp.bfloat16

N_META = 16
CHUNK = 64
CONV_K = 4
NORM_EPS = 1e-6
L2_EPS = 1e-6

SSD_HEADS = 32
SSD_HEAD_DIM = 64
SSD_GROUPS = 4
SSD_STATE = 128
SSD_HPG = SSD_HEADS // SSD_GROUPS
SSD_D_INNER = SSD_HEADS * SSD_HEAD_DIM
SSD_BC = SSD_GROUPS * SSD_STATE

GDN_HEADS = 16
GDN_DK = 128
GDN_DV = 128
GDN_DQK = GDN_HEADS * GDN_DK

PEER_HEADS = 8
PEER_N_KEYS = 128
PEER_TOPK = 16
PEER_D_KEY = 128
PEER_MATMUL_PARTS = 2

CONV_HALO = 16
LANES = 128
VMEM_LIMIT = 56 * 1024 * 1024

_NT = (((1,), (1,)), ((), ()))
_TN = (((0,), (0,)), ((), ()))


def _dot(a, b):
    return jnp.dot(a, b, preferred_element_type=F32)


def _dot_nt(a, b):
    return lax.dot_general(a, b, _NT, preferred_element_type=F32)


def _dot_tn(a, b):
    return lax.dot_general(a, b, _TN, preferred_element_type=F32)


def _silu(x):
    return x / (1.0 + jnp.exp(-x))


def _sigmoid(x):
    return 1.0 / (1.0 + jnp.exp(-x))


def _softplus(x):
    return jnp.maximum(x, 0.0) + jnp.log(1.0 + jnp.exp(-jnp.abs(x)))


def _divisor_tile(n, preferred, multiple):
    t = min(preferred, n)
    t -= t % multiple
    while t >= multiple:
        if n % t == 0:
            return t
        t -= multiple
    return n


def _params(*sem):
    return pltpu.CompilerParams(dimension_semantics=sem, vmem_limit_bytes=VMEM_LIMIT)


def _norm_matmul_body(x_ref, g_ref, w_ref, o_ref, *rest, w_is_transposed):
    n_ref = rest[-1]
    nt_ref = rest[0] if len(rest) == 2 else None

    @pl.when(pl.program_id(1) == 0)
    def _():
        x = x_ref[...]
        ms = jnp.mean(x * x, axis=-1, keepdims=True)
        n = x * lax.rsqrt(ms + NORM_EPS) * g_ref[...]
        n_ref[...] = n.astype(BF16)
        if nt_ref is not None:
            nt_ref[...] = n.T.astype(BF16)

    dot = _dot_nt if w_is_transposed else _dot
    o_ref[...] = dot(n_ref[...], w_ref[...]).astype(o_ref.dtype)


def _segment_row(j, segments, tn):
    unit = 16
    row, first_blk = None, 0
    for start, length in segments:
        assert start % unit == 0 and tn % unit == 0
        cand = start // unit + (j - first_blk) * (tn // unit)
        row = cand if row is None else jnp.where(j >= first_blk, cand, row)
        first_blk += length // tn
    return row * unit


def _norm_matmul(x2d, gain, w, tm_pref=768, tn_pref=1024, out_dtype=BF16, with_transposed=False,
                 w_row_segments=None):
    m, k = x2d.shape
    w_is_transposed = w_row_segments is not None
    n = sum(ln for _, ln in w_row_segments) if w_is_transposed else w.shape[1]
    tm = _divisor_tile(m, tm_pref, LANES if with_transposed else 16)
    tn = _divisor_tile(n, tn_pref, LANES)
    if w_is_transposed:
        assert all(ln % tn == 0 for _, ln in w_row_segments)
        w_spec = pl.BlockSpec((pl.Element(tn), pl.Element(k)),
                              lambda i, j: (_segment_row(j, w_row_segments, tn), 0))
    else:
        w_spec = pl.BlockSpec((k, tn), lambda i, j: (0, j))
    out_specs = pl.BlockSpec((tm, tn), lambda i, j: (i, j))
    out_shape = jax.ShapeDtypeStruct((m, n), out_dtype)
    if with_transposed:
        out_specs = (out_specs, pl.BlockSpec((k, tm), lambda i, j: (0, i)))
        out_shape = (out_shape, jax.ShapeDtypeStruct((k, m), BF16))
    return pl.pallas_call(
        functools.partial(_norm_matmul_body, w_is_transposed=w_is_transposed),
        grid=(m // tm, n // tn),
        in_specs=[
            pl.BlockSpec((tm, k), lambda i, j: (i, 0)),
            pl.BlockSpec((1, k), lambda i, j: (0, 0)),
            w_spec,
        ],
        out_specs=out_specs,
        out_shape=out_shape,
        scratch_shapes=[pltpu.VMEM((tm, k), BF16)],
        compiler_params=_params("arbitrary", "arbitrary"),
        name="norm_matmul",
    )(x2d, gain.reshape(1, k), w)


def _gates_body(raw_ref, bias_ref, alog_ref, sp_ref, cs_ref, bt_ref, *, n_chunks, pad):
    ri = lax.broadcasted_iota(jnp.int32, (CHUNK, CHUNK), 0)
    ci = lax.broadcasted_iota(jnp.int32, (CHUNK, CHUNK), 1)
    tri = (ri >= ci).astype(F32)
    neg_a = -jnp.exp(alog_ref[...])
    bias = bias_ref[...]
    row = lax.broadcasted_iota(jnp.int32, (CHUNK, 1), 0)

    def body(c, carry):
        sl = pl.ds(pl.multiple_of(c * CHUNK, CHUNK), CHUNK)
        r = raw_ref[0, sl, :]
        sp = _softplus(r + bias)
        cs = jnp.dot(tri, sp * neg_a, precision=lax.Precision.HIGHEST, preferred_element_type=F32)
        valid = (row + c * CHUNK) >= pad
        sp_ref[0, sl, :] = sp
        cs_ref[0, sl, :] = cs
        bt_ref[0, sl, :] = jnp.where(valid, _sigmoid(r), 0.0)
        return carry

    lax.fori_loop(0, n_chunks, body, 0)


def _gates(raw, bias_vec, alog_vec, pad):
    b, lp, w = raw.shape
    spec = pl.BlockSpec((1, lp, w), lambda i: (i, 0, 0))
    vec = pl.BlockSpec((1, w), lambda i: (0, 0))
    shp = jax.ShapeDtypeStruct((b, lp, w), F32)
    return pl.pallas_call(
        functools.partial(_gates_body, n_chunks=lp // CHUNK, pad=pad),
        grid=(b,),
        in_specs=[spec, vec, vec],
        out_specs=(spec, spec, spec),
        out_shape=(shp, shp, shp),
        compiler_params=_params("arbitrary"),
        name="gates",
    )(raw, bias_vec, alog_vec)


def _shift_matrix():
    r = jnp.arange((CONV_K - 1) * CHUNK)
    src = CONV_HALO + r % CHUNK - (CONV_K - 1 - r // CHUNK)
    return (jnp.arange(CONV_HALO + CHUNK)[None, :] == src[:, None]).astype(BF16)


def _shifted_taps(shift_ref, xpad_ref, taps_ref):
    taps_ref[...] = _dot(shift_ref[...], xpad_ref[...])


def _causal_conv(taps_ref, xpad_ref, w_ref, lo, width, bias):
    cols = slice(lo, lo + width)
    acc = w_ref[CONV_K - 1:CONV_K, cols] * xpad_ref[CONV_HALO:CONV_HALO + CHUNK, cols].astype(F32)
    if bias is not None:
        acc = acc + bias
    for j in range(CONV_K - 1):
        acc = acc + w_ref[j:j + 1, cols] * taps_ref[j * CHUNK:(j + 1) * CHUNK, cols]
    return acc


def _ssd_body(x_ref, bc_ref, cw_ref, cb_ref, sp_ref, cs_ref, csr_ref, dvec_ref, expand_ref, shift_ref,
              y_ref, xpad_ref, taps_ref, st_ref, *, pad):
    c = pl.program_id(1)

    @pl.when(c == 0)
    def _():
        st_ref[...] = jnp.zeros_like(st_ref)
        xpad_ref[0:CONV_HALO, :] = jnp.zeros((CONV_HALO, xpad_ref.shape[1]), BF16)

    xpad_ref[CONV_HALO:CONV_HALO + CHUNK, 0:SSD_D_INNER] = x_ref[0]
    xpad_ref[CONV_HALO:CONV_HALO + CHUNK, SSD_D_INNER:] = bc_ref[0]
    _shifted_taps(shift_ref, xpad_ref, taps_ref)

    row = lax.broadcasted_iota(jnp.int32, (CHUNK, 1), 0)
    valid = (row + c * CHUNK) >= pad
    ri = lax.broadcasted_iota(jnp.int32, (CHUNK, CHUNK), 0)
    ci = lax.broadcasted_iota(jnp.int32, (CHUNK, CHUNK), 1)
    tril = ri >= ci

    sp = sp_ref[0]
    cs = cs_ref[0]
    csr = csr_ref[0, 0]

    def conv(lo, width):
        a = _causal_conv(taps_ref, xpad_ref, cw_ref, lo, width, cb_ref[:, lo:lo + width])
        return jnp.where(valid, _silu(a), 0.0)

    gw = SSD_HPG * SSD_HEAD_DIM
    groups = range(SSD_GROUPS)
    heads = range(SSD_HEADS)
    xg = [conv(g * gw, gw) for g in groups]
    bg = [conv(SSD_D_INNER + g * SSD_STATE, SSD_STATE).astype(BF16) for g in groups]
    cg = [conv(SSD_D_INNER + SSD_BC + g * SSD_STATE, SSD_STATE).astype(BF16) for g in groups]
    cb = [_dot_nt(cg[g], bg[g]) for g in groups]
    st = [st_ref[g] for g in groups]
    yoff = [_dot(cg[g], st[g].astype(BF16)) for g in groups]

    last_row = cs[CHUNK - 1:CHUNK, :]
    factors = (sp, sp * jnp.exp(last_row - cs), jnp.exp(cs))
    pieces = []
    for f in factors:
        hi = f.astype(BF16)
        pieces += [hi, (f - hi.astype(F32)).astype(BF16)]
    spread = _dot(jnp.concatenate(pieces, axis=0), expand_ref[...])
    e_dt, e_w, e_cs = (spread[(2 * k) * CHUNK:(2 * k + 1) * CHUNK] + spread[(2 * k + 1) * CHUNK:(2 * k + 2) * CHUNK]
                       for k in range(len(factors)))

    xdt = [(xg[g] * e_dt[:, g * gw:(g + 1) * gw]).astype(BF16) for g in groups]
    xw = [(xg[g] * e_w[:, g * gw:(g + 1) * gw]).astype(BF16) for g in groups]
    w_intra = []
    for h in heads:
        decay = jnp.where(tril, jnp.exp(cs[:, h:h + 1] - csr[h:h + 1, :]), 0.0)
        w_intra.append((cb[h // SSD_HPG] * decay).astype(BF16))
    ydiag = [_dot(w_intra[h], xdt[h // SSD_HPG][:, (h % SSD_HPG) * SSD_HEAD_DIM:(h % SSD_HPG + 1) * SSD_HEAD_DIM])
             for h in heads]
    snew = [_dot_tn(bg[g], xw[g]) for g in groups]
    for g in groups:
        sl = slice(g * gw, (g + 1) * gw)
        st_ref[g] = st[g] * e_cs[CHUNK - 1:CHUNK, sl] + snew[g]
        y_g = (jnp.concatenate(ydiag[g * SSD_HPG:(g + 1) * SSD_HPG], axis=1) + yoff[g] * e_cs[:, sl]
               + dvec_ref[:, sl] * xg[g])
        y_ref[0, :, sl] = y_g.astype(y_ref.dtype)

    xpad_ref[0:CONV_HALO, :] = xpad_ref[CHUNK:CHUNK + CONV_HALO, :]


def _ssd(proj, conv_w, conv_b, sp, cs, cs_rows, d_vec, seq, pad):
    b, lp, _ = proj.shape
    nc = lp // CHUNK
    conv_dim = SSD_D_INNER + 2 * SSD_BC
    x_blk = 3 * GDN_DQK // SSD_D_INNER
    bc_blk = (3 * GDN_DQK + SSD_D_INNER) // (2 * SSD_BC)
    expand = (jnp.arange(LANES)[:, None] == jnp.arange(SSD_D_INNER)[None, :] // SSD_HEAD_DIM).astype(BF16)
    return pl.pallas_call(
        functools.partial(_ssd_body, pad=pad),
        grid=(b, nc),
        in_specs=[
            pl.BlockSpec((1, CHUNK, SSD_D_INNER), lambda i, c: (i, c, x_blk)),
            pl.BlockSpec((1, CHUNK, 2 * SSD_BC), lambda i, c: (i, c, bc_blk)),
            pl.BlockSpec((CONV_K, conv_dim), lambda i, c: (0, 0)),
            pl.BlockSpec((1, conv_dim), lambda i, c: (0, 0)),
            pl.BlockSpec((1, CHUNK, LANES), lambda i, c: (i, c, 0)),
            pl.BlockSpec((1, CHUNK, LANES), lambda i, c: (i, c, 0)),
            pl.BlockSpec((1, 1, SSD_HEADS, CHUNK), lambda i, c: (i, c, 0, 0)),
            pl.BlockSpec((1, SSD_D_INNER), lambda i, c: (0, 0)),
            pl.BlockSpec((LANES, SSD_D_INNER), lambda i, c: (0, 0)),
            pl.BlockSpec(((CONV_K - 1) * CHUNK, CONV_HALO + CHUNK), lambda i, c: (0, 0)),
        ],
        out_specs=pl.BlockSpec((1, CHUNK, SSD_D_INNER), lambda i, c: (i, jnp.maximum(c - 1, 0), 0)),
        out_shape=jax.ShapeDtypeStruct((b, seq, SSD_D_INNER), BF16),
        scratch_shapes=[
            pltpu.VMEM((CONV_HALO + CHUNK, conv_dim), BF16),
            pltpu.VMEM(((CONV_K - 1) * CHUNK, conv_dim), F32),
            pltpu.VMEM((SSD_GROUPS, SSD_STATE, SSD_HPG * SSD_HEAD_DIM), F32),
        ],
        compiler_params=_params("arbitrary", "arbitrary"),
        name="ssd_scan",
    )(proj, proj, conv_w, conv_b.reshape(1, conv_dim), sp, cs, cs_rows, d_vec, expand, _shift_matrix())


def _unit_lower_inverses(strict_ls, eye):
    ps = [l.astype(BF16) for l in strict_ls]
    ts = [eye - l for l in strict_ls]
    n = 2
    while n < CHUNK:
        p2 = [_dot(p, p).astype(BF16) for p in ps]
        ts = [t + _dot(t.astype(BF16), q) for t, q in zip(ts, p2)]
        ps = p2
        n *= 2
    return ts


def _gdn_body(q_ref, k_ref, v_ref, cw_ref, cs_ref, bt_ref, gr_ref, nw_ref, shift_ref, o_ref,
              xpad_ref, taps_ref, st_ref, *, pad):
    c = pl.program_id(1)

    @pl.when(c == 0)
    def _():
        st_ref[...] = jnp.zeros_like(st_ref)
        xpad_ref[0:CONV_HALO, :] = jnp.zeros((CONV_HALO, xpad_ref.shape[1]), BF16)

    xpad_ref[CONV_HALO:CONV_HALO + CHUNK, 0:GDN_DQK] = q_ref[0]
    xpad_ref[CONV_HALO:CONV_HALO + CHUNK, GDN_DQK:2 * GDN_DQK] = k_ref[0]
    xpad_ref[CONV_HALO:CONV_HALO + CHUNK, 2 * GDN_DQK:] = v_ref[0]
    _shifted_taps(shift_ref, xpad_ref, taps_ref)

    row = lax.broadcasted_iota(jnp.int32, (CHUNK, 1), 0)
    valid = (row + c * CHUNK) >= pad
    ri = lax.broadcasted_iota(jnp.int32, (CHUNK, CHUNK), 0)
    ci = lax.broadcasted_iota(jnp.int32, (CHUNK, CHUNK), 1)
    tril = ri >= ci
    strict = ri > ci
    eye = (ri == ci).astype(F32)

    cs = cs_ref[0]
    bt = bt_ref[0]
    gr = gr_ref[0, 0]
    nw = nw_ref[...]
    g_off = SSD_HEADS
    b_off = SSD_HEADS + GDN_HEADS

    def conv(lo):
        a = _causal_conv(taps_ref, xpad_ref, cw_ref, lo, GDN_DK, None)
        return jnp.where(valid, _silu(a), 0.0)

    heads = range(GDN_HEADS)
    q16, k16, kb16, rhs16, qg, kd16, decay, elast = [], [], [], [], [], [], [], []
    for h in heads:
        q = conv(h * GDN_DK)
        k = conv(GDN_DQK + h * GDN_DK)
        v = conv(2 * GDN_DQK + h * GDN_DV)
        q = q * lax.rsqrt(jnp.sum(q * q, axis=-1, keepdims=True) + L2_EPS) * (GDN_DK ** -0.5)
        k = k * lax.rsqrt(jnp.sum(k * k, axis=-1, keepdims=True) + L2_EPS)
        col = cs[:, g_off + h:g_off + h + 1]
        rowv = gr[h:h + 1, :]
        last = rowv[:, CHUNK - 1:CHUNK]
        beta = bt[:, b_off + h:b_off + h + 1]
        ecol = jnp.exp(col)
        kb = k * beta
        decay.append(jnp.where(tril, jnp.exp(col - rowv), 0.0))
        q16.append(q.astype(BF16))
        k16.append(k.astype(BF16))
        kb16.append(kb.astype(BF16))
        rhs16.append(jnp.concatenate([v * beta, kb * ecol], axis=1).astype(BF16))
        qg.append(q * ecol)
        kd16.append((k * jnp.exp(last - col)).astype(BF16))
        elast.append(jnp.exp(last))
    kk = [_dot_nt(kb16[h], k16[h]) for h in heads]
    qk = [_dot_nt(q16[h], k16[h]) for h in heads]
    tinv = _unit_lower_inverses([jnp.where(strict, kk[h] * decay[h], 0.0) for h in heads], eye)
    uw = [_dot(tinv[h].astype(BF16), rhs16[h]) for h in heads]
    s = [st_ref[h] for h in heads]
    ws = [_dot(jnp.concatenate([uw[h][:, GDN_DV:], qg[h]], axis=0).astype(BF16), s[h].astype(BF16))
          for h in heads]
    v16 = [(uw[h][:, :GDN_DV] - ws[h][:CHUNK]).astype(BF16) for h in heads]
    o_intra = [_dot((qk[h] * decay[h]).astype(BF16), v16[h]) for h in heads]
    s_add = [_dot_tn(kd16[h], v16[h]) for h in heads]
    for h in heads:
        st_ref[h] = s[h] * elast[h] + s_add[h]
        o = ws[h][CHUNK:] + o_intra[h]
        o = o * lax.rsqrt(jnp.mean(o * o, axis=-1, keepdims=True) + NORM_EPS) * nw
        o_ref[0, :, h * GDN_DV:(h + 1) * GDN_DV] = o.astype(o_ref.dtype)

    xpad_ref[0:CONV_HALO, :] = xpad_ref[CHUNK:CHUNK + CONV_HALO, :]


def _gdn(proj, conv_w, cs, bt, g_rows, norm_w, seq, pad):
    b, lp, _ = proj.shape
    nc = lp // CHUNK
    conv_dim = 3 * GDN_DQK
    qkv = lambda blk: pl.BlockSpec((1, CHUNK, GDN_DQK), lambda i, c: (i, c, blk))
    return pl.pallas_call(
        functools.partial(_gdn_body, pad=pad),
        grid=(b, nc),
        in_specs=[
            qkv(0), qkv(1), qkv(2),
            pl.BlockSpec((CONV_K, conv_dim), lambda i, c: (0, 0)),
            pl.BlockSpec((1, CHUNK, LANES), lambda i, c: (i, c, 0)),
            pl.BlockSpec((1, CHUNK, LANES), lambda i, c: (i, c, 0)),
            pl.BlockSpec((1, 1, GDN_HEADS, CHUNK), lambda i, c: (i, c, 0, 0)),
            pl.BlockSpec((1, GDN_DV), lambda i, c: (0, 0)),
            pl.BlockSpec(((CONV_K - 1) * CHUNK, CONV_HALO + CHUNK), lambda i, c: (0, 0)),
        ],
        out_specs=pl.BlockSpec((1, CHUNK, GDN_DQK), lambda i, c: (i, jnp.maximum(c - 1, 0), 0)),
        out_shape=jax.ShapeDtypeStruct((b, seq, GDN_HEADS * GDN_DV), BF16),
        scratch_shapes=[
            pltpu.VMEM((CONV_HALO + CHUNK, conv_dim), BF16),
            pltpu.VMEM(((CONV_K - 1) * CHUNK, conv_dim), F32),
            pltpu.VMEM((GDN_HEADS, GDN_DK, GDN_DV), F32),
        ],
        compiler_params=_params("arbitrary", "arbitrary"),
        name="gdn_scan",
    )(proj, proj, proj, conv_w, cs, bt, g_rows, norm_w.reshape(1, GDN_DV), _shift_matrix())


def _merge_body(y_ref, z_ref, o_ref, gate_ref, g1_ref, g2_ref, nw_ref, w1_ref, w2_ref, m_ref,
                a1_ref, a2_ref):
    @pl.when(pl.program_id(1) == 0)
    def _():
        gw = SSD_D_INNER // SSD_GROUPS
        for g in range(SSD_GROUPS):
            sl = slice(g * gw, (g + 1) * gw)
            t = y_ref[:, sl].astype(F32) * _silu(z_ref[:, sl].astype(F32))
            ms = jnp.mean(t * t, axis=-1, keepdims=True)
            a1_ref[:, sl] = (t * lax.rsqrt(ms + NORM_EPS) * nw_ref[:, sl]).astype(BF16)
        a2_ref[...] = (o_ref[...].astype(F32) * _silu(gate_ref[...].astype(F32))).astype(BF16)

    m = (_sigmoid(g1_ref[...].astype(F32)) * _dot(a1_ref[...], w1_ref[...])
         + _sigmoid(g2_ref[...].astype(F32)) * _dot(a2_ref[...], w2_ref[...]))
    m_ref[...] = m.astype(m_ref.dtype)


def _merge(y, o, proj_tok, norm_w, w1, w2, tm_pref=256, tn_pref=2048):
    m, d = y.shape
    n = w1.shape[1]
    tm = _divisor_tile(m, tm_pref, 16)
    tn = _divisor_tile(n, tn_pref, LANES)
    nj = n // tn
    row_d = lambda blk: pl.BlockSpec((tm, d), lambda i, j: (i, blk))
    return pl.pallas_call(
        _merge_body,
        grid=(m // tm, nj),
        in_specs=[
            pl.BlockSpec((tm, d), lambda i, j: (i, 0)),
            row_d(0),
            pl.BlockSpec((tm, d), lambda i, j: (i, 0)),
            row_d(1),
            pl.BlockSpec((tm, tn), lambda i, j: (i, 2 * nj + j)),
            pl.BlockSpec((tm, tn), lambda i, j: (i, 3 * nj + j)),
            pl.BlockSpec((1, d), lambda i, j: (0, 0)),
            pl.BlockSpec((d, tn), lambda i, j: (0, j)),
            pl.BlockSpec((d, tn), lambda i, j: (0, j)),
        ],
        out_specs=pl.BlockSpec((tm, tn), lambda i, j: (i, j)),
        out_shape=jax.ShapeDtypeStruct((m, n), BF16),
        scratch_shapes=[pltpu.VMEM((tm, d), BF16), pltpu.VMEM((tm, d), BF16)],
        compiler_params=_params("arbitrary", "arbitrary"),
        name="merge",
    )(y, proj_tok, o, proj_tok, proj_tok, proj_tok, norm_w.reshape(1, d), w1, w2)


def _matmul_residual_body(a_ref, w_ref, r_ref, o_ref):
    o_ref[...] = r_ref[...] + _dot(a_ref[...], w_ref[...])


def _matmul_residual(a, w, res, tm_pref=512, tn_pref=2048):
    m, k = a.shape
    n = w.shape[1]
    tm = _divisor_tile(m, tm_pref, 8)
    tn = _divisor_tile(n, tn_pref, LANES)
    return pl.pallas_call(
        _matmul_residual_body,
        grid=(m // tm, n // tn),
        in_specs=[
            pl.BlockSpec((tm, k), lambda i, j: (i, 0)),
            pl.BlockSpec((k, tn), lambda i, j: (0, j)),
            pl.BlockSpec((tm, tn), lambda i, j: (i, j)),
        ],
        out_specs=pl.BlockSpec((tm, tn), lambda i, j: (i, j)),
        out_shape=jax.ShapeDtypeStruct((m, n), F32),
        compiler_params=_params("arbitrary", "arbitrary"),
        name="out_proj_residual",
    )(a, w, res)


def _top_values(x, count):
    out = []
    for _ in range(count):
        m = jnp.max(x, axis=0, keepdims=True)
        out.append(m)
        x = jnp.where(x == m, -jnp.inf, x)
    return out


def _peer_route_body(q_ref, keys_ref, e2_ref, thr_ref, f1_ref):
    k1 = keys_ref[0].astype(BF16)
    k2 = keys_ref[1].astype(BF16)
    kk = PEER_TOPK
    for h in range(PEER_HEADS):
        lo = 2 * h * PEER_D_KEY
        q1 = q_ref[:, lo:lo + PEER_D_KEY]
        q2 = q_ref[:, lo + PEER_D_KEY:lo + 2 * PEER_D_KEY]
        s1 = _dot_nt(k1, q1)
        s2 = _dot_nt(k2, q2)
        t1 = _top_values(s1, kk + 1)
        t2 = _top_values(s2, kk + 1)
        t1_blk = jnp.concatenate(t1[:kk], axis=0)
        t2_blk = jnp.concatenate(t2[:kk], axis=0)
        half = kk // 2
        cand = jnp.concatenate(
            [t1[0] + t2_blk]
            + [t1[a] + t2_blk[:half] for a in range(1, half)]
            + [t1_blk[half:] + t2[0]], axis=0)
        best = _top_values(cand, kk + 1)
        runner_up = jnp.maximum(best[kk], jnp.maximum(t1[kk] + t2[0], t1[0] + t2[kk]))
        tau = 0.5 * (best[kk - 1] + runner_up)
        m1, m2 = t1[0], t2[0]
        z = jnp.sum(jnp.where(cand >= tau, jnp.exp(cand - (m1 + m2)), 0.0), axis=0, keepdims=True)
        e2 = jnp.exp(s2 - m2)
        for s in range(e2_ref.shape[1]):
            e2_ref[h, s] = e2[:, s * LANES:(s + 1) * LANES]
        thr_ref[h] = jnp.exp(tau - s1 - m2)
        f1_ref[h] = jnp.exp(s1 - m1) / z


def _peer_route(q, keys, t_tok):
    ntok = q.shape[0]
    spec = pl.BlockSpec((PEER_HEADS, PEER_N_KEYS, t_tok), lambda t: (0, 0, t))
    shp = jax.ShapeDtypeStruct((PEER_HEADS, PEER_N_KEYS, ntok), F32)
    strip_spec = pl.BlockSpec((PEER_HEADS, t_tok // LANES, PEER_N_KEYS, LANES), lambda t: (0, t, 0, 0))
    strip_shp = jax.ShapeDtypeStruct((PEER_HEADS, ntok // LANES, PEER_N_KEYS, LANES), F32)
    return pl.pallas_call(
        _peer_route_body,
        grid=(ntok // t_tok,),
        in_specs=[
            pl.BlockSpec((t_tok, q.shape[1]), lambda t: (t, 0)),
            pl.BlockSpec(keys.shape, lambda t: (0, 0, 0)),
        ],
        out_specs=(strip_spec, spec, spec),
        out_shape=(strip_shp, shp, shp),
        compiler_params=_params("arbitrary"),
        name="peer_route",
    )(q, keys)


def _peer_expert_body(hnt_ref, e2_ref, thr_ref, f1_ref, down_ref, upt_ref, acc_ref,
                      w_ref, act_ref, p_ref):
    d, t_tok = hnt_ref.shape
    te = down_ref.shape[0]
    n_strip = t_tok // LANES
    n_part = min(PEER_MATMUL_PARTS, n_strip)
    rows_a = te // n_part
    rows_o = d // n_part
    e = pl.program_id(1)
    last = pl.num_programs(1) - 1
    slot = lax.rem(e, 2)

    def run(build, project):
        for part in range(n_part):
            if build:
                for s in range(part * n_strip // n_part, (part + 1) * n_strip // n_part):
                    ln = slice(s * LANES, (s + 1) * LANES)
                    for i in range(te // PEER_N_KEYS):
                        w = None
                        for hd in range(PEER_HEADS):
                            thr = thr_ref[hd, i:i + 1, ln]
                            f1 = f1_ref[hd, i:i + 1, ln]
                            e2 = e2_ref[hd, s]
                            term = f1 * jnp.where(e2 >= thr, e2, 0.0)
                            w = term if w is None else w + term
                        w_ref[i * PEER_N_KEYS:(i + 1) * PEER_N_KEYS, ln] = w
                ra = slice(part * rows_a, (part + 1) * rows_a)
                a = _dot(down_ref[ra, :], hnt_ref[...])
                act_ref[ra, :] = 0.5 * a * (1.0 + lax.erf(a * (2.0 ** -0.5)))
            if project:
                ro = slice(part * rows_o, (part + 1) * rows_o)
                acc_ref[ro, :] += _dot(upt_ref[ro, :], p_ref[1 - slot])
        if build:
            p_ref[slot] = (w_ref[...] * act_ref[...]).astype(BF16)

    @pl.when(e == 0)
    def _():
        acc_ref[...] = jnp.zeros_like(acc_ref)
        run(build=True, project=False)

    @pl.when(jnp.logical_and(e > 0, e < last))
    def _():
        run(build=True, project=True)

    @pl.when(e == last)
    def _():
        run(build=False, project=True)


def _peer_experts(hn_t, e2, thr, f1, down, up_t, t_tok, keys_per_block):
    d, ntok = hn_t.shape
    n_exp = down.shape[0]
    te = keys_per_block * PEER_N_KEYS
    n_blk = n_exp // te
    cur = lambda e: jnp.minimum(e, n_blk - 1)
    prev = lambda e: jnp.maximum(e - 1, 0)
    full = pl.BlockSpec((PEER_HEADS, t_tok // LANES, PEER_N_KEYS, LANES), lambda t, e: (0, t, 0, 0))
    part = pl.BlockSpec((PEER_HEADS, keys_per_block, t_tok), lambda t, e: (0, cur(e), t))
    return pl.pallas_call(
        _peer_expert_body,
        grid=(ntok // t_tok, n_blk + 1),
        in_specs=[
            pl.BlockSpec((d, t_tok), lambda t, e: (0, t)),
            full, part, part,
            pl.BlockSpec((te, d), lambda t, e: (cur(e), 0)),
            pl.BlockSpec((d, te), lambda t, e: (0, prev(e))),
        ],
        out_specs=pl.BlockSpec((d, t_tok), lambda t, e: (0, t)),
        out_shape=jax.ShapeDtypeStruct((d, ntok), F32),
        scratch_shapes=[
            pltpu.VMEM((te, t_tok), F32),
            pltpu.VMEM((te, t_tok), F32),
            pltpu.VMEM((2, te, t_tok), BF16),
        ],
        compiler_params=_params("arbitrary", "arbitrary"),
        name="peer_experts",
    )(hn_t, e2, thr, f1, down, up_t)


def _residual_norm_body(h_ref, acc_ref, g_ref, o_ref):
    y = h_ref[...] + acc_ref[...].T
    ms = jnp.mean(y * y, axis=-1, keepdims=True)
    o_ref[...] = y * lax.rsqrt(ms + NORM_EPS) * g_ref[...]


def _residual_norm(h2, acc_t, gain, tm_pref=256):
    m, d = h2.shape
    tm = _divisor_tile(m, tm_pref, LANES)
    return pl.pallas_call(
        _residual_norm_body,
        grid=(m // tm,),
        in_specs=[
            pl.BlockSpec((tm, d), lambda i: (i, 0)),
            pl.BlockSpec((d, tm), lambda i: (0, i)),
            pl.BlockSpec((1, d), lambda i: (0, 0)),
        ],
        out_specs=pl.BlockSpec((tm, d), lambda i: (i, 0)),
        out_shape=jax.ShapeDtypeStruct((m, d), F32),
        compiler_params=_params("arbitrary"),
        name="residual_norm",
    )(h2, acc_t, gain.reshape(1, d))


def _cast_body(x_ref, o_ref, *, transpose):
    x = x_ref[...]
    o_ref[...] = (x.T if transpose else x).astype(o_ref.dtype)


def _cast_bf16(w, rows_pref, transpose=False, row_multiple=LANES):
    r, c = w.shape
    tr = _divisor_tile(r, rows_pref, row_multiple)
    out_spec = pl.BlockSpec((c, tr), lambda i: (0, i)) if transpose else pl.BlockSpec((tr, c), lambda i: (i, 0))
    return pl.pallas_call(
        functools.partial(_cast_body, transpose=transpose),
        grid=(r // tr,),
        in_specs=[pl.BlockSpec((tr, c), lambda i: (i, 0))],
        out_specs=out_spec,
        out_shape=jax.ShapeDtypeStruct((c, r) if transpose else (r, c), BF16),
        compiler_params=_params("arbitrary"),
        name="cast_bf16",
    )(w)


def _w_in_segments(d):
    widths = (("z", SSD_D_INNER), ("xbc", SSD_D_INNER + 2 * SSD_BC), ("dt", SSD_HEADS),
              ("qkv", 3 * GDN_DQK), ("gate", GDN_HEADS * GDN_DV), ("a", GDN_HEADS),
              ("beta", GDN_HEADS), ("merge", 2 * d))
    seg, off = {}, 0
    for name, wd in widths:
        seg[name] = slice(off, off + wd)
        off += wd
    return seg


def _row_segments(seg, names):
    return tuple((seg[n].start, seg[n].stop - seg[n].start) for n in names)


def _lane_vec(*parts):
    v = jnp.concatenate([p.astype(F32) for p in parts])
    return jnp.pad(v, (0, LANES - v.shape[0])).reshape(1, LANES)


def _mixer(x, meta_tokens, norm_w, w_in, ssd_conv_w, ssd_conv_b, ssd_dt_bias, ssd_A_log, ssd_D,
           ssd_norm_w, gdn_conv_w, gdn_dt_bias, gdn_A_log, gdn_norm_w, w_branch_ssd, w_branch_gdn, w_out):
    b, seq, d = x.shape
    pad = CHUNK - N_META
    lp = seq + CHUNK
    nc = lp // CHUNK
    seg = _w_in_segments(d)
    w16_t = _cast_bf16(jnp.swapaxes(w_in, 0, 1), rows_pref=512, row_multiple=64)
    small = jnp.concatenate([w16_t[seg[n]] for n in ("dt", "a", "beta")], axis=0)
    w_small = jnp.pad(small, ((0, LANES - small.shape[0]), (0, 0)))

    meta = jnp.broadcast_to(meta_tokens[None].astype(x.dtype), (b, N_META, d))
    hp = jnp.concatenate([jnp.zeros((b, pad, d), x.dtype), meta, x], axis=1).reshape(b * lp, d)
    x2d = x.reshape(b * seq, d)

    proj_scan = _norm_matmul(hp, norm_w, w16_t, tm_pref=1056,
                             w_row_segments=_row_segments(seg, ("qkv", "xbc"))).reshape(b, lp, -1)
    raw_small = _norm_matmul(hp, norm_w, w_small, tm_pref=1056, out_dtype=F32,
                             w_row_segments=((0, LANES),)).reshape(b, lp, LANES)
    proj_tok = _norm_matmul(x2d, norm_w, w16_t, tm_pref=1024,
                            w_row_segments=_row_segments(seg, ("z", "gate", "merge")))

    zeros_h = jnp.zeros((GDN_HEADS,), F32)
    bias_vec = _lane_vec(ssd_dt_bias, gdn_dt_bias, zeros_h)
    alog_vec = _lane_vec(ssd_A_log, gdn_A_log, zeros_h)
    sp, cs, bt = _gates(raw_small, bias_vec, alog_vec, pad)
    cs_c = cs.reshape(b, nc, CHUNK, LANES)
    ssd_rows = jnp.swapaxes(cs_c[..., :SSD_HEADS], 2, 3)
    gdn_rows = jnp.swapaxes(cs_c[..., SSD_HEADS:SSD_HEADS + GDN_HEADS], 2, 3)

    d_vec = jnp.repeat(ssd_D.astype(F32), SSD_HEAD_DIM).reshape(1, SSD_D_INNER)
    y = _ssd(proj_scan, ssd_conv_w, ssd_conv_b, sp, cs, ssd_rows, d_vec, seq, pad)
    o = _gdn(proj_scan, gdn_conv_w, cs, bt, gdn_rows, gdn_norm_w, seq, pad)

    m = _merge(y.reshape(b * seq, -1), o.reshape(b * seq, -1), proj_tok, ssd_norm_w,
               w_branch_ssd.astype(BF16), w_branch_gdn.astype(BF16))
    return _matmul_residual(m, w_out.astype(BF16), x2d)


def _peer(h2, norm_w, w_query, sub_keys, expert_down, expert_up, final_w):
    ntok = h2.shape[0]
    q, hn_t = _norm_matmul(h2, norm_w, w_query.astype(BF16), tm_pref=512, tn_pref=2048,
                           with_transposed=True)
    e2, thr, f1 = _peer_route(q, sub_keys, _divisor_tile(ntok, 256, LANES))
    acc_t = _peer_experts(hn_t, e2, thr, f1, _cast_bf16(expert_down, rows_pref=1024),
                          _cast_bf16(expert_up, rows_pref=512, transpose=True),
                          _divisor_tile(ntok, 512, LANES), keys_per_block=8)
    return _residual_norm(h2, acc_t, final_w)


def kernel(x, meta_tokens, norm_mix_w, w_in, ssd_conv_w, ssd_conv_b, ssd_dt_bias, ssd_A_log, ssd_D, ssd_norm_w, gdn_conv_w, gdn_dt_bias, gdn_A_log, gdn_norm_w, w_branch_ssd, w_branch_gdn, w_out, norm_ffn_w, peer_w_query, peer_sub_keys, peer_expert_down, peer_expert_up, final_norm_w):
    b, seq, d = x.shape
    assert norm_mix_w.shape[0] == 1, "single-layer trunk"
    h2 = _mixer(x, meta_tokens, norm_mix_w[0], w_in[0], ssd_conv_w[0], ssd_conv_b[0], ssd_dt_bias[0],
                ssd_A_log[0], ssd_D[0], ssd_norm_w[0], gdn_conv_w[0], gdn_dt_bias[0], gdn_A_log[0],
                gdn_norm_w[0], w_branch_ssd[0], w_branch_gdn[0], w_out[0])
    out = _peer(h2, norm_ffn_w[0], peer_w_query[0], peer_sub_keys[0], peer_expert_down[0],
                peer_expert_up[0], final_norm_w)
    return out.reshape(b, seq, d)
```

```python
import functools

import jax
import jax.numpy as jnp
from jax import lax
from jax.experimental import pallas as pl
from jax.experimental.pallas import tpu as pltpu

F32 = jnp.float32
BF16 = jnp.bfloat16

N_META = 16
CHUNK = 64
CONV_K = 4
NORM_EPS = 1e-6
L2_EPS = 1e-6

SSD_HEADS = 32
SSD_HEAD_DIM = 64
SSD_GROUPS = 4
SSD_STATE = 128
SSD_HPG = SSD_HEADS // SSD_GROUPS
SSD_D_INNER = SSD_HEADS * SSD_HEAD_DIM
SSD_BC = SSD_GROUPS * SSD_STATE

GDN_HEADS = 16
GDN_DK = 128
GDN_DV = 128
GDN_DQK = GDN_HEADS * GDN_DK

PEER_HEADS = 8
PEER_N_KEYS = 128
PEER_TOPK = 16
PEER_D_KEY = 128
PEER_MATMUL_PARTS = 2

CONV_HALO = 16
LANES = 128
VMEM_LIMIT = 56 * 1024 * 1024

_NT = (((1,), (1,)), ((), ()))
_TN = (((0,), (0,)), ((), ()))


def _dot(a, b):
    return jnp.dot(a, b, preferred_element_type=F32)


def _dot_nt(a, b):
    return lax.dot_general(a, b, _NT, preferred_element_type=F32)


def _dot_tn(a, b):
    return lax.dot_general(a, b, _TN, preferred_element_type=F32)


def _silu(x):
    return x / (1.0 + jnp.exp(-x))


def _sigmoid(x):
    return 1.0 / (1.0 + jnp.exp(-x))


def _softplus(x):
    return jnp.maximum(x, 0.0) + jnp.log(1.0 + jnp.exp(-jnp.abs(x)))


def _divisor_tile(n, preferred, multiple):
    t = min(preferred, n)
    t -= t % multiple
    while t >= multiple:
        if n % t == 0:
            return t
        t -= multiple
    return n


def _params(*sem):
    return pltpu.CompilerParams(dimension_semantics=sem, vmem_limit_bytes=VMEM_LIMIT)


def _norm_matmul_body(x_ref, g_ref, w_ref, o_ref, *rest, w_is_transposed):
    n_ref = rest[-1]
    nt_ref = rest[0] if len(rest) == 2 else None

    @pl.when(pl.program_id(1) == 0)
    def _():
        x = x_ref[...]
        ms = jnp.mean(x * x, axis=-1, keepdims=True)
        n = x * lax.rsqrt(ms + NORM_EPS) * g_ref[...]
        n_ref[...] = n.astype(BF16)
        if nt_ref is not None:
            nt_ref[...] = n.T.astype(BF16)

    dot = _dot_nt if w_is_transposed else _dot
    o_ref[...] = dot(n_ref[...], w_ref[...]).astype(o_ref.dtype)


def _segment_row(j, segments, tn):
    unit = 16
    row, first_blk = None, 0
    for start, length in segments:
        assert start % unit == 0 and tn % unit == 0
        cand = start // unit + (j - first_blk) * (tn // unit)
        row = cand if row is None else jnp.where(j >= first_blk, cand, row)
        first_blk += length // tn
    return row * unit


def _norm_matmul(x2d, gain, w, tm_pref=768, tn_pref=1024, out_dtype=BF16, with_transposed=False,
                 w_row_segments=None):
    m, k = x2d.shape
    w_is_transposed = w_row_segments is not None
    n = sum(ln for _, ln in w_row_segments) if w_is_transposed else w.shape[1]
    tm = _divisor_tile(m, tm_pref, LANES if with_transposed else 16)
    tn = _divisor_tile(n, tn_pref, LANES)
    if w_is_transposed:
        assert all(ln % tn == 0 for _, ln in w_row_segments)
        w_spec = pl.BlockSpec((pl.Element(tn), pl.Element(k)),
                              lambda i, j: (_segment_row(j, w_row_segments, tn), 0))
    else:
        w_spec = pl.BlockSpec((k, tn), lambda i, j: (0, j))
    out_specs = pl.BlockSpec((tm, tn), lambda i, j: (i, j))
    out_shape = jax.ShapeDtypeStruct((m, n), out_dtype)
    if with_transposed:
        out_specs = (out_specs, pl.BlockSpec((k, tm), lambda i, j: (0, i)))
        out_shape = (out_shape, jax.ShapeDtypeStruct((k, m), BF16))
    return pl.pallas_call(
        functools.partial(_norm_matmul_body, w_is_transposed=w_is_transposed),
        grid=(m // tm, n // tn),
        in_specs=[
            pl.BlockSpec((tm, k), lambda i, j: (i, 0)),
            pl.BlockSpec((1, k), lambda i, j: (0, 0)),
            w_spec,
        ],
        out_specs=out_specs,
        out_shape=out_shape,
        scratch_shapes=[pltpu.VMEM((tm, k), BF16)],
        compiler_params=_params("arbitrary", "arbitrary"),
        name="norm_matmul",
    )(x2d, gain.reshape(1, k), w)


def _gates_body(raw_ref, bias_ref, alog_ref, sp_ref, cs_ref, bt_ref, *, n_chunks, pad):
    ri = lax.broadcasted_iota(jnp.int32, (CHUNK, CHUNK), 0)
    ci = lax.broadcasted_iota(jnp.int32, (CHUNK, CHUNK), 1)
    tri = (ri >= ci).astype(F32)
    neg_a = -jnp.exp(alog_ref[...])
    bias = bias_ref[...]
    row = lax.broadcasted_iota(jnp.int32, (CHUNK, 1), 0)

    def body(c, carry):
        sl = pl.ds(pl.multiple_of(c * CHUNK, CHUNK), CHUNK)
        r = raw_ref[0, sl, :]
        sp = _softplus(r + bias)
        cs = jnp.dot(tri, sp * neg_a, precision=lax.Precision.HIGHEST, preferred_element_type=F32)
        valid = (row + c * CHUNK) >= pad
        sp_ref[0, sl, :] = sp
        cs_ref[0, sl, :] = cs
        bt_ref[0, sl, :] = jnp.where(valid, _sigmoid(r), 0.0)
        return carry

    lax.fori_loop(0, n_chunks, body, 0)


def _gates(raw, bias_vec, alog_vec, pad):
    b, lp, w = raw.shape
    spec = pl.BlockSpec((1, lp, w), lambda i: (i, 0, 0))
    vec = pl.BlockSpec((1, w), lambda i: (0, 0))
    shp = jax.ShapeDtypeStruct((b, lp, w), F32)
    return pl.pallas_call(
        functools.partial(_gates_body, n_chunks=lp // CHUNK, pad=pad),
        grid=(b,),
        in_specs=[spec, vec, vec],
        out_specs=(spec, spec, spec),
        out_shape=(shp, shp, shp),
        compiler_params=_params("arbitrary"),
        name="gates",
    )(raw, bias_vec, alog_vec)


def _shift_matrix():
    r = jnp.arange((CONV_K - 1) * CHUNK)
    src = CONV_HALO + r % CHUNK - (CONV_K - 1 - r // CHUNK)
    return (jnp.arange(CONV_HALO + CHUNK)[None, :] == src[:, None]).astype(BF16)


def _shifted_taps(shift_ref, xpad_ref, taps_ref):
    taps_ref[...] = _dot(shift_ref[...], xpad_ref[...])


def _causal_conv(taps_ref, xpad_ref, w_ref, lo, width, bias):
    cols = slice(lo, lo + width)
    acc = w_ref[CONV_K - 1:CONV_K, cols] * xpad_ref[CONV_HALO:CONV_HALO + CHUNK, cols].astype(F32)
    if bias is not None:
        acc = acc + bias
    for j in range(CONV_K - 1):
        acc = acc + w_ref[j:j + 1, cols] * taps_ref[j * CHUNK:(j + 1) * CHUNK, cols]
    return acc


def _ssd_body(x_ref, bc_ref, cw_ref, cb_ref, sp_ref, cs_ref, csr_ref, dvec_ref, expand_ref, shift_ref,
              y_ref, xpad_ref, taps_ref, st_ref, *, pad):
    c = pl.program_id(1)

    @pl.when(c == 0)
    def _():
        st_ref[...] = jnp.zeros_like(st_ref)
        xpad_ref[0:CONV_HALO, :] = jnp.zeros((CONV_HALO, xpad_ref.shape[1]), BF16)

    xpad_ref[CONV_HALO:CONV_HALO + CHUNK, 0:SSD_D_INNER] = x_ref[0]
    xpad_ref[CONV_HALO:CONV_HALO + CHUNK, SSD_D_INNER:] = bc_ref[0]
    _shifted_taps(shift_ref, xpad_ref, taps_ref)

    row = lax.broadcasted_iota(jnp.int32, (CHUNK, 1), 0)
    valid = (row + c * CHUNK) >= pad
    ri = lax.broadcasted_iota(jnp.int32, (CHUNK, CHUNK), 0)
    ci = lax.broadcasted_iota(jnp.int32, (CHUNK, CHUNK), 1)
    tril = ri >= ci

    sp = sp_ref[0]
    cs = cs_ref[0]
    csr = csr_ref[0, 0]

    def conv(lo, width):
        a = _causal_conv(taps_ref, xpad_ref, cw_ref, lo, width, cb_ref[:, lo:lo + width])
        return jnp.where(valid, _silu(a), 0.0)

    gw = SSD_HPG * SSD_HEAD_DIM
    groups = range(SSD_GROUPS)
    heads = range(SSD_HEADS)
    xg = [conv(g * gw, gw) for g in groups]
    bg = [conv(SSD_D_INNER + g * SSD_STATE, SSD_STATE).astype(BF16) for g in groups]
    cg = [conv(SSD_D_INNER + SSD_BC + g * SSD_STATE, SSD_STATE).astype(BF16) for g in groups]
    cb = [_dot_nt(cg[g], bg[g]) for g in groups]
    st = [st_ref[g] for g in groups]
    yoff = [_dot(cg[g], st[g].astype(BF16)) for g in groups]

    last_row = cs[CHUNK - 1:CHUNK, :]
    factors = (sp, sp * jnp.exp(last_row - cs), jnp.exp(cs))
    pieces = []
    for f in factors:
        hi = f.astype(BF16)
        pieces += [hi, (f - hi.astype(F32)).astype(BF16)]
    spread = _dot(jnp.concatenate(pieces, axis=0), expand_ref[...])
    e_dt, e_w, e_cs = (spread[(2 * k) * CHUNK:(2 * k + 1) * CHUNK] + spread[(2 * k + 1) * CHUNK:(2 * k + 2) * CHUNK]
                       for k in range(len(factors)))

    xdt = [(xg[g] * e_dt[:, g * gw:(g + 1) * gw]).astype(BF16) for g in groups]
    xw = [(xg[g] * e_w[:, g * gw:(g + 1) * gw]).astype(BF16) for g in groups]
    w_intra = []
    for h in heads:
        decay = jnp.where(tril, jnp.exp(cs[:, h:h + 1] - csr[h:h + 1, :]), 0.0)
        w_intra.append((cb[h // SSD_HPG] * decay).astype(BF16))
    ydiag = [_dot(w_intra[h], xdt[h // SSD_HPG][:, (h % SSD_HPG) * SSD_HEAD_DIM:(h % SSD_HPG + 1) * SSD_HEAD_DIM])
             for h in heads]
    snew = [_dot_tn(bg[g], xw[g]) for g in groups]
    for g in groups:
        sl = slice(g * gw, (g + 1) * gw)
        st_ref[g] = st[g] * e_cs[CHUNK - 1:CHUNK, sl] + snew[g]
        y_g = (jnp.concatenate(ydiag[g * SSD_HPG:(g + 1) * SSD_HPG], axis=1) + yoff[g] * e_cs[:, sl]
               + dvec_ref[:, sl] * xg[g])
        y_ref[0, :, sl] = y_g.astype(y_ref.dtype)

    xpad_ref[0:CONV_HALO, :] = xpad_ref[CHUNK:CHUNK + CONV_HALO, :]


def _ssd(proj, conv_w, conv_b, sp, cs, cs_rows, d_vec, seq, pad):
    b, lp, _ = proj.shape
    nc = lp // CHUNK
    conv_dim = SSD_D_INNER + 2 * SSD_BC
    x_blk = 3 * GDN_DQK // SSD_D_INNER
    bc_blk = (3 * GDN_DQK + SSD_D_INNER) // (2 * SSD_BC)
    expand = (jnp.arange(LANES)[:, None] == jnp.arange(SSD_D_INNER)[None, :] // SSD_HEAD_DIM).astype(BF16)
    return pl.pallas_call(
        functools.partial(_ssd_body, pad=pad),
        grid=(b, nc),
        in_specs=[
            pl.BlockSpec((1, CHUNK, SSD_D_INNER), lambda i, c: (i, c, x_blk)),
            pl.BlockSpec((1, CHUNK, 2 * SSD_BC), lambda i, c: (i, c, bc_blk)),
            pl.BlockSpec((CONV_K, conv_dim), lambda i, c: (0, 0)),
            pl.BlockSpec((1, conv_dim), lambda i, c: (0, 0)),
            pl.BlockSpec((1, CHUNK, LANES), lambda i, c: (i, c, 0)),
            pl.BlockSpec((1, CHUNK, LANES), lambda i, c: (i, c, 0)),
            pl.BlockSpec((1, 1, SSD_HEADS, CHUNK), lambda i, c: (i, c, 0, 0)),
            pl.BlockSpec((1, SSD_D_INNER), lambda i, c: (0, 0)),
            pl.BlockSpec((LANES, SSD_D_INNER), lambda i, c: (0, 0)),
            pl.BlockSpec(((CONV_K - 1) * CHUNK, CONV_HALO + CHUNK), lambda i, c: (0, 0)),
        ],
        out_specs=pl.BlockSpec((1, CHUNK, SSD_D_INNER), lambda i, c: (i, jnp.maximum(c - 1, 0), 0)),
        out_shape=jax.ShapeDtypeStruct((b, seq, SSD_D_INNER), BF16),
        scratch_shapes=[
            pltpu.VMEM((CONV_HALO + CHUNK, conv_dim), BF16),
            pltpu.VMEM(((CONV_K - 1) * CHUNK, conv_dim), F32),
            pltpu.VMEM((SSD_GROUPS, SSD_STATE, SSD_HPG * SSD_HEAD_DIM), F32),
        ],
        compiler_params=_params("arbitrary", "arbitrary"),
        name="ssd_scan",
    )(proj, proj, conv_w, conv_b.reshape(1, conv_dim), sp, cs, cs_rows, d_vec, expand, _shift_matrix())


def _unit_lower_inverses(strict_ls, eye):
    ps = [l.astype(BF16) for l in strict_ls]
    ts = [eye - l for l in strict_ls]
    n = 2
    while n < CHUNK:
        p2 = [_dot(p, p).astype(BF16) for p in ps]
        ts = [t + _dot(t.astype(BF16), q) for t, q in zip(ts, p2)]
        ps = p2
        n *= 2
    return ts


def _gdn_body(q_ref, k_ref, v_ref, cw_ref, cs_ref, bt_ref, gr_ref, nw_ref, shift_ref, o_ref,
              xpad_ref, taps_ref, st_ref, *, pad):
    c = pl.program_id(1)

    @pl.when(c == 0)
    def _():
        st_ref[...] = jnp.zeros_like(st_ref)
        xpad_ref[0:CONV_HALO, :] = jnp.zeros((CONV_HALO, xpad_ref.shape[1]), BF16)

    xpad_ref[CONV_HALO:CONV_HALO + CHUNK, 0:GDN_DQK] = q_ref[0]
    xpad_ref[CONV_HALO:CONV_HALO + CHUNK, GDN_DQK:2 * GDN_DQK] = k_ref[0]
    xpad_ref[CONV_HALO:CONV_HALO + CHUNK, 2 * GDN_DQK:] = v_ref[0]
    _shifted_taps(shift_ref, xpad_ref, taps_ref)

    row = lax.broadcasted_iota(jnp.int32, (CHUNK, 1), 0)
    valid = (row + c * CHUNK) >= pad
    ri = lax.broadcasted_iota(jnp.int32, (CHUNK, CHUNK), 0)
    ci = lax.broadcasted_iota(jnp.int32, (CHUNK, CHUNK), 1)
    tril = ri >= ci
    strict = ri > ci
    eye = (ri == ci).astype(F32)

    cs = cs_ref[0]
    bt = bt_ref[0]
    gr = gr_ref[0, 0]
    nw = nw_ref[...]
    g_off = SSD_HEADS
    b_off = SSD_HEADS + GDN_HEADS

    def conv(lo):
        a = _causal_conv(taps_ref, xpad_ref, cw_ref, lo, GDN_DK, None)
        return jnp.where(valid, _silu(a), 0.0)

    heads = range(GDN_HEADS)
    q16, k16, kb16, rhs16, qg, kd16, decay, elast = [], [], [], [], [], [], [], []
    for h in heads:
        q = conv(h * GDN_DK)
        k = conv(GDN_DQK + h * GDN_DK)
        v = conv(2 * GDN_DQK + h * GDN_DV)
        q = q * lax.rsqrt(jnp.sum(q * q, axis=-1, keepdims=True) + L2_EPS) * (GDN_DK ** -0.5)
        k = k * lax.rsqrt(jnp.sum(k * k, axis=-1, keepdims=True) + L2_EPS)
        col = cs[:, g_off + h:g_off + h + 1]
        rowv = gr[h:h + 1, :]
        last = rowv[:, CHUNK - 1:CHUNK]
        beta = bt[:, b_off + h:b_off + h + 1]
        ecol = jnp.exp(col)
        kb = k * beta
        decay.append(jnp.where(tril, jnp.exp(col - rowv), 0.0))
        q16.append(q.astype(BF16))
        k16.append(k.astype(BF16))
        kb16.append(kb.astype(BF16))
        rhs16.append(jnp.concatenate([v * beta, kb * ecol], axis=1).astype(BF16))
        qg.append(q * ecol)
        kd16.append((k * jnp.exp(last - col)).astype(BF16))
        elast.append(jnp.exp(last))
    kk = [_dot_nt(kb16[h], k16[h]) for h in heads]
    qk = [_dot_nt(q16[h], k16[h]) for h in heads]
    tinv = _unit_lower_inverses([jnp.where(strict, kk[h] * decay[h], 0.0) for h in heads], eye)
    uw = [_dot(tinv[h].astype(BF16), rhs16[h]) for h in heads]
    s = [st_ref[h] for h in heads]
    ws = [_dot(jnp.concatenate([uw[h][:, GDN_DV:], qg[h]], axis=0).astype(BF16), s[h].astype(BF16))
          for h in heads]
    v16 = [(uw[h][:, :GDN_DV] - ws[h][:CHUNK]).astype(BF16) for h in heads]
    o_intra = [_dot((qk[h] * decay[h]).astype(BF16), v16[h]) for h in heads]
    s_add = [_dot_tn(kd16[h], v16[h]) for h in heads]
    for h in heads:
        st_ref[h] = s[h] * elast[h] + s_add[h]
        o = ws[h][CHUNK:] + o_intra[h]
        o = o * lax.rsqrt(jnp.mean(o * o, axis=-1, keepdims=True) + NORM_EPS) * nw
        o_ref[0, :, h * GDN_DV:(h + 1) * GDN_DV] = o.astype(o_ref.dtype)

    xpad_ref[0:CONV_HALO, :] = xpad_ref[CHUNK:CHUNK + CONV_HALO, :]


def _gdn(proj, conv_w, cs, bt, g_rows, norm_w, seq, pad):
    b, lp, _ = proj.shape
    nc = lp // CHUNK
    conv_dim = 3 * GDN_DQK
    qkv = lambda blk: pl.BlockSpec((1, CHUNK, GDN_DQK), lambda i, c: (i, c, blk))
    return pl.pallas_call(
        functools.partial(_gdn_body, pad=pad),
        grid=(b, nc),
        in_specs=[
            qkv(0), qkv(1), qkv(2),
            pl.BlockSpec((CONV_K, conv_dim), lambda i, c: (0, 0)),
            pl.BlockSpec((1, CHUNK, LANES), lambda i, c: (i, c, 0)),
            pl.BlockSpec((1, CHUNK, LANES), lambda i, c: (i, c, 0)),
            pl.BlockSpec((1, 1, GDN_HEADS, CHUNK), lambda i, c: (i, c, 0, 0)),
            pl.BlockSpec((1, GDN_DV), lambda i, c: (0, 0)),
            pl.BlockSpec(((CONV_K - 1) * CHUNK, CONV_HALO + CHUNK), lambda i, c: (0, 0)),
        ],
        out_specs=pl.BlockSpec((1, CHUNK, GDN_DQK), lambda i, c: (i, jnp.maximum(c - 1, 0), 0)),
        out_shape=jax.ShapeDtypeStruct((b, seq, GDN_HEADS * GDN_DV), BF16),
        scratch_shapes=[
            pltpu.VMEM((CONV_HALO + CHUNK, conv_dim), BF16),
            pltpu.VMEM(((CONV_K - 1) * CHUNK, conv_dim), F32),
            pltpu.VMEM((GDN_HEADS, GDN_DK, GDN_DV), F32),
        ],
        compiler_params=_params("arbitrary", "arbitrary"),
        name="gdn_scan",
    )(proj, proj, proj, conv_w, cs, bt, g_rows, norm_w.reshape(1, GDN_DV), _shift_matrix())


def _merge_body(y_ref, z_ref, o_ref, gate_ref, g1_ref, g2_ref, nw_ref, w1_ref, w2_ref, m_ref,
                a1_ref, a2_ref):
    @pl.when(pl.program_id(1) == 0)
    def _():
        gw = SSD_D_INNER // SSD_GROUPS
        for g in range(SSD_GROUPS):
            sl = slice(g * gw, (g + 1) * gw)
            t = y_ref[:, sl].astype(F32) * _silu(z_ref[:, sl].astype(F32))
            ms = jnp.mean(t * t, axis=-1, keepdims=True)
            a1_ref[:, sl] = (t * lax.rsqrt(ms + NORM_EPS) * nw_ref[:, sl]).astype(BF16)
        a2_ref[...] = (o_ref[...].astype(F32) * _silu(gate_ref[...].astype(F32))).astype(BF16)

    m = (_sigmoid(g1_ref[...].astype(F32)) * _dot(a1_ref[...], w1_ref[...])
         + _sigmoid(g2_ref[...].astype(F32)) * _dot(a2_ref[...], w2_ref[...]))
    m_ref[...] = m.astype(m_ref.dtype)


def _merge(y, o, proj_tok, norm_w, w1, w2, tm_pref=256, tn_pref=2048):
    m, d = y.shape
    n = w1.shape[1]
    tm = _divisor_tile(m, tm_pref, 16)
    tn = _divisor_tile(n, tn_pref, LANES)
    nj = n // tn
    row_d = lambda blk: pl.BlockSpec((tm, d), lambda i, j: (i, blk))
    return pl.pallas_call(
        _merge_body,
        grid=(m // tm, nj),
        in_specs=[
            pl.BlockSpec((tm, d), lambda i, j: (i, 0)),
            row_d(0),
            pl.BlockSpec((tm, d), lambda i, j: (i, 0)),
            row_d(1),
            pl.BlockSpec((tm, tn), lambda i, j: (i, 2 * nj + j)),
            pl.BlockSpec((tm, tn), lambda i, j: (i, 3 * nj + j)),
            pl.BlockSpec((1, d), lambda i, j: (0, 0)),
            pl.BlockSpec((d, tn), lambda i, j: (0, j)),
            pl.BlockSpec((d, tn), lambda i, j: (0, j)),
        ],
        out_specs=pl.BlockSpec((tm, tn), lambda i, j: (i, j)),
        out_shape=jax.ShapeDtypeStruct((m, n), BF16),
        scratch_shapes=[pltpu.VMEM((tm, d), BF16), pltpu.VMEM((tm, d), BF16)],
        compiler_params=_params("arbitrary", "arbitrary"),
        name="merge",
    )(y, proj_tok, o, proj_tok, proj_tok, proj_tok, norm_w.reshape(1, d), w1, w2)


def _matmul_residual_body(a_ref, w_ref, r_ref, o_ref):
    o_ref[...] = r_ref[...] + _dot(a_ref[...], w_ref[...])


def _matmul_residual(a, w, res, tm_pref=512, tn_pref=2048):
    m, k = a.shape
    n = w.shape[1]
    tm = _divisor_tile(m, tm_pref, 8)
    tn = _divisor_tile(n, tn_pref, LANES)
    return pl.pallas_call(
        _matmul_residual_body,
        grid=(m // tm, n // tn),
        in_specs=[
            pl.BlockSpec((tm, k), lambda i, j: (i, 0)),
            pl.BlockSpec((k, tn), lambda i, j: (0, j)),
            pl.BlockSpec((tm, tn), lambda i, j: (i, j)),
        ],
        out_specs=pl.BlockSpec((tm, tn), lambda i, j: (i, j)),
        out_shape=jax.ShapeDtypeStruct((m, n), F32),
        compiler_params=_params("arbitrary", "arbitrary"),
        name="out_proj_residual",
    )(a, w, res)


def _top_values(x, count):
    out = []
    for _ in range(count):
        m = jnp.max(x, axis=0, keepdims=True)
        out.append(m)
        x = jnp.where(x == m, -jnp.inf, x)
    return out


def _peer_route_body(q_ref, keys_ref, e2_ref, thr_ref, f1_ref):
    k1 = keys_ref[0].astype(BF16)
    k2 = keys_ref[1].astype(BF16)
    kk = PEER_TOPK
    for h in range(PEER_HEADS):
        lo = 2 * h * PEER_D_KEY
        q1 = q_ref[:, lo:lo + PEER_D_KEY]
        q2 = q_ref[:, lo + PEER_D_KEY:lo + 2 * PEER_D_KEY]
        s1 = _dot_nt(k1, q1)
        s2 = _dot_nt(k2, q2)
        t1 = _top_values(s1, kk + 1)
        t2 = _top_values(s2, kk + 1)
        t1_blk = jnp.concatenate(t1[:kk], axis=0)
        t2_blk = jnp.concatenate(t2[:kk], axis=0)
        half = kk // 2
        cand = jnp.concatenate(
            [t1[0] + t2_blk]
            + [t1[a] + t2_blk[:half] for a in range(1, half)]
            + [t1_blk[half:] + t2[0]], axis=0)
        best = _top_values(cand, kk + 1)
        runner_up = jnp.maximum(best[kk], jnp.maximum(t1[kk] + t2[0], t1[0] + t2[kk]))
        tau = 0.5 * (best[kk - 1] + runner_up)
        m1, m2 = t1[0], t2[0]
        z = jnp.sum(jnp.where(cand >= tau, jnp.exp(cand - (m1 + m2)), 0.0), axis=0, keepdims=True)
        e2 = jnp.exp(s2 - m2)
        for s in range(e2_ref.shape[1]):
            e2_ref[h, s] = e2[:, s * LANES:(s + 1) * LANES]
        thr_ref[h] = jnp.exp(tau - s1 - m2)
        f1_ref[h] = jnp.exp(s1 - m1) / z


def _peer_route(q, keys, t_tok):
    ntok = q.shape[0]
    spec = pl.BlockSpec((PEER_HEADS, PEER_N_KEYS, t_tok), lambda t: (0, 0, t))
    shp = jax.ShapeDtypeStruct((PEER_HEADS, PEER_N_KEYS, ntok), F32)
    strip_spec = pl.BlockSpec((PEER_HEADS, t_tok // LANES, PEER_N_KEYS, LANES), lambda t: (0, t, 0, 0))
    strip_shp = jax.ShapeDtypeStruct((PEER_HEADS, ntok // LANES, PEER_N_KEYS, LANES), F32)
    return pl.pallas_call(
        _peer_route_body,
        grid=(ntok // t_tok,),
        in_specs=[
            pl.BlockSpec((t_tok, q.shape[1]), lambda t: (t, 0)),
            pl.BlockSpec(keys.shape, lambda t: (0, 0, 0)),
        ],
        out_specs=(strip_spec, spec, spec),
        out_shape=(strip_shp, shp, shp),
        compiler_params=_params("arbitrary"),
        name="peer_route",
    )(q, keys)


def _peer_expert_body(hnt_ref, e2_ref, thr_ref, f1_ref, down_ref, upt_ref, acc_ref,
                      w_ref, act_ref, p_ref):
    d, t_tok = hnt_ref.shape
    te = down_ref.shape[0]
    n_strip = t_tok // LANES
    n_part = min(PEER_MATMUL_PARTS, n_strip)
    rows_a = te // n_part
    rows_o = d // n_part
    e = pl.program_id(1)
    last = pl.num_programs(1) - 1
    slot = lax.rem(e, 2)

    def run(build, project):
        for part in range(n_part):
            if build:
                for s in range(part * n_strip // n_part, (part + 1) * n_strip // n_part):
                    ln = slice(s * LANES, (s + 1) * LANES)
                    for i in range(te // PEER_N_KEYS):
                        w = None
                        for hd in range(PEER_HEADS):
                            thr = thr_ref[hd, i:i + 1, ln]
                            f1 = f1_ref[hd, i:i + 1, ln]
                            e2 = e2_ref[hd, s]
                            term = f1 * jnp.where(e2 >= thr, e2, 0.0)
                            w = term if w is None else w + term
                        w_ref[i * PEER_N_KEYS:(i + 1) * PEER_N_KEYS, ln] = w
                ra = slice(part * rows_a, (part + 1) * rows_a)
                a = _dot(down_ref[ra, :], hnt_ref[...])
                act_ref[ra, :] = 0.5 * a * (1.0 + lax.erf(a * (2.0 ** -0.5)))
            if project:
                ro = slice(part * rows_o, (part + 1) * rows_o)
                acc_ref[ro, :] += _dot(upt_ref[ro, :], p_ref[1 - slot])
        if build:
            p_ref[slot] = (w_ref[...] * act_ref[...]).astype(BF16)

    @pl.when(e == 0)
    def _():
        acc_ref[...] = jnp.zeros_like(acc_ref)
        run(build=True, project=False)

    @pl.when(jnp.logical_and(e > 0, e < last))
    def _():
        run(build=True, project=True)

    @pl.when(e == last)
    def _():
        run(build=False, project=True)


def _peer_experts(hn_t, e2, thr, f1, down, up_t, t_tok, keys_per_block):
    d, ntok = hn_t.shape
    n_exp = down.shape[0]
    te = keys_per_block * PEER_N_KEYS
    n_blk = n_exp // te
    cur = lambda e: jnp.minimum(e, n_blk - 1)
    prev = lambda e: jnp.maximum(e - 1, 0)
    full = pl.BlockSpec((PEER_HEADS, t_tok // LANES, PEER_N_KEYS, LANES), lambda t, e: (0, t, 0, 0))
    part = pl.BlockSpec((PEER_HEADS, keys_per_block, t_tok), lambda t, e: (0, cur(e), t))
    return pl.pallas_call(
        _peer_expert_body,
        grid=(ntok // t_tok, n_blk + 1),
        in_specs=[
            pl.BlockSpec((d, t_tok), lambda t, e: (0, t)),
            full, part, part,
            pl.BlockSpec((te, d), lambda t, e: (cur(e), 0)),
            pl.BlockSpec((None, d, te), lambda t, e: (prev(e), 0, 0)),
        ],
        out_specs=pl.BlockSpec((d, t_tok), lambda t, e: (0, t)),
        out_shape=jax.ShapeDtypeStruct((d, ntok), F32),
        scratch_shapes=[
            pltpu.VMEM((te, t_tok), F32),
            pltpu.VMEM((te, t_tok), F32),
            pltpu.VMEM((2, te, t_tok), BF16),
        ],
        compiler_params=_params("arbitrary", "arbitrary"),
        name="peer_experts",
    )(hn_t, e2, thr, f1, down, up_t)


def _residual_norm_body(h_ref, acc_ref, g_ref, o_ref):
    y = h_ref[...] + acc_ref[...].T
    ms = jnp.mean(y * y, axis=-1, keepdims=True)
    o_ref[...] = y * lax.rsqrt(ms + NORM_EPS) * g_ref[...]


def _residual_norm(h2, acc_t, gain, tm_pref=256):
    m, d = h2.shape
    tm = _divisor_tile(m, tm_pref, LANES)
    return pl.pallas_call(
        _residual_norm_body,
        grid=(m // tm,),
        in_specs=[
            pl.BlockSpec((tm, d), lambda i: (i, 0)),
            pl.BlockSpec((d, tm), lambda i: (0, i)),
            pl.BlockSpec((1, d), lambda i: (0, 0)),
        ],
        out_specs=pl.BlockSpec((tm, d), lambda i: (i, 0)),
        out_shape=jax.ShapeDtypeStruct((m, d), F32),
        compiler_params=_params("arbitrary"),
        name="residual_norm",
    )(h2, acc_t, gain.reshape(1, d))


def _cast_body(x_ref, o_ref, *, transpose):
    x = x_ref[...]
    o_ref[...] = (x.T if transpose else x).astype(o_ref.dtype)


def _cast_bf16(w, rows_pref, transpose=False, row_multiple=LANES, group=None):
    r, c = w.shape
    tr = _divisor_tile(r, rows_pref, row_multiple)
    if not transpose:
        out_spec, out_dims = pl.BlockSpec((tr, c), lambda i: (i, 0)), (r, c)
    elif group is None:
        out_spec, out_dims = pl.BlockSpec((c, tr), lambda i: (0, i)), (c, r)
    else:
        per = group // tr
        out_spec = pl.BlockSpec((None, c, tr), lambda i: (i // per, 0, i % per))
        out_dims = (r // group, c, group)
    return pl.pallas_call(
        functools.partial(_cast_body, transpose=transpose),
        grid=(r // tr,),
        in_specs=[pl.BlockSpec((tr, c), lambda i: (i, 0))],
        out_specs=out_spec,
        out_shape=jax.ShapeDtypeStruct(out_dims, BF16),
        compiler_params=_params("arbitrary"),
        name="cast_bf16",
    )(w)


def _w_in_segments(d):
    widths = (("z", SSD_D_INNER), ("xbc", SSD_D_INNER + 2 * SSD_BC), ("dt", SSD_HEADS),
              ("qkv", 3 * GDN_DQK), ("gate", GDN_HEADS * GDN_DV), ("a", GDN_HEADS),
              ("beta", GDN_HEADS), ("merge", 2 * d))
    seg, off = {}, 0
    for name, wd in widths:
        seg[name] = slice(off, off + wd)
        off += wd
    return seg


def _row_segments(seg, names):
    return tuple((seg[n].start, seg[n].stop - seg[n].start) for n in names)


def _lane_vec(*parts):
    v = jnp.concatenate([p.astype(F32) for p in parts])
    return jnp.pad(v, (0, LANES - v.shape[0])).reshape(1, LANES)


def _mixer(x, meta_tokens, norm_w, w_in, ssd_conv_w, ssd_conv_b, ssd_dt_bias, ssd_A_log, ssd_D,
           ssd_norm_w, gdn_conv_w, gdn_dt_bias, gdn_A_log, gdn_norm_w, w_branch_ssd, w_branch_gdn, w_out):
    b, seq, d = x.shape
    pad = CHUNK - N_META
    lp = seq + CHUNK
    nc = lp // CHUNK
    seg = _w_in_segments(d)
    w16_t = _cast_bf16(jnp.swapaxes(w_in, 0, 1), rows_pref=512, row_multiple=64)
    small = jnp.concatenate([w16_t[seg[n]] for n in ("dt", "a", "beta")], axis=0)
    w_small = jnp.pad(small, ((0, LANES - small.shape[0]), (0, 0)))

    meta = jnp.broadcast_to(meta_tokens[None].astype(x.dtype), (b, N_META, d))
    hp = jnp.concatenate([jnp.zeros((b, pad, d), x.dtype), meta, x], axis=1).reshape(b * lp, d)
    x2d = x.reshape(b * seq, d)

    proj_scan = _norm_matmul(hp, norm_w, w16_t, tm_pref=1056,
                             w_row_segments=_row_segments(seg, ("qkv", "xbc"))).reshape(b, lp, -1)
    raw_small = _norm_matmul(hp, norm_w, w_small, tm_pref=1056, out_dtype=F32,
                             w_row_segments=((0, LANES),)).reshape(b, lp, LANES)
    proj_tok = _norm_matmul(x2d, norm_w, w16_t, tm_pref=1024,
                            w_row_segments=_row_segments(seg, ("z", "gate", "merge")))

    zeros_h = jnp.zeros((GDN_HEADS,), F32)
    bias_vec = _lane_vec(ssd_dt_bias, gdn_dt_bias, zeros_h)
    alog_vec = _lane_vec(ssd_A_log, gdn_A_log, zeros_h)
    sp, cs, bt = _gates(raw_small, bias_vec, alog_vec, pad)
    cs_c = cs.reshape(b, nc, CHUNK, LANES)
    ssd_rows = jnp.swapaxes(cs_c[..., :SSD_HEADS], 2, 3)
    gdn_rows = jnp.swapaxes(cs_c[..., SSD_HEADS:SSD_HEADS + GDN_HEADS], 2, 3)

    d_vec = jnp.repeat(ssd_D.astype(F32), SSD_HEAD_DIM).reshape(1, SSD_D_INNER)
    y = _ssd(proj_scan, ssd_conv_w, ssd_conv_b, sp, cs, ssd_rows, d_vec, seq, pad)
    o = _gdn(proj_scan, gdn_conv_w, cs, bt, gdn_rows, gdn_norm_w, seq, pad)

    m = _merge(y.reshape(b * seq, -1), o.reshape(b * seq, -1), proj_tok, ssd_norm_w,
               w_branch_ssd.astype(BF16), w_branch_gdn.astype(BF16))
    return _matmul_residual(m, w_out.astype(BF16), x2d)


def _peer(h2, norm_w, w_query, sub_keys, expert_down, expert_up, final_w):
    ntok = h2.shape[0]
    q, hn_t = _norm_matmul(h2, norm_w, w_query.astype(BF16), tm_pref=512, tn_pref=2048,
                           with_transposed=True)
    e2, thr, f1 = _peer_route(q, sub_keys, _divisor_tile(ntok, 256, LANES))
    keys_per_block = 8
    up_t = _cast_bf16(expert_up, rows_pref=512, transpose=True, group=keys_per_block * PEER_N_KEYS)
    acc_t = _peer_experts(hn_t, e2, thr, f1, _cast_bf16(expert_down, rows_pref=1024), up_t,
                          _divisor_tile(ntok, 512, LANES), keys_per_block)
    return _residual_norm(h2, acc_t, final_w)


def kernel(x, meta_tokens, norm_mix_w, w_in, ssd_conv_w, ssd_conv_b, ssd_dt_bias, ssd_A_log, ssd_D, ssd_norm_w, gdn_conv_w, gdn_dt_bias, gdn_A_log, gdn_norm_w, w_branch_ssd, w_branch_gdn, w_out, norm_ffn_w, peer_w_query, peer_sub_keys, peer_expert_down, peer_expert_up, final_norm_w):
    b, seq, d = x.shape
    assert norm_mix_w.shape[0] == 1, "single-layer trunk"
    h2 = _mixer(x, meta_tokens, norm_mix_w[0], w_in[0], ssd_conv_w[0], ssd_conv_b[0], ssd_dt_bias[0],
                ssd_A_log[0], ssd_D[0], ssd_norm_w[0], gdn_conv_w[0], gdn_dt_bias[0], gdn_A_log[0],
                gdn_norm_w[0], w_branch_ssd[0], w_branch_gdn[0], w_out[0])
    out = _peer(h2, norm_ffn_w[0], peer_w_query[0], peer_sub_keys[0], peer_expert_down[0],
                peer_expert_up[0], final_norm_w)
    return out.reshape(b, seq, d)
```

```python
import functools

import jax
import jax.numpy as jnp
from jax import lax
from jax.experimental import pallas as pl
from jax.experimental.pallas import tpu as pltpu

F32 = jnp.float32
BF16 = jnp.bfloat16

N_META = 16
CHUNK = 64
CONV_K = 4
NORM_EPS = 1e-6
L2_EPS = 1e-6

SSD_HEADS = 32
SSD_HEAD_DIM = 64
SSD_GROUPS = 4
SSD_STATE = 128
SSD_HPG = SSD_HEADS // SSD_GROUPS
SSD_D_INNER = SSD_HEADS * SSD_HEAD_DIM
SSD_BC = SSD_GROUPS * SSD_STATE

GDN_HEADS = 16
GDN_DK = 128
GDN_DV = 128
GDN_DQK = GDN_HEADS * GDN_DK

PEER_HEADS = 8
PEER_N_KEYS = 128
PEER_TOPK = 16
PEER_D_KEY = 128
PEER_MATMUL_PARTS = 2

CONV_HALO = 16
LANES = 128
VMEM_LIMIT = 56 * 1024 * 1024

_NT = (((1,), (1,)), ((), ()))
_TN = (((0,), (0,)), ((), ()))


def _dot(a, b):
    return jnp.dot(a, b, preferred_element_type=F32)


def _dot_nt(a, b):
    return lax.dot_general(a, b, _NT, preferred_element_type=F32)


def _dot_tn(a, b):
    return lax.dot_general(a, b, _TN, preferred_element_type=F32)


def _silu(x):
    return x / (1.0 + jnp.exp(-x))


def _sigmoid(x):
    return 1.0 / (1.0 + jnp.exp(-x))


def _softplus(x):
    return jnp.maximum(x, 0.0) + jnp.log(1.0 + jnp.exp(-jnp.abs(x)))


def _divisor_tile(n, preferred, multiple):
    t = min(preferred, n)
    t -= t % multiple
    while t >= multiple:
        if n % t == 0:
            return t
        t -= multiple
    return n


def _params(*sem):
    return pltpu.CompilerParams(dimension_semantics=sem, vmem_limit_bytes=VMEM_LIMIT)


def _norm_matmul_body(x_ref, g_ref, w_ref, o_ref, *rest, w_is_transposed):
    n_ref = rest[-1]
    nt_ref = rest[0] if len(rest) == 2 else None

    @pl.when(pl.program_id(1) == 0)
    def _():
        x = x_ref[...]
        ms = jnp.mean(x * x, axis=-1, keepdims=True)
        n = x * lax.rsqrt(ms + NORM_EPS) * g_ref[...]
        n_ref[...] = n.astype(BF16)
        if nt_ref is not None:
            nt_ref[...] = n.T.astype(BF16)

    dot = _dot_nt if w_is_transposed else _dot
    o_ref[...] = dot(n_ref[...], w_ref[...]).astype(o_ref.dtype)


def _segment_row(j, segments, tn):
    unit = 16
    row, first_blk = None, 0
    for start, length in segments:
        assert start % unit == 0 and tn % unit == 0
        cand = start // unit + (j - first_blk) * (tn // unit)
        row = cand if row is None else jnp.where(j >= first_blk, cand, row)
        first_blk += length // tn
    return row * unit


def _norm_matmul(x2d, gain, w, tm_pref=768, tn_pref=1024, out_dtype=BF16, with_transposed=False,
                 w_row_segments=None):
    m, k = x2d.shape
    w_is_transposed = w_row_segments is not None
    n = sum(ln for _, ln in w_row_segments) if w_is_transposed else w.shape[1]
    tm = _divisor_tile(m, tm_pref, LANES if with_transposed else 16)
    tn = _divisor_tile(n, tn_pref, LANES)
    if w_is_transposed:
        assert all(ln % tn == 0 for _, ln in w_row_segments)
        w_spec = pl.BlockSpec((pl.Element(tn), pl.Element(k)),
                              lambda i, j: (_segment_row(j, w_row_segments, tn), 0))
    else:
        w_spec = pl.BlockSpec((k, tn), lambda i, j: (0, j))
    out_specs = pl.BlockSpec((tm, tn), lambda i, j: (i, j))
    out_shape = jax.ShapeDtypeStruct((m, n), out_dtype)
    if with_transposed:
        out_specs = (out_specs, pl.BlockSpec((k, tm), lambda i, j: (0, i)))
        out_shape = (out_shape, jax.ShapeDtypeStruct((k, m), BF16))
    return pl.pallas_call(
        functools.partial(_norm_matmul_body, w_is_transposed=w_is_transposed),
        grid=(m // tm, n // tn),
        in_specs=[
            pl.BlockSpec((tm, k), lambda i, j: (i, 0)),
            pl.BlockSpec((1, k), lambda i, j: (0, 0)),
            w_spec,
        ],
        out_specs=out_specs,
        out_shape=out_shape,
        scratch_shapes=[pltpu.VMEM((tm, k), BF16)],
        compiler_params=_params("arbitrary", "arbitrary"),
        name="norm_matmul",
    )(x2d, gain.reshape(1, k), w)


def _gates_body(raw_ref, bias_ref, alog_ref, sp_ref, cs_ref, bt_ref, *, n_chunks, pad):
    ri = lax.broadcasted_iota(jnp.int32, (CHUNK, CHUNK), 0)
    ci = lax.broadcasted_iota(jnp.int32, (CHUNK, CHUNK), 1)
    tri = (ri >= ci).astype(F32)
    neg_a = -jnp.exp(alog_ref[...])
    bias = bias_ref[...]
    row = lax.broadcasted_iota(jnp.int32, (CHUNK, 1), 0)

    def body(c, carry):
        sl = pl.ds(pl.multiple_of(c * CHUNK, CHUNK), CHUNK)
        r = raw_ref[0, sl, :]
        sp = _softplus(r + bias)
        cs = jnp.dot(tri, sp * neg_a, precision=lax.Precision.HIGHEST, preferred_element_type=F32)
        valid = (row + c * CHUNK) >= pad
        sp_ref[0, sl, :] = sp
        cs_ref[0, sl, :] = cs
        bt_ref[0, sl, :] = jnp.where(valid, _sigmoid(r), 0.0)
        return carry

    lax.fori_loop(0, n_chunks, body, 0)


def _gates(raw, bias_vec, alog_vec, pad):
    b, lp, w = raw.shape
    spec = pl.BlockSpec((1, lp, w), lambda i: (i, 0, 0))
    vec = pl.BlockSpec((1, w), lambda i: (0, 0))
    shp = jax.ShapeDtypeStruct((b, lp, w), F32)
    return pl.pallas_call(
        functools.partial(_gates_body, n_chunks=lp // CHUNK, pad=pad),
        grid=(b,),
        in_specs=[spec, vec, vec],
        out_specs=(spec, spec, spec),
        out_shape=(shp, shp, shp),
        compiler_params=_params("arbitrary"),
        name="gates",
    )(raw, bias_vec, alog_vec)


def _shift_matrix():
    r = jnp.arange((CONV_K - 1) * CHUNK)
    src = CONV_HALO + r % CHUNK - (CONV_K - 1 - r // CHUNK)
    return (jnp.arange(CONV_HALO + CHUNK)[None, :] == src[:, None]).astype(BF16)


def _shifted_taps(shift_ref, xpad_ref, taps_ref):
    taps_ref[...] = _dot(shift_ref[...], xpad_ref[...])


def _causal_conv(taps_ref, xpad_ref, w_ref, lo, width, bias):
    cols = slice(lo, lo + width)
    acc = w_ref[CONV_K - 1:CONV_K, cols] * xpad_ref[CONV_HALO:CONV_HALO + CHUNK, cols].astype(F32)
    if bias is not None:
        acc = acc + bias
    for j in range(CONV_K - 1):
        acc = acc + w_ref[j:j + 1, cols] * taps_ref[j * CHUNK:(j + 1) * CHUNK, cols]
    return acc


def _ssd_body(x_ref, bc_ref, cw_ref, cb_ref, sp_ref, cs_ref, csr_ref, dvec_ref, expand_ref, shift_ref,
              y_ref, xpad_ref, taps_ref, st_ref, *, pad):
    c = pl.program_id(1)

    @pl.when(c == 0)
    def _():
        st_ref[...] = jnp.zeros_like(st_ref)
        xpad_ref[0:CONV_HALO, :] = jnp.zeros((CONV_HALO, xpad_ref.shape[1]), BF16)

    xpad_ref[CONV_HALO:CONV_HALO + CHUNK, 0:SSD_D_INNER] = x_ref[0]
    xpad_ref[CONV_HALO:CONV_HALO + CHUNK, SSD_D_INNER:] = bc_ref[0]
    _shifted_taps(shift_ref, xpad_ref, taps_ref)

    row = lax.broadcasted_iota(jnp.int32, (CHUNK, 1), 0)
    valid = (row + c * CHUNK) >= pad
    ri = lax.broadcasted_iota(jnp.int32, (CHUNK, CHUNK), 0)
    ci = lax.broadcasted_iota(jnp.int32, (CHUNK, CHUNK), 1)
    tril = ri >= ci

    sp = sp_ref[0]
    cs = cs_ref[0]
    csr = csr_ref[0, 0]

    def conv(lo, width):
        a = _causal_conv(taps_ref, xpad_ref, cw_ref, lo, width, cb_ref[:, lo:lo + width])
        return jnp.where(valid, _silu(a), 0.0)

    gw = SSD_HPG * SSD_HEAD_DIM
    groups = range(SSD_GROUPS)
    heads = range(SSD_HEADS)
    xg = [conv(g * gw, gw) for g in groups]
    bg = [conv(SSD_D_INNER + g * SSD_STATE, SSD_STATE).astype(BF16) for g in groups]
    cg = [conv(SSD_D_INNER + SSD_BC + g * SSD_STATE, SSD_STATE).astype(BF16) for g in groups]
    cb = [_dot_nt(cg[g], bg[g]) for g in groups]
    st = [st_ref[g] for g in groups]
    yoff = [_dot(cg[g], st[g].astype(BF16)) for g in groups]

    last_row = cs[CHUNK - 1:CHUNK, :]
    factors = (sp, sp * jnp.exp(last_row - cs), jnp.exp(cs))
    pieces = []
    for f in factors:
        hi = f.astype(BF16)
        pieces += [hi, (f - hi.astype(F32)).astype(BF16)]
    spread = _dot(jnp.concatenate(pieces, axis=0), expand_ref[...])
    e_dt, e_w, e_cs = (spread[(2 * k) * CHUNK:(2 * k + 1) * CHUNK] + spread[(2 * k + 1) * CHUNK:(2 * k + 2) * CHUNK]
                       for k in range(len(factors)))

    xdt = [(xg[g] * e_dt[:, g * gw:(g + 1) * gw]).astype(BF16) for g in groups]
    xw = [(xg[g] * e_w[:, g * gw:(g + 1) * gw]).astype(BF16) for g in groups]
    w_intra = []
    for h in heads:
        decay = jnp.where(tril, jnp.exp(cs[:, h:h + 1] - csr[h:h + 1, :]), 0.0)
        w_intra.append((cb[h // SSD_HPG] * decay).astype(BF16))
    ydiag = [_dot(w_intra[h], xdt[h // SSD_HPG][:, (h % SSD_HPG) * SSD_HEAD_DIM:(h % SSD_HPG + 1) * SSD_HEAD_DIM])
             for h in heads]
    snew = [_dot_tn(bg[g], xw[g]) for g in groups]
    for g in groups:
        sl = slice(g * gw, (g + 1) * gw)
        st_ref[g] = st[g] * e_cs[CHUNK - 1:CHUNK, sl] + snew[g]
        y_g = (jnp.concatenate(ydiag[g * SSD_HPG:(g + 1) * SSD_HPG], axis=1) + yoff[g] * e_cs[:, sl]
               + dvec_ref[:, sl] * xg[g])
        y_ref[0, :, sl] = y_g.astype(y_ref.dtype)

    xpad_ref[0:CONV_HALO, :] = xpad_ref[CHUNK:CHUNK + CONV_HALO, :]


def _ssd(proj, conv_w, conv_b, sp, cs, cs_rows, d_vec, seq, pad):
    b, lp, _ = proj.shape
    nc = lp // CHUNK
    conv_dim = SSD_D_INNER + 2 * SSD_BC
    x_blk = 3 * GDN_DQK // SSD_D_INNER
    bc_blk = (3 * GDN_DQK + SSD_D_INNER) // (2 * SSD_BC)
    expand = (jnp.arange(LANES)[:, None] == jnp.arange(SSD_D_INNER)[None, :] // SSD_HEAD_DIM).astype(BF16)
    return pl.pallas_call(
        functools.partial(_ssd_body, pad=pad),
        grid=(b, nc),
        in_specs=[
            pl.BlockSpec((1, CHUNK, SSD_D_INNER), lambda i, c: (i, c, x_blk)),
            pl.BlockSpec((1, CHUNK, 2 * SSD_BC), lambda i, c: (i, c, bc_blk)),
            pl.BlockSpec((CONV_K, conv_dim), lambda i, c: (0, 0)),
            pl.BlockSpec((1, conv_dim), lambda i, c: (0, 0)),
            pl.BlockSpec((1, CHUNK, LANES), lambda i, c: (i, c, 0)),
            pl.BlockSpec((1, CHUNK, LANES), lambda i, c: (i, c, 0)),
            pl.BlockSpec((1, 1, SSD_HEADS, CHUNK), lambda i, c: (i, c, 0, 0)),
            pl.BlockSpec((1, SSD_D_INNER), lambda i, c: (0, 0)),
            pl.BlockSpec((LANES, SSD_D_INNER), lambda i, c: (0, 0)),
            pl.BlockSpec(((CONV_K - 1) * CHUNK, CONV_HALO + CHUNK), lambda i, c: (0, 0)),
        ],
        out_specs=pl.BlockSpec((1, CHUNK, SSD_D_INNER), lambda i, c: (i, jnp.maximum(c - 1, 0), 0)),
        out_shape=jax.ShapeDtypeStruct((b, seq, SSD_D_INNER), BF16),
        scratch_shapes=[
            pltpu.VMEM((CONV_HALO + CHUNK, conv_dim), BF16),
            pltpu.VMEM(((CONV_K - 1) * CHUNK, conv_dim), F32),
            pltpu.VMEM((SSD_GROUPS, SSD_STATE, SSD_HPG * SSD_HEAD_DIM), F32),
        ],
        compiler_params=_params("arbitrary", "arbitrary"),
        name="ssd_scan",
    )(proj, proj, conv_w, conv_b.reshape(1, conv_dim), sp, cs, cs_rows, d_vec, expand, _shift_matrix())


def _unit_lower_inverses(strict_ls, eye):
    ps = [l.astype(BF16) for l in strict_ls]
    ts = [eye - l for l in strict_ls]
    n = 2
    while n < CHUNK:
        p2 = [_dot(p, p).astype(BF16) for p in ps]
        ts = [t + _dot(t.astype(BF16), q) for t, q in zip(ts, p2)]
        ps = p2
        n *= 2
    return ts


def _gdn_body(q_ref, k_ref, v_ref, cw_ref, cs_ref, bt_ref, gr_ref, nw_ref, shift_ref, o_ref,
              xpad_ref, taps_ref, st_ref, *, pad):
    c = pl.program_id(1)
    n_seq = q_ref.shape[0]

    @pl.when(c == 0)
    def _():
        st_ref[...] = jnp.zeros_like(st_ref)
        xpad_ref[:, 0:CONV_HALO, :] = jnp.zeros((n_seq, CONV_HALO, xpad_ref.shape[2]), BF16)

    for bi in range(n_seq):
        xpad_ref[bi, CONV_HALO:CONV_HALO + CHUNK, 0:GDN_DQK] = q_ref[bi]
        xpad_ref[bi, CONV_HALO:CONV_HALO + CHUNK, GDN_DQK:2 * GDN_DQK] = k_ref[bi]
        xpad_ref[bi, CONV_HALO:CONV_HALO + CHUNK, 2 * GDN_DQK:] = v_ref[bi]
        _shifted_taps(shift_ref, xpad_ref.at[bi], taps_ref.at[bi])

    row = lax.broadcasted_iota(jnp.int32, (CHUNK, 1), 0)
    valid = (row + c * CHUNK) >= pad
    ri = lax.broadcasted_iota(jnp.int32, (CHUNK, CHUNK), 0)
    ci = lax.broadcasted_iota(jnp.int32, (CHUNK, CHUNK), 1)
    tril = ri >= ci
    strict = ri > ci
    eye = (ri == ci).astype(F32)

    nw = nw_ref[...]
    g_off = SSD_HEADS
    b_off = SSD_HEADS + GDN_HEADS

    def conv(bi, lo):
        a = _causal_conv(taps_ref.at[bi], xpad_ref.at[bi], cw_ref, lo, GDN_DK, None)
        return jnp.where(valid, _silu(a), 0.0)

    units = [(bi, h) for bi in range(n_seq) for h in range(GDN_HEADS)]
    heads = range(len(units))
    q16, k16, kb16, rhs16, qg, kd16, decay, elast = [], [], [], [], [], [], [], []
    for bi, h in units:
        cs = cs_ref[bi]
        bt = bt_ref[bi]
        gr = gr_ref[bi, 0]
        q = conv(bi, h * GDN_DK)
        k = conv(bi, GDN_DQK + h * GDN_DK)
        v = conv(bi, 2 * GDN_DQK + h * GDN_DV)
        q = q * lax.rsqrt(jnp.sum(q * q, axis=-1, keepdims=True) + L2_EPS) * (GDN_DK ** -0.5)
        k = k * lax.rsqrt(jnp.sum(k * k, axis=-1, keepdims=True) + L2_EPS)
        col = cs[:, g_off + h:g_off + h + 1]
        rowv = gr[h:h + 1, :]
        last = rowv[:, CHUNK - 1:CHUNK]
        beta = bt[:, b_off + h:b_off + h + 1]
        ecol = jnp.exp(col)
        kb = k * beta
        decay.append(jnp.where(tril, jnp.exp(col - rowv), 0.0))
        q16.append(q.astype(BF16))
        k16.append(k.astype(BF16))
        kb16.append(kb.astype(BF16))
        rhs16.append(jnp.concatenate([v * beta, kb * ecol], axis=1).astype(BF16))
        qg.append(q * ecol)
        kd16.append((k * jnp.exp(last - col)).astype(BF16))
        elast.append(jnp.exp(last))
    kk = [_dot_nt(kb16[h], k16[h]) for h in heads]
    qk = [_dot_nt(q16[h], k16[h]) for h in heads]
    tinv = _unit_lower_inverses([jnp.where(strict, kk[h] * decay[h], 0.0) for h in heads], eye)
    uw = [_dot(tinv[h].astype(BF16), rhs16[h]) for h in heads]
    s = [st_ref[bi, h] for bi, h in units]
    ws = [_dot(jnp.concatenate([uw[h][:, GDN_DV:], qg[h]], axis=0).astype(BF16), s[h].astype(BF16))
          for h in heads]
    v16 = [(uw[h][:, :GDN_DV] - ws[h][:CHUNK]).astype(BF16) for h in heads]
    o_intra = [_dot((qk[h] * decay[h]).astype(BF16), v16[h]) for h in heads]
    s_add = [_dot_tn(kd16[h], v16[h]) for h in heads]
    for u, (bi, h) in enumerate(units):
        st_ref[bi, h] = s[u] * elast[u] + s_add[u]
        o = ws[u][CHUNK:] + o_intra[u]
        o = o * lax.rsqrt(jnp.mean(o * o, axis=-1, keepdims=True) + NORM_EPS) * nw
        o_ref[bi, :, h * GDN_DV:(h + 1) * GDN_DV] = o.astype(o_ref.dtype)

    xpad_ref[:, 0:CONV_HALO, :] = xpad_ref[:, CHUNK:CHUNK + CONV_HALO, :]


def _gdn(proj, conv_w, cs, bt, g_rows, norm_w, seq, pad):
    b, lp, _ = proj.shape
    nc = lp // CHUNK
    conv_dim = 3 * GDN_DQK
    n_seq = 2 if b % 2 == 0 else 1
    qkv = lambda blk: pl.BlockSpec((n_seq, CHUNK, GDN_DQK), lambda i, c: (i, c, blk))
    return pl.pallas_call(
        functools.partial(_gdn_body, pad=pad),
        grid=(b // n_seq, nc),
        in_specs=[
            qkv(0), qkv(1), qkv(2),
            pl.BlockSpec((CONV_K, conv_dim), lambda i, c: (0, 0)),
            pl.BlockSpec((n_seq, CHUNK, LANES), lambda i, c: (i, c, 0)),
            pl.BlockSpec((n_seq, CHUNK, LANES), lambda i, c: (i, c, 0)),
            pl.BlockSpec((n_seq, 1, GDN_HEADS, CHUNK), lambda i, c: (i, c, 0, 0)),
            pl.BlockSpec((1, GDN_DV), lambda i, c: (0, 0)),
            pl.BlockSpec(((CONV_K - 1) * CHUNK, CONV_HALO + CHUNK), lambda i, c: (0, 0)),
        ],
        out_specs=pl.BlockSpec((n_seq, CHUNK, GDN_DQK), lambda i, c: (i, jnp.maximum(c - 1, 0), 0)),
        out_shape=jax.ShapeDtypeStruct((b, seq, GDN_HEADS * GDN_DV), BF16),
        scratch_shapes=[
            pltpu.VMEM((n_seq, CONV_HALO + CHUNK, conv_dim), BF16),
            pltpu.VMEM((n_seq, (CONV_K - 1) * CHUNK, conv_dim), F32),
            pltpu.VMEM((n_seq, GDN_HEADS, GDN_DK, GDN_DV), F32),
        ],
        compiler_params=_params("arbitrary", "arbitrary"),
        name="gdn_scan",
    )(proj, proj, proj, conv_w, cs, bt, g_rows, norm_w.reshape(1, GDN_DV), _shift_matrix())


def _merge_body(y_ref, z_ref, o_ref, gate_ref, g1_ref, g2_ref, nw_ref, w1_ref, w2_ref, m_ref,
                a1_ref, a2_ref):
    @pl.when(pl.program_id(1) == 0)
    def _():
        gw = SSD_D_INNER // SSD_GROUPS
        for g in range(SSD_GROUPS):
            sl = slice(g * gw, (g + 1) * gw)
            t = y_ref[:, sl].astype(F32) * _silu(z_ref[:, sl].astype(F32))
            ms = jnp.mean(t * t, axis=-1, keepdims=True)
            a1_ref[:, sl] = (t * lax.rsqrt(ms + NORM_EPS) * nw_ref[:, sl]).astype(BF16)
        a2_ref[...] = (o_ref[...].astype(F32) * _silu(gate_ref[...].astype(F32))).astype(BF16)

    m = (_sigmoid(g1_ref[...].astype(F32)) * _dot(a1_ref[...], w1_ref[...])
         + _sigmoid(g2_ref[...].astype(F32)) * _dot(a2_ref[...], w2_ref[...]))
    m_ref[...] = m.astype(m_ref.dtype)


def _merge(y, o, proj_tok, norm_w, w1, w2, tm_pref=256, tn_pref=2048):
    m, d = y.shape
    n = w1.shape[1]
    tm = _divisor_tile(m, tm_pref, 16)
    tn = _divisor_tile(n, tn_pref, LANES)
    nj = n // tn
    row_d = lambda blk: pl.BlockSpec((tm, d), lambda i, j: (i, blk))
    return pl.pallas_call(
        _merge_body,
        grid=(m // tm, nj),
        in_specs=[
            pl.BlockSpec((tm, d), lambda i, j: (i, 0)),
            row_d(0),
            pl.BlockSpec((tm, d), lambda i, j: (i, 0)),
            row_d(1),
            pl.BlockSpec((tm, tn), lambda i, j: (i, 2 * nj + j)),
            pl.BlockSpec((tm, tn), lambda i, j: (i, 3 * nj + j)),
            pl.BlockSpec((1, d), lambda i, j: (0, 0)),
            pl.BlockSpec((d, tn), lambda i, j: (0, j)),
            pl.BlockSpec((d, tn), lambda i, j: (0, j)),
        ],
        out_specs=pl.BlockSpec((tm, tn), lambda i, j: (i, j)),
        out_shape=jax.ShapeDtypeStruct((m, n), BF16),
        scratch_shapes=[pltpu.VMEM((tm, d), BF16), pltpu.VMEM((tm, d), BF16)],
        compiler_params=_params("arbitrary", "arbitrary"),
        name="merge",
    )(y, proj_tok, o, proj_tok, proj_tok, proj_tok, norm_w.reshape(1, d), w1, w2)


def _matmul_residual_body(a_ref, w_ref, r_ref, o_ref):
    o_ref[...] = r_ref[...] + _dot(a_ref[...], w_ref[...])


def _matmul_residual(a, w, res, tm_pref=512, tn_pref=2048):
    m, k = a.shape
    n = w.shape[1]
    tm = _divisor_tile(m, tm_pref, 8)
    tn = _divisor_tile(n, tn_pref, LANES)
    return pl.pallas_call(
        _matmul_residual_body,
        grid=(m // tm, n // tn),
        in_specs=[
            pl.BlockSpec((tm, k), lambda i, j: (i, 0)),
            pl.BlockSpec((k, tn), lambda i, j: (0, j)),
            pl.BlockSpec((tm, tn), lambda i, j: (i, j)),
        ],
        out_specs=pl.BlockSpec((tm, tn), lambda i, j: (i, j)),
        out_shape=jax.ShapeDtypeStruct((m, n), F32),
        compiler_params=_params("arbitrary", "arbitrary"),
        name="out_proj_residual",
    )(a, w, res)


def _top_values(x, count):
    out = []
    for _ in range(count):
        m = jnp.max(x, axis=0, keepdims=True)
        out.append(m)
        x = jnp.where(x == m, -jnp.inf, x)
    return out


def _peer_route_body(q_ref, keys_ref, e2_ref, thr_ref, f1_ref):
    k1 = keys_ref[0].astype(BF16)
    k2 = keys_ref[1].astype(BF16)
    kk = PEER_TOPK
    for h in range(PEER_HEADS):
        lo = 2 * h * PEER_D_KEY
        q1 = q_ref[:, lo:lo + PEER_D_KEY]
        q2 = q_ref[:, lo + PEER_D_KEY:lo + 2 * PEER_D_KEY]
        s1 = _dot_nt(k1, q1)
        s2 = _dot_nt(k2, q2)
        t1 = _top_values(s1, kk + 1)
        t2 = _top_values(s2, kk + 1)
        t1_blk = jnp.concatenate(t1[:kk], axis=0)
        t2_blk = jnp.concatenate(t2[:kk], axis=0)
        half = kk // 2
        cand = jnp.concatenate(
            [t1[0] + t2_blk]
            + [t1[a] + t2_blk[:half] for a in range(1, half)]
            + [t1_blk[half:] + t2[0]], axis=0)
        best = _top_values(cand, kk + 1)
        runner_up = jnp.maximum(best[kk], jnp.maximum(t1[kk] + t2[0], t1[0] + t2[kk]))
        tau = 0.5 * (best[kk - 1] + runner_up)
        m1, m2 = t1[0], t2[0]
        z = jnp.sum(jnp.where(cand >= tau, jnp.exp(cand - (m1 + m2)), 0.0), axis=0, keepdims=True)
        e2 = jnp.exp(s2 - m2)
        for s in range(e2_ref.shape[1]):
            e2_ref[h, s] = e2[:, s * LANES:(s + 1) * LANES]
        thr_ref[h] = jnp.exp(tau - s1 - m2)
        f1_ref[h] = jnp.exp(s1 - m1) / z


def _peer_route(q, keys, t_tok):
    ntok = q.shape[0]
    spec = pl.BlockSpec((PEER_HEADS, PEER_N_KEYS, t_tok), lambda t: (0, 0, t))
    shp = jax.ShapeDtypeStruct((PEER_HEADS, PEER_N_KEYS, ntok), F32)
    strip_spec = pl.BlockSpec((PEER_HEADS, t_tok // LANES, PEER_N_KEYS, LANES), lambda t: (0, t, 0, 0))
    strip_shp = jax.ShapeDtypeStruct((PEER_HEADS, ntok // LANES, PEER_N_KEYS, LANES), F32)
    return pl.pallas_call(
        _peer_route_body,
        grid=(ntok // t_tok,),
        in_specs=[
            pl.BlockSpec((t_tok, q.shape[1]), lambda t: (t, 0)),
            pl.BlockSpec(keys.shape, lambda t: (0, 0, 0)),
        ],
        out_specs=(strip_spec, spec, spec),
        out_shape=(strip_shp, shp, shp),
        compiler_params=_params("arbitrary"),
        name="peer_route",
    )(q, keys)


def _peer_expert_body(hnt_ref, e2_ref, thr_ref, f1_ref, down_ref, upt_ref, acc_ref,
                      w_ref, act_ref, p_ref):
    d, t_tok = hnt_ref.shape
    te = down_ref.shape[0]
    n_strip = t_tok // LANES
    n_part = min(PEER_MATMUL_PARTS, n_strip)
    rows_a = te // n_part
    rows_o = d // n_part
    e = pl.program_id(1)
    last = pl.num_programs(1) - 1
    slot = lax.rem(e, 2)

    def run(build, project):
        for part in range(n_part):
            if build:
                for s in range(part * n_strip // n_part, (part + 1) * n_strip // n_part):
                    ln = slice(s * LANES, (s + 1) * LANES)
                    for i in range(te // PEER_N_KEYS):
                        w = None
                        for hd in range(PEER_HEADS):
                            thr = thr_ref[hd, i:i + 1, ln]
                            f1 = f1_ref[hd, i:i + 1, ln]
                            e2 = e2_ref[hd, s]
                            term = f1 * jnp.where(e2 >= thr, e2, 0.0)
                            w = term if w is None else w + term
                        w_ref[i * PEER_N_KEYS:(i + 1) * PEER_N_KEYS, ln] = w
                ra = slice(part * rows_a, (part + 1) * rows_a)
                a = _dot(down_ref[ra, :], hnt_ref[...])
                act_ref[ra, :] = 0.5 * a * (1.0 + lax.erf(a * (2.0 ** -0.5)))
            if project:
                ro = slice(part * rows_o, (part + 1) * rows_o)
                acc_ref[ro, :] += _dot(upt_ref[ro, :], p_ref[1 - slot])
        if build:
            p_ref[slot] = (w_ref[...] * act_ref[...]).astype(BF16)

    @pl.when(e == 0)
    def _():
        acc_ref[...] = jnp.zeros_like(acc_ref)
        run(build=True, project=False)

    @pl.when(jnp.logical_and(e > 0, e < last))
    def _():
        run(build=True, project=True)

    @pl.when(e == last)
    def _():
        run(build=False, project=True)


def _peer_experts(hn_t, e2, thr, f1, down, up_t, t_tok, keys_per_block):
    d, ntok = hn_t.shape
    n_exp = down.shape[0]
    te = keys_per_block * PEER_N_KEYS
    n_blk = n_exp // te
    cur = lambda e: jnp.minimum(e, n_blk - 1)
    prev = lambda e: jnp.maximum(e - 1, 0)
    full = pl.BlockSpec((PEER_HEADS, t_tok // LANES, PEER_N_KEYS, LANES), lambda t, e: (0, t, 0, 0))
    part = pl.BlockSpec((PEER_HEADS, keys_per_block, t_tok), lambda t, e: (0, cur(e), t))
    return pl.pallas_call(
        _peer_expert_body,
        grid=(ntok // t_tok, n_blk + 1),
        in_specs=[
            pl.BlockSpec((d, t_tok), lambda t, e: (0, t)),
            full, part, part,
            pl.BlockSpec((te, d), lambda t, e: (cur(e), 0)),
            pl.BlockSpec((d, te), lambda t, e: (0, prev(e))),
        ],
        out_specs=pl.BlockSpec((d, t_tok), lambda t, e: (0, t)),
        out_shape=jax.ShapeDtypeStruct((d, ntok), F32),
        scratch_shapes=[
            pltpu.VMEM((te, t_tok), F32),
            pltpu.VMEM((te, t_tok), F32),
            pltpu.VMEM((2, te, t_tok), BF16),
        ],
        compiler_params=_params("arbitrary", "arbitrary"),
        name="peer_experts",
    )(hn_t, e2, thr, f1, down, up_t)


def _residual_norm_body(h_ref, acc_ref, g_ref, o_ref):
    y = h_ref[...] + acc_ref[...].T
    ms = jnp.mean(y * y, axis=-1, keepdims=True)
    o_ref[...] = y * lax.rsqrt(ms + NORM_EPS) * g_ref[...]


def _residual_norm(h2, acc_t, gain, tm_pref=256):
    m, d = h2.shape
    tm = _divisor_tile(m, tm_pref, LANES)
    return pl.pallas_call(
        _residual_norm_body,
        grid=(m // tm,),
        in_specs=[
            pl.BlockSpec((tm, d), lambda i: (i, 0)),
            pl.BlockSpec((d, tm), lambda i: (0, i)),
            pl.BlockSpec((1, d), lambda i: (0, 0)),
        ],
        out_specs=pl.BlockSpec((tm, d), lambda i: (i, 0)),
        out_shape=jax.ShapeDtypeStruct((m, d), F32),
        compiler_params=_params("arbitrary"),
        name="residual_norm",
    )(h2, acc_t, gain.reshape(1, d))


def _cast_body(x_ref, o_ref, *, transpose):
    x = x_ref[...]
    o_ref[...] = (x.T if transpose else x).astype(o_ref.dtype)


def _cast_bf16(w, rows_pref, transpose=False, row_multiple=LANES):
    r, c = w.shape
    tr = _divisor_tile(r, rows_pref, row_multiple)
    out_spec = pl.BlockSpec((c, tr), lambda i: (0, i)) if transpose else pl.BlockSpec((tr, c), lambda i: (i, 0))
    return pl.pallas_call(
        functools.partial(_cast_body, transpose=transpose),
        grid=(r // tr,),
        in_specs=[pl.BlockSpec((tr, c), lambda i: (i, 0))],
        out_specs=out_spec,
        out_shape=jax.ShapeDtypeStruct((c, r) if transpose else (r, c), BF16),
        compiler_params=_params("arbitrary"),
        name="cast_bf16",
    )(w)


def _w_in_segments(d):
    widths = (("z", SSD_D_INNER), ("xbc", SSD_D_INNER + 2 * SSD_BC), ("dt", SSD_HEADS),
              ("qkv", 3 * GDN_DQK), ("gate", GDN_HEADS * GDN_DV), ("a", GDN_HEADS),
              ("beta", GDN_HEADS), ("merge", 2 * d))
    seg, off = {}, 0
    for name, wd in widths:
        seg[name] = slice(off, off + wd)
        off += wd
    return seg


def _row_segments(seg, names):
    return tuple((seg[n].start, seg[n].stop - seg[n].start) for n in names)


def _lane_vec(*parts):
    v = jnp.concatenate([p.astype(F32) for p in parts])
    return jnp.pad(v, (0, LANES - v.shape[0])).reshape(1, LANES)


def _mixer(x, meta_tokens, norm_w, w_in, ssd_conv_w, ssd_conv_b, ssd_dt_bias, ssd_A_log, ssd_D,
           ssd_norm_w, gdn_conv_w, gdn_dt_bias, gdn_A_log, gdn_norm_w, w_branch_ssd, w_branch_gdn, w_out):
    b, seq, d = x.shape
    pad = CHUNK - N_META
    lp = seq + CHUNK
    nc = lp // CHUNK
    seg = _w_in_segments(d)
    w16_t = _cast_bf16(jnp.swapaxes(w_in, 0, 1), rows_pref=512, row_multiple=64)
    small = jnp.concatenate([w16_t[seg[n]] for n in ("dt", "a", "beta")], axis=0)
    w_small = jnp.pad(small, ((0, LANES - small.shape[0]), (0, 0)))

    meta = jnp.broadcast_to(meta_tokens[None].astype(x.dtype), (b, N_META, d))
    hp = jnp.concatenate([jnp.zeros((b, pad, d), x.dtype), meta, x], axis=1).reshape(b * lp, d)
    x2d = x.reshape(b * seq, d)

    proj_scan = _norm_matmul(hp, norm_w, w16_t, tm_pref=1056,
                             w_row_segments=_row_segments(seg, ("qkv", "xbc"))).reshape(b, lp, -1)
    raw_small = _norm_matmul(hp, norm_w, w_small, tm_pref=1056, out_dtype=F32,
                             w_row_segments=((0, LANES),)).reshape(b, lp, LANES)
    proj_tok = _norm_matmul(x2d, norm_w, w16_t, tm_pref=1024,
                            w_row_segments=_row_segments(seg, ("z", "gate", "merge")))

    zeros_h = jnp.zeros((GDN_HEADS,), F32)
    bias_vec = _lane_vec(ssd_dt_bias, gdn_dt_bias, zeros_h)
    alog_vec = _lane_vec(ssd_A_log, gdn_A_log, zeros_h)
    sp, cs, bt = _gates(raw_small, bias_vec, alog_vec, pad)
    cs_c = cs.reshape(b, nc, CHUNK, LANES)
    ssd_rows = jnp.swapaxes(cs_c[..., :SSD_HEADS], 2, 3)
    gdn_rows = jnp.swapaxes(cs_c[..., SSD_HEADS:SSD_HEADS + GDN_HEADS], 2, 3)

    d_vec = jnp.repeat(ssd_D.astype(F32), SSD_HEAD_DIM).reshape(1, SSD_D_INNER)
    y = _ssd(proj_scan, ssd_conv_w, ssd_conv_b, sp, cs, ssd_rows, d_vec, seq, pad)
    o = _gdn(proj_scan, gdn_conv_w, cs, bt, gdn_rows, gdn_norm_w, seq, pad)

    m = _merge(y.reshape(b * seq, -1), o.reshape(b * seq, -1), proj_tok, ssd_norm_w,
               w_branch_ssd.astype(BF16), w_branch_gdn.astype(BF16))
    return _matmul_residual(m, w_out.astype(BF16), x2d)


def _peer(h2, norm_w, w_query, sub_keys, expert_down, expert_up, final_w):
    ntok = h2.shape[0]
    q, hn_t = _norm_matmul(h2, norm_w, w_query.astype(BF16), tm_pref=512, tn_pref=2048,
                           with_transposed=True)
    e2, thr, f1 = _peer_route(q, sub_keys, _divisor_tile(ntok, 256, LANES))
    acc_t = _peer_experts(hn_t, e2, thr, f1, _cast_bf16(expert_down, rows_pref=1024),
                          _cast_bf16(expert_up, rows_pref=512, transpose=True),
                          _divisor_tile(ntok, 512, LANES), keys_per_block=8)
    return _residual_norm(h2, acc_t, final_w)


def kernel(x, meta_tokens, norm_mix_w, w_in, ssd_conv_w, ssd_conv_b, ssd_dt_bias, ssd_A_log, ssd_D, ssd_norm_w, gdn_conv_w, gdn_dt_bias, gdn_A_log, gdn_norm_w, w_branch_ssd, w_branch_gdn, w_out, norm_ffn_w, peer_w_query, peer_sub_keys, peer_expert_down, peer_expert_up, final_norm_w):
    b, seq, d = x.shape
    assert norm_mix_w.shape[0] == 1, "single-layer trunk"
    h2 = _mixer(x, meta_tokens, norm_mix_w[0], w_in[0], ssd_conv_w[0], ssd_conv_b[0], ssd_dt_bias[0],
                ssd_A_log[0], ssd_D[0], ssd_norm_w[0], gdn_conv_w[0], gdn_dt_bias[0], gdn_A_log[0],
                gdn_norm_w[0], w_branch_ssd[0], w_branch_gdn[0], w_out[0])
    out = _peer(h2, norm_ffn_w[0], peer_w_query[0], peer_sub_keys[0], peer_expert_down[0],
                peer_expert_up[0], final_norm_w)
    return out.reshape(b, seq, d)
```

```python
import functools

import jax
import jax.numpy as jnp
from jax import lax
from jax.experimental import pallas as pl
from jax.experimental.pallas import tpu as pltpu

F32 = jnp.float32
BF16 = jnp.bfloat16

N_META = 16
CHUNK = 64
CONV_K = 4
NORM_EPS = 1e-6
L2_EPS = 1e-6

SSD_HEADS = 32
SSD_HEAD_DIM = 64
SSD_GROUPS = 4
SSD_STATE = 128
SSD_HPG = SSD_HEADS // SSD_GROUPS
SSD_D_INNER = SSD_HEADS * SSD_HEAD_DIM
SSD_BC = SSD_GROUPS * SSD_STATE

GDN_HEADS = 16
GDN_DK = 128
GDN_DV = 128
GDN_DQK = GDN_HEADS * GDN_DK

PEER_HEADS = 8
PEER_N_KEYS = 128
PEER_TOPK = 16
PEER_D_KEY = 128
PEER_MATMUL_PARTS = 2

CONV_HALO = 16
LANES = 128
VMEM_LIMIT = 56 * 1024 * 1024

_NT = (((1,), (1,)), ((), ()))
_TN = (((0,), (0,)), ((), ()))


def _dot(a, b):
    return jnp.dot(a, b, preferred_element_type=F32)


def _dot_nt(a, b):
    return lax.dot_general(a, b, _NT, preferred_element_type=F32)


def _dot_tn(a, b):
    return lax.dot_general(a, b, _TN, preferred_element_type=F32)


def _silu(x):
    return x / (1.0 + jnp.exp(-x))


def _sigmoid(x):
    return 1.0 / (1.0 + jnp.exp(-x))


def _softplus(x):
    return jnp.maximum(x, 0.0) + jnp.log(1.0 + jnp.exp(-jnp.abs(x)))


def _divisor_tile(n, preferred, multiple):
    t = min(preferred, n)
    t -= t % multiple
    while t >= multiple:
        if n % t == 0:
            return t
        t -= multiple
    return n


def _params(*sem):
    return pltpu.CompilerParams(dimension_semantics=sem, vmem_limit_bytes=VMEM_LIMIT)


def _norm_matmul_body(x_ref, g_ref, w_ref, o_ref, *rest, w_is_transposed):
    n_ref = rest[-1]
    nt_ref = rest[0] if len(rest) == 2 else None

    @pl.when(pl.program_id(1) == 0)
    def _():
        x = x_ref[...]
        ms = jnp.mean(x * x, axis=-1, keepdims=True)
        n = x * lax.rsqrt(ms + NORM_EPS) * g_ref[...]
        n_ref[...] = n.astype(BF16)
        if nt_ref is not None:
            nt_ref[...] = n.T.astype(BF16)

    dot = _dot_nt if w_is_transposed else _dot
    o_ref[...] = dot(n_ref[...], w_ref[...]).astype(o_ref.dtype)


def _segment_row(j, segments, tn):
    unit = 16
    row, first_blk = None, 0
    for start, length in segments:
        assert start % unit == 0 and tn % unit == 0
        cand = start // unit + (j - first_blk) * (tn // unit)
        row = cand if row is None else jnp.where(j >= first_blk, cand, row)
        first_blk += length // tn
    return row * unit


def _norm_matmul(x2d, gain, w, tm_pref=768, tn_pref=1024, out_dtype=BF16, with_transposed=False,
                 w_row_segments=None):
    m, k = x2d.shape
    w_is_transposed = w_row_segments is not None
    n = sum(ln for _, ln in w_row_segments) if w_is_transposed else w.shape[1]
    tm = _divisor_tile(m, tm_pref, LANES if with_transposed else 16)
    tn = _divisor_tile(n, tn_pref, LANES)
    if w_is_transposed:
        assert all(ln % tn == 0 for _, ln in w_row_segments)
        w_spec = pl.BlockSpec((pl.Element(tn), pl.Element(k)),
                              lambda i, j: (_segment_row(j, w_row_segments, tn), 0))
    else:
        w_spec = pl.BlockSpec((k, tn), lambda i, j: (0, j))
    out_specs = pl.BlockSpec((tm, tn), lambda i, j: (i, j))
    out_shape = jax.ShapeDtypeStruct((m, n), out_dtype)
    if with_transposed:
        out_specs = (out_specs, pl.BlockSpec((k, tm), lambda i, j: (0, i)))
        out_shape = (out_shape, jax.ShapeDtypeStruct((k, m), BF16))
    return pl.pallas_call(
        functools.partial(_norm_matmul_body, w_is_transposed=w_is_transposed),
        grid=(m // tm, n // tn),
        in_specs=[
            pl.BlockSpec((tm, k), lambda i, j: (i, 0)),
            pl.BlockSpec((1, k), lambda i, j: (0, 0)),
            w_spec,
        ],
        out_specs=out_specs,
        out_shape=out_shape,
        scratch_shapes=[pltpu.VMEM((tm, k), BF16)],
        compiler_params=_params("arbitrary", "arbitrary"),
        name="norm_matmul",
    )(x2d, gain.reshape(1, k), w)


def _gates_body(raw_ref, bias_ref, alog_ref, sp_ref, cs_ref, bt_ref, *, n_chunks, pad):
    ri = lax.broadcasted_iota(jnp.int32, (CHUNK, CHUNK), 0)
    ci = lax.broadcasted_iota(jnp.int32, (CHUNK, CHUNK), 1)
    tri = (ri >= ci).astype(F32)
    neg_a = -jnp.exp(alog_ref[...])
    bias = bias_ref[...]
    row = lax.broadcasted_iota(jnp.int32, (CHUNK, 1), 0)

    def body(c, carry):
        sl = pl.ds(pl.multiple_of(c * CHUNK, CHUNK), CHUNK)
        r = raw_ref[0, sl, :]
        sp = _softplus(r + bias)
        cs = jnp.dot(tri, sp * neg_a, precision=lax.Precision.HIGHEST, preferred_element_type=F32)
        valid = (row + c * CHUNK) >= pad
        sp_ref[0, sl, :] = sp
        cs_ref[0, sl, :] = cs
        bt_ref[0, sl, :] = jnp.where(valid, _sigmoid(r), 0.0)
        return carry

    lax.fori_loop(0, n_chunks, body, 0)


def _gates(raw, bias_vec, alog_vec, pad):
    b, lp, w = raw.shape
    spec = pl.BlockSpec((1, lp, w), lambda i: (i, 0, 0))
    vec = pl.BlockSpec((1, w), lambda i: (0, 0))
    shp = jax.ShapeDtypeStruct((b, lp, w), F32)
    return pl.pallas_call(
        functools.partial(_gates_body, n_chunks=lp // CHUNK, pad=pad),
        grid=(b,),
        in_specs=[spec, vec, vec],
        out_specs=(spec, spec, spec),
        out_shape=(shp, shp, shp),
        compiler_params=_params("arbitrary"),
        name="gates",
    )(raw, bias_vec, alog_vec)


def _shift_matrix():
    r = jnp.arange((CONV_K - 1) * CHUNK)
    src = CONV_HALO + r % CHUNK - (CONV_K - 1 - r // CHUNK)
    return (jnp.arange(CONV_HALO + CHUNK)[None, :] == src[:, None]).astype(BF16)


def _shifted_taps(shift_ref, xpad_ref, taps_ref):
    taps_ref[...] = _dot(shift_ref[...], xpad_ref[...])


def _causal_conv(taps_ref, xpad_ref, w_ref, lo, width, bias):
    cols = slice(lo, lo + width)
    acc = w_ref[CONV_K - 1:CONV_K, cols] * xpad_ref[CONV_HALO:CONV_HALO + CHUNK, cols].astype(F32)
    if bias is not None:
        acc = acc + bias
    for j in range(CONV_K - 1):
        acc = acc + w_ref[j:j + 1, cols] * taps_ref[j * CHUNK:(j + 1) * CHUNK, cols]
    return acc


def _ssd_body(x_ref, bc_ref, cw_ref, cb_ref, sp_ref, cs_ref, csr_ref, dvec_ref, expand_ref, shift_ref,
              y_ref, xpad_ref, taps_ref, st_ref, *, pad):
    c = pl.program_id(1)
    n_seq = x_ref.shape[0]

    @pl.when(c == 0)
    def _():
        st_ref[...] = jnp.zeros_like(st_ref)
        xpad_ref[:, 0:CONV_HALO, :] = jnp.zeros((n_seq, CONV_HALO, xpad_ref.shape[2]), BF16)

    for bi in range(n_seq):
        xpad_ref[bi, CONV_HALO:CONV_HALO + CHUNK, 0:SSD_D_INNER] = x_ref[bi]
        xpad_ref[bi, CONV_HALO:CONV_HALO + CHUNK, SSD_D_INNER:] = bc_ref[bi]
        _shifted_taps(shift_ref, xpad_ref.at[bi], taps_ref.at[bi])

    row = lax.broadcasted_iota(jnp.int32, (CHUNK, 1), 0)
    valid = (row + c * CHUNK) >= pad
    ri = lax.broadcasted_iota(jnp.int32, (CHUNK, CHUNK), 0)
    ci = lax.broadcasted_iota(jnp.int32, (CHUNK, CHUNK), 1)
    tril = ri >= ci

    def conv(bi, lo, width):
        a = _causal_conv(taps_ref.at[bi], xpad_ref.at[bi], cw_ref, lo, width, cb_ref[:, lo:lo + width])
        return jnp.where(valid, _silu(a), 0.0)

    gw = SSD_HPG * SSD_HEAD_DIM
    seqs = range(n_seq)
    gunits = [(bi, g) for bi in seqs for g in range(SSD_GROUPS)]
    groups = range(len(gunits))
    xg = [conv(bi, g * gw, gw) for bi, g in gunits]
    bg = [conv(bi, SSD_D_INNER + g * SSD_STATE, SSD_STATE).astype(BF16) for bi, g in gunits]
    cg = [conv(bi, SSD_D_INNER + SSD_BC + g * SSD_STATE, SSD_STATE).astype(BF16) for bi, g in gunits]
    cb = [_dot_nt(cg[u], bg[u]) for u in groups]
    st = [st_ref[bi, g] for bi, g in gunits]
    yoff = [_dot(cg[u], st[u].astype(BF16)) for u in groups]

    e_dt, e_w, e_cs = [], [], []
    for bi in seqs:
        sp, cs = sp_ref[bi], cs_ref[bi]
        factors = (sp, sp * jnp.exp(cs[CHUNK - 1:CHUNK, :] - cs), jnp.exp(cs))
        pieces = []
        for f in factors:
            hi = f.astype(BF16)
            pieces += [hi, (f - hi.astype(F32)).astype(BF16)]
        spread = _dot(jnp.concatenate(pieces, axis=0), expand_ref[...])
        for k, dst in enumerate((e_dt, e_w, e_cs)):
            dst.append(spread[(2 * k) * CHUNK:(2 * k + 1) * CHUNK] + spread[(2 * k + 1) * CHUNK:(2 * k + 2) * CHUNK])

    xdt = [(xg[u] * e_dt[bi][:, g * gw:(g + 1) * gw]).astype(BF16) for u, (bi, g) in enumerate(gunits)]
    xw = [(xg[u] * e_w[bi][:, g * gw:(g + 1) * gw]).astype(BF16) for u, (bi, g) in enumerate(gunits)]
    w_intra, xdt_head = [], []
    for bi in seqs:
        cs, csr = cs_ref[bi], csr_ref[bi, 0]
        for h in range(SSD_HEADS):
            g, r = divmod(h, SSD_HPG)
            decay = jnp.where(tril, jnp.exp(cs[:, h:h + 1] - csr[h:h + 1, :]), 0.0)
            w_intra.append((cb[bi * SSD_GROUPS + g] * decay).astype(BF16))
            xdt_head.append(xdt[bi * SSD_GROUPS + g][:, r * SSD_HEAD_DIM:(r + 1) * SSD_HEAD_DIM])
    ydiag = [_dot(w, x) for w, x in zip(w_intra, xdt_head)]
    snew = [_dot_tn(bg[u], xw[u]) for u in groups]
    for u, (bi, g) in enumerate(gunits):
        sl = slice(g * gw, (g + 1) * gw)
        st_ref[bi, g] = st[u] * e_cs[bi][CHUNK - 1:CHUNK, sl] + snew[u]
        y_g = (jnp.concatenate(ydiag[u * SSD_HPG:(u + 1) * SSD_HPG], axis=1) + yoff[u] * e_cs[bi][:, sl]
               + dvec_ref[:, sl] * xg[u])
        y_ref[bi, :, sl] = y_g.astype(y_ref.dtype)

    xpad_ref[:, 0:CONV_HALO, :] = xpad_ref[:, CHUNK:CHUNK + CONV_HALO, :]


def _ssd(proj, conv_w, conv_b, sp, cs, cs_rows, d_vec, seq, pad):
    b, lp, _ = proj.shape
    nc = lp // CHUNK
    conv_dim = SSD_D_INNER + 2 * SSD_BC
    x_blk = 3 * GDN_DQK // SSD_D_INNER
    bc_blk = (3 * GDN_DQK + SSD_D_INNER) // (2 * SSD_BC)
    expand = (jnp.arange(LANES)[:, None] == jnp.arange(SSD_D_INNER)[None, :] // SSD_HEAD_DIM).astype(BF16)
    n_seq = 2 if b % 2 == 0 else 1
    return pl.pallas_call(
        functools.partial(_ssd_body, pad=pad),
        grid=(b // n_seq, nc),
        in_specs=[
            pl.BlockSpec((n_seq, CHUNK, SSD_D_INNER), lambda i, c: (i, c, x_blk)),
            pl.BlockSpec((n_seq, CHUNK, 2 * SSD_BC), lambda i, c: (i, c, bc_blk)),
            pl.BlockSpec((CONV_K, conv_dim), lambda i, c: (0, 0)),
            pl.BlockSpec((1, conv_dim), lambda i, c: (0, 0)),
            pl.BlockSpec((n_seq, CHUNK, LANES), lambda i, c: (i, c, 0)),
            pl.BlockSpec((n_seq, CHUNK, LANES), lambda i, c: (i, c, 0)),
            pl.BlockSpec((n_seq, 1, SSD_HEADS, CHUNK), lambda i, c: (i, c, 0, 0)),
            pl.BlockSpec((1, SSD_D_INNER), lambda i, c: (0, 0)),
            pl.BlockSpec((LANES, SSD_D_INNER), lambda i, c: (0, 0)),
            pl.BlockSpec(((CONV_K - 1) * CHUNK, CONV_HALO + CHUNK), lambda i, c: (0, 0)),
        ],
        out_specs=pl.BlockSpec((n_seq, CHUNK, SSD_D_INNER), lambda i, c: (i, jnp.maximum(c - 1, 0), 0)),
        out_shape=jax.ShapeDtypeStruct((b, seq, SSD_D_INNER), BF16),
        scratch_shapes=[
            pltpu.VMEM((n_seq, CONV_HALO + CHUNK, conv_dim), BF16),
            pltpu.VMEM((n_seq, (CONV_K - 1) * CHUNK, conv_dim), F32),
            pltpu.VMEM((n_seq, SSD_GROUPS, SSD_STATE, SSD_HPG * SSD_HEAD_DIM), F32),
        ],
        compiler_params=_params("arbitrary", "arbitrary"),
        name="ssd_scan",
    )(proj, proj, conv_w, conv_b.reshape(1, conv_dim), sp, cs, cs_rows, d_vec, expand, _shift_matrix())


def _unit_lower_inverses(strict_ls, eye):
    ps = [l.astype(BF16) for l in strict_ls]
    ts = [eye - l for l in strict_ls]
    n = 2
    while n < CHUNK:
        p2 = [_dot(p, p).astype(BF16) for p in ps]
        ts = [t + _dot(t.astype(BF16), q) for t, q in zip(ts, p2)]
        ps = p2
        n *= 2
    return ts


def _gdn_body(q_ref, k_ref, v_ref, cw_ref, cs_ref, bt_ref, gr_ref, nw_ref, shift_ref, o_ref,
              xpad_ref, taps_ref, st_ref, *, pad):
    c = pl.program_id(1)
    n_seq = q_ref.shape[0]

    @pl.when(c == 0)
    def _():
        st_ref[...] = jnp.zeros_like(st_ref)
        xpad_ref[:, 0:CONV_HALO, :] = jnp.zeros((n_seq, CONV_HALO, xpad_ref.shape[2]), BF16)

    for bi in range(n_seq):
        xpad_ref[bi, CONV_HALO:CONV_HALO + CHUNK, 0:GDN_DQK] = q_ref[bi]
        xpad_ref[bi, CONV_HALO:CONV_HALO + CHUNK, GDN_DQK:2 * GDN_DQK] = k_ref[bi]
        xpad_ref[bi, CONV_HALO:CONV_HALO + CHUNK, 2 * GDN_DQK:] = v_ref[bi]
        _shifted_taps(shift_ref, xpad_ref.at[bi], taps_ref.at[bi])

    row = lax.broadcasted_iota(jnp.int32, (CHUNK, 1), 0)
    valid = (row + c * CHUNK) >= pad
    ri = lax.broadcasted_iota(jnp.int32, (CHUNK, CHUNK), 0)
    ci = lax.broadcasted_iota(jnp.int32, (CHUNK, CHUNK), 1)
    tril = ri >= ci
    strict = ri > ci
    eye = (ri == ci).astype(F32)

    nw = nw_ref[...]
    g_off = SSD_HEADS
    b_off = SSD_HEADS + GDN_HEADS

    def conv(bi, lo):
        a = _causal_conv(taps_ref.at[bi], xpad_ref.at[bi], cw_ref, lo, GDN_DK, None)
        return jnp.where(valid, _silu(a), 0.0)

    units = [(bi, h) for bi in range(n_seq) for h in range(GDN_HEADS)]
    heads = range(len(units))
    q16, k16, kb16, rhs16, qg, kd16, decay, elast = [], [], [], [], [], [], [], []
    for bi, h in units:
        cs = cs_ref[bi]
        bt = bt_ref[bi]
        gr = gr_ref[bi, 0]
        q = conv(bi, h * GDN_DK)
        k = conv(bi, GDN_DQK + h * GDN_DK)
        v = conv(bi, 2 * GDN_DQK + h * GDN_DV)
        q = q * lax.rsqrt(jnp.sum(q * q, axis=-1, keepdims=True) + L2_EPS) * (GDN_DK ** -0.5)
        k = k * lax.rsqrt(jnp.sum(k * k, axis=-1, keepdims=True) + L2_EPS)
        col = cs[:, g_off + h:g_off + h + 1]
        rowv = gr[h:h + 1, :]
        last = rowv[:, CHUNK - 1:CHUNK]
        beta = bt[:, b_off + h:b_off + h + 1]
        ecol = jnp.exp(col)
        kb = k * beta
        decay.append(jnp.where(tril, jnp.exp(col - rowv), 0.0))
        q16.append(q.astype(BF16))
        k16.append(k.astype(BF16))
        kb16.append(kb.astype(BF16))
        rhs16.append(jnp.concatenate([v * beta, kb * ecol], axis=1).astype(BF16))
        qg.append(q * ecol)
        kd16.append((k * jnp.exp(last - col)).astype(BF16))
        elast.append(jnp.exp(last))
    kk = [_dot_nt(kb16[h], k16[h]) for h in heads]
    qk = [_dot_nt(q16[h], k16[h]) for h in heads]
    tinv = _unit_lower_inverses([jnp.where(strict, kk[h] * decay[h], 0.0) for h in heads], eye)
    uw = [_dot(tinv[h].astype(BF16), rhs16[h]) for h in heads]
    s = [st_ref[bi, h] for bi, h in units]
    ws = [_dot(jnp.concatenate([uw[h][:, GDN_DV:], qg[h]], axis=0).astype(BF16), s[h].astype(BF16))
          for h in heads]
    v16 = [(uw[h][:, :GDN_DV] - ws[h][:CHUNK]).astype(BF16) for h in heads]
    o_intra = [_dot((qk[h] * decay[h]).astype(BF16), v16[h]) for h in heads]
    s_add = [_dot_tn(kd16[h], v16[h]) for h in heads]
    for u, (bi, h) in enumerate(units):
        st_ref[bi, h] = s[u] * elast[u] + s_add[u]
        o = ws[u][CHUNK:] + o_intra[u]
        o = o * lax.rsqrt(jnp.mean(o * o, axis=-1, keepdims=True) + NORM_EPS) * nw
        o_ref[bi, :, h * GDN_DV:(h + 1) * GDN_DV] = o.astype(o_ref.dtype)

    xpad_ref[:, 0:CONV_HALO, :] = xpad_ref[:, CHUNK:CHUNK + CONV_HALO, :]


def _gdn(proj, conv_w, cs, bt, g_rows, norm_w, seq, pad):
    b, lp, _ = proj.shape
    nc = lp // CHUNK
    conv_dim = 3 * GDN_DQK
    n_seq = 2 if b % 2 == 0 else 1
    qkv = lambda blk: pl.BlockSpec((n_seq, CHUNK, GDN_DQK), lambda i, c: (i, c, blk))
    return pl.pallas_call(
        functools.partial(_gdn_body, pad=pad),
        grid=(b // n_seq, nc),
        in_specs=[
            qkv(0), qkv(1), qkv(2),
            pl.BlockSpec((CONV_K, conv_dim), lambda i, c: (0, 0)),
            pl.BlockSpec((n_seq, CHUNK, LANES), lambda i, c: (i, c, 0)),
            pl.BlockSpec((n_seq, CHUNK, LANES), lambda i, c: (i, c, 0)),
            pl.BlockSpec((n_seq, 1, GDN_HEADS, CHUNK), lambda i, c: (i, c, 0, 0)),
            pl.BlockSpec((1, GDN_DV), lambda i, c: (0, 0)),
            pl.BlockSpec(((CONV_K - 1) * CHUNK, CONV_HALO + CHUNK), lambda i, c: (0, 0)),
        ],
        out_specs=pl.BlockSpec((n_seq, CHUNK, GDN_DQK), lambda i, c: (i, jnp.maximum(c - 1, 0), 0)),
        out_shape=jax.ShapeDtypeStruct((b, seq, GDN_HEADS * GDN_DV), BF16),
        scratch_shapes=[
            pltpu.VMEM((n_seq, CONV_HALO + CHUNK, conv_dim), BF16),
            pltpu.VMEM((n_seq, (CONV_K - 1) * CHUNK, conv_dim), F32),
            pltpu.VMEM((n_seq, GDN_HEADS, GDN_DK, GDN_DV), F32),
        ],
        compiler_params=_params("arbitrary", "arbitrary"),
        name="gdn_scan",
    )(proj, proj, proj, conv_w, cs, bt, g_rows, norm_w.reshape(1, GDN_DV), _shift_matrix())


def _merge_body(y_ref, z_ref, o_ref, gate_ref, g1_ref, g2_ref, nw_ref, w1_ref, w2_ref, m_ref,
                a1_ref, a2_ref):
    @pl.when(pl.program_id(1) == 0)
    def _():
        gw = SSD_D_INNER // SSD_GROUPS
        for g in range(SSD_GROUPS):
            sl = slice(g * gw, (g + 1) * gw)
            t = y_ref[:, sl].astype(F32) * _silu(z_ref[:, sl].astype(F32))
            ms = jnp.mean(t * t, axis=-1, keepdims=True)
            a1_ref[:, sl] = (t * lax.rsqrt(ms + NORM_EPS) * nw_ref[:, sl]).astype(BF16)
        a2_ref[...] = (o_ref[...].astype(F32) * _silu(gate_ref[...].astype(F32))).astype(BF16)

    m = (_sigmoid(g1_ref[...].astype(F32)) * _dot(a1_ref[...], w1_ref[...])
         + _sigmoid(g2_ref[...].astype(F32)) * _dot(a2_ref[...], w2_ref[...]))
    m_ref[...] = m.astype(m_ref.dtype)


def _merge(y, o, proj_tok, norm_w, w1, w2, tm_pref=256, tn_pref=2048):
    m, d = y.shape
    n = w1.shape[1]
    tm = _divisor_tile(m, tm_pref, 16)
    tn = _divisor_tile(n, tn_pref, LANES)
    nj = n // tn
    row_d = lambda blk: pl.BlockSpec((tm, d), lambda i, j: (i, blk))
    return pl.pallas_call(
        _merge_body,
        grid=(m // tm, nj),
        in_specs=[
            pl.BlockSpec((tm, d), lambda i, j: (i, 0)),
            row_d(0),
            pl.BlockSpec((tm, d), lambda i, j: (i, 0)),
            row_d(1),
            pl.BlockSpec((tm, tn), lambda i, j: (i, 2 * nj + j)),
            pl.BlockSpec((tm, tn), lambda i, j: (i, 3 * nj + j)),
            pl.BlockSpec((1, d), lambda i, j: (0, 0)),
            pl.BlockSpec((d, tn), lambda i, j: (0, j)),
            pl.BlockSpec((d, tn), lambda i, j: (0, j)),
        ],
        out_specs=pl.BlockSpec((tm, tn), lambda i, j: (i, j)),
        out_shape=jax.ShapeDtypeStruct((m, n), BF16),
        scratch_shapes=[pltpu.VMEM((tm, d), BF16), pltpu.VMEM((tm, d), BF16)],
        compiler_params=_params("arbitrary", "arbitrary"),
        name="merge",
    )(y, proj_tok, o, proj_tok, proj_tok, proj_tok, norm_w.reshape(1, d), w1, w2)


def _matmul_residual_body(a_ref, w_ref, r_ref, o_ref):
    o_ref[...] = r_ref[...] + _dot(a_ref[...], w_ref[...])


def _matmul_residual(a, w, res, tm_pref=512, tn_pref=2048):
    m, k = a.shape
    n = w.shape[1]
    tm = _divisor_tile(m, tm_pref, 8)
    tn = _divisor_tile(n, tn_pref, LANES)
    return pl.pallas_call(
        _matmul_residual_body,
        grid=(m // tm, n // tn),
        in_specs=[
            pl.BlockSpec((tm, k), lambda i, j: (i, 0)),
            pl.BlockSpec((k, tn), lambda i, j: (0, j)),
            pl.BlockSpec((tm, tn), lambda i, j: (i, j)),
        ],
        out_specs=pl.BlockSpec((tm, tn), lambda i, j: (i, j)),
        out_shape=jax.ShapeDtypeStruct((m, n), F32),
        compiler_params=_params("arbitrary", "arbitrary"),
        name="out_proj_residual",
    )(a, w, res)


def _top_values(x, count):
    out = []
    for _ in range(count):
        m = jnp.max(x, axis=0, keepdims=True)
        out.append(m)
        x = jnp.where(x == m, -jnp.inf, x)
    return out


def _peer_route_body(q_ref, keys_ref, e2_ref, thr_ref, f1_ref):
    k1 = keys_ref[0].astype(BF16)
    k2 = keys_ref[1].astype(BF16)
    kk = PEER_TOPK
    for h in range(PEER_HEADS):
        lo = 2 * h * PEER_D_KEY
        q1 = q_ref[:, lo:lo + PEER_D_KEY]
        q2 = q_ref[:, lo + PEER_D_KEY:lo + 2 * PEER_D_KEY]
        s1 = _dot_nt(k1, q1)
        s2 = _dot_nt(k2, q2)
        t1 = _top_values(s1, kk + 1)
        t2 = _top_values(s2, kk + 1)
        t1_blk = jnp.concatenate(t1[:kk], axis=0)
        t2_blk = jnp.concatenate(t2[:kk], axis=0)
        half = kk // 2
        cand = jnp.concatenate(
            [t1[0] + t2_blk]
            + [t1[a] + t2_blk[:half] for a in range(1, half)]
            + [t1_blk[half:] + t2[0]], axis=0)
        best = _top_values(cand, kk + 1)
        runner_up = jnp.maximum(best[kk], jnp.maximum(t1[kk] + t2[0], t1[0] + t2[kk]))
        tau = 0.5 * (best[kk - 1] + runner_up)
        m1, m2 = t1[0], t2[0]
        z = jnp.sum(jnp.where(cand >= tau, jnp.exp(cand - (m1 + m2)), 0.0), axis=0, keepdims=True)
        e2 = jnp.exp(s2 - m2)
        for s in range(e2_ref.shape[1]):
            e2_ref[h, s] = e2[:, s * LANES:(s + 1) * LANES]
        thr_ref[h] = jnp.exp(tau - s1 - m2)
        f1_ref[h] = jnp.exp(s1 - m1) / z


def _peer_route(q, keys, t_tok):
    ntok = q.shape[0]
    spec = pl.BlockSpec((PEER_HEADS, PEER_N_KEYS, t_tok), lambda t: (0, 0, t))
    shp = jax.ShapeDtypeStruct((PEER_HEADS, PEER_N_KEYS, ntok), F32)
    strip_spec = pl.BlockSpec((PEER_HEADS, t_tok // LANES, PEER_N_KEYS, LANES), lambda t: (0, t, 0, 0))
    strip_shp = jax.ShapeDtypeStruct((PEER_HEADS, ntok // LANES, PEER_N_KEYS, LANES), F32)
    return pl.pallas_call(
        _peer_route_body,
        grid=(ntok // t_tok,),
        in_specs=[
            pl.BlockSpec((t_tok, q.shape[1]), lambda t: (t, 0)),
            pl.BlockSpec(keys.shape, lambda t: (0, 0, 0)),
        ],
        out_specs=(strip_spec, spec, spec),
        out_shape=(strip_shp, shp, shp),
        compiler_params=_params("arbitrary"),
        name="peer_route",
    )(q, keys)


def _peer_expert_body(hnt_ref, e2_ref, thr_ref, f1_ref, down_ref, upt_ref, acc_ref,
                      w_ref, act_ref, p_ref):
    d, t_tok = hnt_ref.shape
    te = down_ref.shape[0]
    n_strip = t_tok // LANES
    n_part = min(PEER_MATMUL_PARTS, n_strip)
    rows_a = te // n_part
    rows_o = d // n_part
    e = pl.program_id(1)
    last = pl.num_programs(1) - 1
    slot = lax.rem(e, 2)

    def run(build, project):
        for part in range(n_part):
            if build:
                for s in range(part * n_strip // n_part, (part + 1) * n_strip // n_part):
                    ln = slice(s * LANES, (s + 1) * LANES)
                    for i in range(te // PEER_N_KEYS):
                        w = None
                        for hd in range(PEER_HEADS):
                            thr = thr_ref[hd, i:i + 1, ln]
                            f1 = f1_ref[hd, i:i + 1, ln]
                            e2 = e2_ref[hd, s]
                            term = f1 * jnp.where(e2 >= thr, e2, 0.0)
                            w = term if w is None else w + term
                        w_ref[i * PEER_N_KEYS:(i + 1) * PEER_N_KEYS, ln] = w
                ra = slice(part * rows_a, (part + 1) * rows_a)
                a = _dot(down_ref[ra, :], hnt_ref[...])
                act_ref[ra, :] = 0.5 * a * (1.0 + lax.erf(a * (2.0 ** -0.5)))
            if project:
                ro = slice(part * rows_o, (part + 1) * rows_o)
                acc_ref[ro, :] += _dot(upt_ref[ro, :], p_ref[1 - slot])
        if build:
            p_ref[slot] = (w_ref[...] * act_ref[...]).astype(BF16)

    @pl.when(e == 0)
    def _():
        acc_ref[...] = jnp.zeros_like(acc_ref)
        run(build=True, project=False)

    @pl.when(jnp.logical_and(e > 0, e < last))
    def _():
        run(build=True, project=True)

    @pl.when(e == last)
    def _():
        run(build=False, project=True)


def _peer_experts(hn_t, e2, thr, f1, down, up_t, t_tok, keys_per_block):
    d, ntok = hn_t.shape
    n_exp = down.shape[0]
    te = keys_per_block * PEER_N_KEYS
    n_blk = n_exp // te
    cur = lambda e: jnp.minimum(e, n_blk - 1)
    prev = lambda e: jnp.maximum(e - 1, 0)
    full = pl.BlockSpec((PEER_HEADS, t_tok // LANES, PEER_N_KEYS, LANES), lambda t, e: (0, t, 0, 0))
    part = pl.BlockSpec((PEER_HEADS, keys_per_block, t_tok), lambda t, e: (0, cur(e), t))
    return pl.pallas_call(
        _peer_expert_body,
        grid=(ntok // t_tok, n_blk + 1),
        in_specs=[
            pl.BlockSpec((d, t_tok), lambda t, e: (0, t)),
            full, part, part,
            pl.BlockSpec((te, d), lambda t, e: (cur(e), 0)),
            pl.BlockSpec((d, te), lambda t, e: (0, prev(e))),
        ],
        out_specs=pl.BlockSpec((d, t_tok), lambda t, e: (0, t)),
        out_shape=jax.ShapeDtypeStruct((d, ntok), F32),
        scratch_shapes=[
            pltpu.VMEM((te, t_tok), F32),
            pltpu.VMEM((te, t_tok), F32),
            pltpu.VMEM((2, te, t_tok), BF16),
        ],
        compiler_params=_params("arbitrary", "arbitrary"),
        name="peer_experts",
    )(hn_t, e2, thr, f1, down, up_t)


def _residual_norm_body(h_ref, acc_ref, g_ref, o_ref):
    y = h_ref[...] + acc_ref[...].T
    ms = jnp.mean(y * y, axis=-1, keepdims=True)
    o_ref[...] = y * lax.rsqrt(ms + NORM_EPS) * g_ref[...]


def _residual_norm(h2, acc_t, gain, tm_pref=256):
    m, d = h2.shape
    tm = _divisor_tile(m, tm_pref, LANES)
    return pl.pallas_call(
        _residual_norm_body,
        grid=(m // tm,),
        in_specs=[
            pl.BlockSpec((tm, d), lambda i: (i, 0)),
            pl.BlockSpec((d, tm), lambda i: (0, i)),
            pl.BlockSpec((1, d), lambda i: (0, 0)),
        ],
        out_specs=pl.BlockSpec((tm, d), lambda i: (i, 0)),
        out_shape=jax.ShapeDtypeStruct((m, d), F32),
        compiler_params=_params("arbitrary"),
        name="residual_norm",
    )(h2, acc_t, gain.reshape(1, d))


def _cast_body(x_ref, o_ref, *, transpose):
    x = x_ref[...]
    o_ref[...] = (x.T if transpose else x).astype(o_ref.dtype)


def _cast_bf16(w, rows_pref, transpose=False, row_multiple=LANES):
    r, c = w.shape
    tr = _divisor_tile(r, rows_pref, row_multiple)
    out_spec = pl.BlockSpec((c, tr), lambda i: (0, i)) if transpose else pl.BlockSpec((tr, c), lambda i: (i, 0))
    return pl.pallas_call(
        functools.partial(_cast_body, transpose=transpose),
        grid=(r // tr,),
        in_specs=[pl.BlockSpec((tr, c), lambda i: (i, 0))],
        out_specs=out_spec,
        out_shape=jax.ShapeDtypeStruct((c, r) if transpose else (r, c), BF16),
        compiler_params=_params("arbitrary"),
        name="cast_bf16",
    )(w)


def _w_in_segments(d):
    widths = (("z", SSD_D_INNER), ("xbc", SSD_D_INNER + 2 * SSD_BC), ("dt", SSD_HEADS),
              ("qkv", 3 * GDN_DQK), ("gate", GDN_HEADS * GDN_DV), ("a", GDN_HEADS),
              ("beta", GDN_HEADS), ("merge", 2 * d))
    seg, off = {}, 0
    for name, wd in widths:
        seg[name] = slice(off, off + wd)
        off += wd
    return seg


def _row_segments(seg, names):
    return tuple((seg[n].start, seg[n].stop - seg[n].start) for n in names)


def _lane_vec(*parts):
    v = jnp.concatenate([p.astype(F32) for p in parts])
    return jnp.pad(v, (0, LANES - v.shape[0])).reshape(1, LANES)


def _mixer(x, meta_tokens, norm_w, w_in, ssd_conv_w, ssd_conv_b, ssd_dt_bias, ssd_A_log, ssd_D,
           ssd_norm_w, gdn_conv_w, gdn_dt_bias, gdn_A_log, gdn_norm_w, w_branch_ssd, w_branch_gdn, w_out):
    b, seq, d = x.shape
    pad = CHUNK - N_META
    lp = seq + CHUNK
    nc = lp // CHUNK
    seg = _w_in_segments(d)
    w16_t = _cast_bf16(jnp.swapaxes(w_in, 0, 1), rows_pref=512, row_multiple=64)
    small = jnp.concatenate([w16_t[seg[n]] for n in ("dt", "a", "beta")], axis=0)
    w_small = jnp.pad(small, ((0, LANES - small.shape[0]), (0, 0)))

    meta = jnp.broadcast_to(meta_tokens[None].astype(x.dtype), (b, N_META, d))
    hp = jnp.concatenate([jnp.zeros((b, pad, d), x.dtype), meta, x], axis=1).reshape(b * lp, d)
    x2d = x.reshape(b * seq, d)

    proj_scan = _norm_matmul(hp, norm_w, w16_t, tm_pref=1056,
                             w_row_segments=_row_segments(seg, ("qkv", "xbc"))).reshape(b, lp, -1)
    raw_small = _norm_matmul(hp, norm_w, w_small, tm_pref=1056, out_dtype=F32,
                             w_row_segments=((0, LANES),)).reshape(b, lp, LANES)
    proj_tok = _norm_matmul(x2d, norm_w, w16_t, tm_pref=1024,
                            w_row_segments=_row_segments(seg, ("z", "gate", "merge")))

    zeros_h = jnp.zeros((GDN_HEADS,), F32)
    bias_vec = _lane_vec(ssd_dt_bias, gdn_dt_bias, zeros_h)
    alog_vec = _lane_vec(ssd_A_log, gdn_A_log, zeros_h)
    sp, cs, bt = _gates(raw_small, bias_vec, alog_vec, pad)
    cs_c = cs.reshape(b, nc, CHUNK, LANES)
    ssd_rows = jnp.swapaxes(cs_c[..., :SSD_HEADS], 2, 3)
    gdn_rows = jnp.swapaxes(cs_c[..., SSD_HEADS:SSD_HEADS + GDN_HEADS], 2, 3)

    d_vec = jnp.repeat(ssd_D.astype(F32), SSD_HEAD_DIM).reshape(1, SSD_D_INNER)
    y = _ssd(proj_scan, ssd_conv_w, ssd_conv_b, sp, cs, ssd_rows, d_vec, seq, pad)
    o = _gdn(proj_scan, gdn_conv_w, cs, bt, gdn_rows, gdn_norm_w, seq, pad)

    m = _merge(y.reshape(b * seq, -1), o.reshape(b * seq, -1), proj_tok, ssd_norm_w,
               w_branch_ssd.astype(BF16), w_branch_gdn.astype(BF16))
    return _matmul_residual(m, w_out.astype(BF16), x2d)


def _peer(h2, norm_w, w_query, sub_keys, expert_down, expert_up, final_w):
    ntok = h2.shape[0]
    q, hn_t = _norm_matmul(h2, norm_w, w_query.astype(BF16), tm_pref=512, tn_pref=2048,
                           with_transposed=True)
    e2, thr, f1 = _peer_route(q, sub_keys, _divisor_tile(ntok, 256, LANES))
    acc_t = _peer_experts(hn_t, e2, thr, f1, _cast_bf16(expert_down, rows_pref=1024),
                          _cast_bf16(expert_up, rows_pref=512, transpose=True),
                          _divisor_tile(ntok, 512, LANES), keys_per_block=8)
    return _residual_norm(h2, acc_t, final_w)


def kernel(x, meta_tokens, norm_mix_w, w_in, ssd_conv_w, ssd_conv_b, ssd_dt_bias, ssd_A_log, ssd_D, ssd_norm_w, gdn_conv_w, gdn_dt_bias, gdn_A_log, gdn_norm_w, w_branch_ssd, w_branch_gdn, w_out, norm_ffn_w, peer_w_query, peer_sub_keys, peer_expert_down, peer_expert_up, final_norm_w):
    b, seq, d = x.shape
    assert norm_mix_w.shape[0] == 1, "single-layer trunk"
    h2 = _mixer(x, meta_tokens, norm_mix_w[0], w_in[0], ssd_conv_w[0], ssd_conv_b[0], ssd_dt_bias[0],
                ssd_A_log[0], ssd_D[0], ssd_norm_w[0], gdn_conv_w[0], gdn_dt_bias[0], gdn_A_log[0],
                gdn_norm_w[0], w_branch_ssd[0], w_branch_gdn[0], w_out[0])
    out = _peer(h2, norm_ffn_w[0], peer_w_query[0], peer_sub_keys[0], peer_expert_down[0],
                peer_expert_up[0], final_norm_w)
    return out.reshape(b, seq, d)
```

```python
import functools

import jax
import jax.numpy as jnp
from jax import lax
from jax.experimental import pallas as pl
from jax.experimental.pallas import tpu as pltpu

F32 = jnp.float32
BF16 = jnp.bfloat16

N_META = 16
CHUNK = 64
CONV_K = 4
NORM_EPS = 1e-6
L2_EPS = 1e-6

SSD_HEADS = 32
SSD_HEAD_DIM = 64
SSD_GROUPS = 4
SSD_STATE = 128
SSD_HPG = SSD_HEADS // SSD_GROUPS
SSD_D_INNER = SSD_HEADS * SSD_HEAD_DIM
SSD_BC = SSD_GROUPS * SSD_STATE

GDN_HEADS = 16
GDN_DK = 128
GDN_DV = 128
GDN_DQK = GDN_HEADS * GDN_DK

PEER_HEADS = 8
PEER_N_KEYS = 128
PEER_TOPK = 16
PEER_D_KEY = 128
PEER_MATMUL_PARTS = 2

CONV_HALO = 16
LANES = 128
VMEM_LIMIT = 56 * 1024 * 1024

_NT = (((1,), (1,)), ((), ()))
_TN = (((0,), (0,)), ((), ()))


def _dot(a, b):
    return jnp.dot(a, b, preferred_element_type=F32)


def _dot_nt(a, b):
    return lax.dot_general(a, b, _NT, preferred_element_type=F32)


def _dot_tn(a, b):
    return lax.dot_general(a, b, _TN, preferred_element_type=F32)


def _silu(x):
    return x / (1.0 + jnp.exp(-x))


def _sigmoid(x):
    return 1.0 / (1.0 + jnp.exp(-x))


def _softplus(x):
    return jnp.maximum(x, 0.0) + jnp.log(1.0 + jnp.exp(-jnp.abs(x)))


def _divisor_tile(n, preferred, multiple):
    t = min(preferred, n)
    t -= t % multiple
    while t >= multiple:
        if n % t == 0:
            return t
        t -= multiple
    return n


def _params(*sem):
    return pltpu.CompilerParams(dimension_semantics=sem, vmem_limit_bytes=VMEM_LIMIT)


def _norm_matmul_body(x_ref, g_ref, w_ref, o_ref, *rest, w_is_transposed):
    n_ref = rest[-1]
    nt_ref = rest[0] if len(rest) == 2 else None

    @pl.when(pl.program_id(1) == 0)
    def _():
        x = x_ref[...]
        ms = jnp.mean(x * x, axis=-1, keepdims=True)
        n = x * lax.rsqrt(ms + NORM_EPS) * g_ref[...]
        n_ref[...] = n.astype(BF16)
        if nt_ref is not None:
            nt_ref[...] = n.T.astype(BF16)

    dot = _dot_nt if w_is_transposed else _dot
    o_ref[...] = dot(n_ref[...], w_ref[...]).astype(o_ref.dtype)


def _segment_row(j, segments, tn):
    unit = 16
    row, first_blk = None, 0
    for start, length in segments:
        assert start % unit == 0 and tn % unit == 0
        cand = start // unit + (j - first_blk) * (tn // unit)
        row = cand if row is None else jnp.where(j >= first_blk, cand, row)
        first_blk += length // tn
    return row * unit


def _norm_matmul(x2d, gain, w, tm_pref=768, tn_pref=1024, out_dtype=BF16, with_transposed=False,
                 w_row_segments=None):
    m, k = x2d.shape
    w_is_transposed = w_row_segments is not None
    n = sum(ln for _, ln in w_row_segments) if w_is_transposed else w.shape[1]
    tm = _divisor_tile(m, tm_pref, LANES if with_transposed else 16)
    tn = _divisor_tile(n, tn_pref, LANES)
    if w_is_transposed:
        assert all(ln % tn == 0 for _, ln in w_row_segments)
        w_spec = pl.BlockSpec((pl.Element(tn), pl.Element(k)),
                              lambda i, j: (_segment_row(j, w_row_segments, tn), 0))
    else:
        w_spec = pl.BlockSpec((k, tn), lambda i, j: (0, j))
    out_specs = pl.BlockSpec((tm, tn), lambda i, j: (i, j))
    out_shape = jax.ShapeDtypeStruct((m, n), out_dtype)
    if with_transposed:
        out_specs = (out_specs, pl.BlockSpec((k, tm), lambda i, j: (0, i)))
        out_shape = (out_shape, jax.ShapeDtypeStruct((k, m), BF16))
    return pl.pallas_call(
        functools.partial(_norm_matmul_body, w_is_transposed=w_is_transposed),
        grid=(m // tm, n // tn),
        in_specs=[
            pl.BlockSpec((tm, k), lambda i, j: (i, 0)),
            pl.BlockSpec((1, k), lambda i, j: (0, 0)),
            w_spec,
        ],
        out_specs=out_specs,
        out_shape=out_shape,
        scratch_shapes=[pltpu.VMEM((tm, k), BF16)],
        compiler_params=_params("arbitrary", "arbitrary"),
        name="norm_matmul",
    )(x2d, gain.reshape(1, k), w)


def _gates_body(raw_ref, bias_ref, alog_ref, sp_ref, cs_ref, bt_ref, *, n_chunks, pad):
    ri = lax.broadcasted_iota(jnp.int32, (CHUNK, CHUNK), 0)
    ci = lax.broadcasted_iota(jnp.int32, (CHUNK, CHUNK), 1)
    tri = (ri >= ci).astype(F32)
    neg_a = -jnp.exp(alog_ref[...])
    bias = bias_ref[...]
    row = lax.broadcasted_iota(jnp.int32, (CHUNK, 1), 0)

    def body(c, carry):
        sl = pl.ds(pl.multiple_of(c * CHUNK, CHUNK), CHUNK)
        r = raw_ref[0, sl, :]
        sp = _softplus(r + bias)
        cs = jnp.dot(tri, sp * neg_a, precision=lax.Precision.HIGHEST, preferred_element_type=F32)
        valid = (row + c * CHUNK) >= pad
        sp_ref[0, sl, :] = sp
        cs_ref[0, sl, :] = cs
        bt_ref[0, sl, :] = jnp.where(valid, _sigmoid(r), 0.0)
        return carry

    lax.fori_loop(0, n_chunks, body, 0)


def _gates(raw, bias_vec, alog_vec, pad):
    b, lp, w = raw.shape
    spec = pl.BlockSpec((1, lp, w), lambda i: (i, 0, 0))
    vec = pl.BlockSpec((1, w), lambda i: (0, 0))
    shp = jax.ShapeDtypeStruct((b, lp, w), F32)
    return pl.pallas_call(
        functools.partial(_gates_body, n_chunks=lp // CHUNK, pad=pad),
        grid=(b,),
        in_specs=[spec, vec, vec],
        out_specs=(spec, spec, spec),
        out_shape=(shp, shp, shp),
        compiler_params=_params("arbitrary"),
        name="gates",
    )(raw, bias_vec, alog_vec)


def _shift_matrix():
    r = jnp.arange((CONV_K - 1) * CHUNK)
    src = CONV_HALO + r % CHUNK - (CONV_K - 1 - r // CHUNK)
    return (jnp.arange(CONV_HALO + CHUNK)[None, :] == src[:, None]).astype(BF16)


def _shifted_taps(shift_ref, xpad_ref, taps_ref):
    taps_ref[...] = _dot(shift_ref[...], xpad_ref[...])


def _causal_conv(taps_ref, xpad_ref, w_ref, lo, width, bias):
    cols = slice(lo, lo + width)
    acc = w_ref[CONV_K - 1:CONV_K, cols] * xpad_ref[CONV_HALO:CONV_HALO + CHUNK, cols].astype(F32)
    if bias is not None:
        acc = acc + bias
    for j in range(CONV_K - 1):
        acc = acc + w_ref[j:j + 1, cols] * taps_ref[j * CHUNK:(j + 1) * CHUNK, cols]
    return acc


def _scan_init(st_ref, xpad_ref):
    st_ref[...] = jnp.zeros_like(st_ref)
    xpad_ref[:, 0:CONV_HALO, :] = jnp.zeros((xpad_ref.shape[0], CONV_HALO, xpad_ref.shape[2]), BF16)


def _ssd_body(x_ref, bc_ref, cw_ref, cb_ref, sp_ref, cs_ref, csr_ref, dvec_ref, expand_ref, shift_ref,
              y_ref, xpad_ref, taps_ref, st_ref, *, pad, do_init=True):
    c = pl.program_id(1)
    n_seq = x_ref.shape[0]

    if do_init:
        @pl.when(c == 0)
        def _():
            _scan_init(st_ref, xpad_ref)

    for bi in range(n_seq):
        xpad_ref[bi, CONV_HALO:CONV_HALO + CHUNK, 0:SSD_D_INNER] = x_ref[bi]
        xpad_ref[bi, CONV_HALO:CONV_HALO + CHUNK, SSD_D_INNER:] = bc_ref[bi]
        _shifted_taps(shift_ref, xpad_ref.at[bi], taps_ref.at[bi])

    row = lax.broadcasted_iota(jnp.int32, (CHUNK, 1), 0)
    valid = (row + c * CHUNK) >= pad
    ri = lax.broadcasted_iota(jnp.int32, (CHUNK, CHUNK), 0)
    ci = lax.broadcasted_iota(jnp.int32, (CHUNK, CHUNK), 1)
    tril = ri >= ci

    def conv(bi, lo, width):
        a = _causal_conv(taps_ref.at[bi], xpad_ref.at[bi], cw_ref, lo, width, cb_ref[:, lo:lo + width])
        return jnp.where(valid, _silu(a), 0.0)

    gw = SSD_HPG * SSD_HEAD_DIM
    seqs = range(n_seq)
    gunits = [(bi, g) for bi in seqs for g in range(SSD_GROUPS)]
    groups = range(len(gunits))
    xg = [conv(bi, g * gw, gw) for bi, g in gunits]
    bg = [conv(bi, SSD_D_INNER + g * SSD_STATE, SSD_STATE).astype(BF16) for bi, g in gunits]
    cg = [conv(bi, SSD_D_INNER + SSD_BC + g * SSD_STATE, SSD_STATE).astype(BF16) for bi, g in gunits]
    cb = [_dot_nt(cg[u], bg[u]) for u in groups]
    st = [st_ref[bi, g] for bi, g in gunits]
    yoff = [_dot(cg[u], st[u].astype(BF16)) for u in groups]

    e_dt, e_w, e_cs = [], [], []
    for bi in seqs:
        sp, cs = sp_ref[bi], cs_ref[bi]
        factors = (sp, sp * jnp.exp(cs[CHUNK - 1:CHUNK, :] - cs), jnp.exp(cs))
        pieces = []
        for f in factors:
            hi = f.astype(BF16)
            pieces += [hi, (f - hi.astype(F32)).astype(BF16)]
        spread = _dot(jnp.concatenate(pieces, axis=0), expand_ref[...])
        for k, dst in enumerate((e_dt, e_w, e_cs)):
            dst.append(spread[(2 * k) * CHUNK:(2 * k + 1) * CHUNK] + spread[(2 * k + 1) * CHUNK:(2 * k + 2) * CHUNK])

    xdt = [(xg[u] * e_dt[bi][:, g * gw:(g + 1) * gw]).astype(BF16) for u, (bi, g) in enumerate(gunits)]
    xw = [(xg[u] * e_w[bi][:, g * gw:(g + 1) * gw]).astype(BF16) for u, (bi, g) in enumerate(gunits)]
    w_intra, xdt_head = [], []
    for bi in seqs:
        cs, csr = cs_ref[bi], csr_ref[bi, 0]
        for h in range(SSD_HEADS):
            g, r = divmod(h, SSD_HPG)
            decay = jnp.where(tril, jnp.exp(cs[:, h:h + 1] - csr[h:h + 1, :]), 0.0)
            w_intra.append((cb[bi * SSD_GROUPS + g] * decay).astype(BF16))
            xdt_head.append(xdt[bi * SSD_GROUPS + g][:, r * SSD_HEAD_DIM:(r + 1) * SSD_HEAD_DIM])
    ydiag = [_dot(w, x) for w, x in zip(w_intra, xdt_head)]
    snew = [_dot_tn(bg[u], xw[u]) for u in groups]
    for u, (bi, g) in enumerate(gunits):
        sl = slice(g * gw, (g + 1) * gw)
        st_ref[bi, g] = st[u] * e_cs[bi][CHUNK - 1:CHUNK, sl] + snew[u]
        y_g = (jnp.concatenate(ydiag[u * SSD_HPG:(u + 1) * SSD_HPG], axis=1) + yoff[u] * e_cs[bi][:, sl]
               + dvec_ref[:, sl] * xg[u])
        y_ref[bi, :, sl] = y_g.astype(y_ref.dtype)

    xpad_ref[:, 0:CONV_HALO, :] = xpad_ref[:, CHUNK:CHUNK + CONV_HALO, :]


def _ssd_call_parts(proj, conv_w, conv_b, sp, cs, cs_rows, d_vec, seq, n_seq):
    b = proj.shape[0]
    conv_dim = SSD_D_INNER + 2 * SSD_BC
    x_blk = 3 * GDN_DQK // SSD_D_INNER
    bc_blk = (3 * GDN_DQK + SSD_D_INNER) // (2 * SSD_BC)
    expand = (jnp.arange(LANES)[:, None] == jnp.arange(SSD_D_INNER)[None, :] // SSD_HEAD_DIM).astype(BF16)
    return dict(
        in_specs=[
            pl.BlockSpec((n_seq, CHUNK, SSD_D_INNER), lambda i, c: (i, c, x_blk)),
            pl.BlockSpec((n_seq, CHUNK, 2 * SSD_BC), lambda i, c: (i, c, bc_blk)),
            pl.BlockSpec((CONV_K, conv_dim), lambda i, c: (0, 0)),
            pl.BlockSpec((1, conv_dim), lambda i, c: (0, 0)),
            pl.BlockSpec((n_seq, CHUNK, LANES), lambda i, c: (i, c, 0)),
            pl.BlockSpec((n_seq, CHUNK, LANES), lambda i, c: (i, c, 0)),
            pl.BlockSpec((n_seq, 1, SSD_HEADS, CHUNK), lambda i, c: (i, c, 0, 0)),
            pl.BlockSpec((1, SSD_D_INNER), lambda i, c: (0, 0)),
            pl.BlockSpec((LANES, SSD_D_INNER), lambda i, c: (0, 0)),
            pl.BlockSpec(((CONV_K - 1) * CHUNK, CONV_HALO + CHUNK), lambda i, c: (0, 0)),
        ],
        out_spec=pl.BlockSpec((n_seq, CHUNK, SSD_D_INNER), lambda i, c: (i, jnp.maximum(c - 1, 0), 0)),
        out_shape=jax.ShapeDtypeStruct((b, seq, SSD_D_INNER), BF16),
        scratch=[
            pltpu.VMEM((n_seq, CONV_HALO + CHUNK, conv_dim), BF16),
            pltpu.VMEM((n_seq, (CONV_K - 1) * CHUNK, conv_dim), F32),
            pltpu.VMEM((n_seq, SSD_GROUPS, SSD_STATE, SSD_HPG * SSD_HEAD_DIM), F32),
        ],
        args=(proj, proj, conv_w, conv_b.reshape(1, conv_dim), sp, cs, cs_rows, d_vec, expand, _shift_matrix()),
    )


def _unit_lower_inverses(strict_ls, eye):
    ps = [l.astype(BF16) for l in strict_ls]
    ts = [eye - l for l in strict_ls]
    n = 2
    while n < CHUNK:
        p2 = [_dot(p, p).astype(BF16) for p in ps]
        ts = [t + _dot(t.astype(BF16), q) for t, q in zip(ts, p2)]
        ps = p2
        n *= 2
    return ts


def _gdn_body(q_ref, k_ref, v_ref, cw_ref, cs_ref, bt_ref, gr_ref, nw_ref, shift_ref, o_ref,
              xpad_ref, taps_ref, st_ref, *, pad, do_init=True):
    c = pl.program_id(1)
    n_seq = q_ref.shape[0]

    if do_init:
        @pl.when(c == 0)
        def _():
            _scan_init(st_ref, xpad_ref)

    for bi in range(n_seq):
        xpad_ref[bi, CONV_HALO:CONV_HALO + CHUNK, 0:GDN_DQK] = q_ref[bi]
        xpad_ref[bi, CONV_HALO:CONV_HALO + CHUNK, GDN_DQK:2 * GDN_DQK] = k_ref[bi]
        xpad_ref[bi, CONV_HALO:CONV_HALO + CHUNK, 2 * GDN_DQK:] = v_ref[bi]
        _shifted_taps(shift_ref, xpad_ref.at[bi], taps_ref.at[bi])

    row = lax.broadcasted_iota(jnp.int32, (CHUNK, 1), 0)
    valid = (row + c * CHUNK) >= pad
    ri = lax.broadcasted_iota(jnp.int32, (CHUNK, CHUNK), 0)
    ci = lax.broadcasted_iota(jnp.int32, (CHUNK, CHUNK), 1)
    tril = ri >= ci
    strict = ri > ci
    eye = (ri == ci).astype(F32)

    nw = nw_ref[...]
    g_off = SSD_HEADS
    b_off = SSD_HEADS + GDN_HEADS

    def conv(bi, lo):
        a = _causal_conv(taps_ref.at[bi], xpad_ref.at[bi], cw_ref, lo, GDN_DK, None)
        return jnp.where(valid, _silu(a), 0.0)

    units = [(bi, h) for bi in range(n_seq) for h in range(GDN_HEADS)]
    heads = range(len(units))
    q16, k16, kb16, rhs16, qg, kd16, decay, elast = [], [], [], [], [], [], [], []
    for bi, h in units:
        cs = cs_ref[bi]
        bt = bt_ref[bi]
        gr = gr_ref[bi, 0]
        q = conv(bi, h * GDN_DK)
        k = conv(bi, GDN_DQK + h * GDN_DK)
        v = conv(bi, 2 * GDN_DQK + h * GDN_DV)
        q = q * lax.rsqrt(jnp.sum(q * q, axis=-1, keepdims=True) + L2_EPS) * (GDN_DK ** -0.5)
        k = k * lax.rsqrt(jnp.sum(k * k, axis=-1, keepdims=True) + L2_EPS)
        col = cs[:, g_off + h:g_off + h + 1]
        rowv = gr[h:h + 1, :]
        last = rowv[:, CHUNK - 1:CHUNK]
        beta = bt[:, b_off + h:b_off + h + 1]
        ecol = jnp.exp(col)
        kb = k * beta
        decay.append(jnp.where(tril, jnp.exp(col - rowv), 0.0))
        q16.append(q.astype(BF16))
        k16.append(k.astype(BF16))
        kb16.append(kb.astype(BF16))
        rhs16.append(jnp.concatenate([v * beta, kb * ecol], axis=1).astype(BF16))
        qg.append(q * ecol)
        kd16.append((k * jnp.exp(last - col)).astype(BF16))
        elast.append(jnp.exp(last))
    kk = [_dot_nt(kb16[h], k16[h]) for h in heads]
    qk = [_dot_nt(q16[h], k16[h]) for h in heads]
    tinv = _unit_lower_inverses([jnp.where(strict, kk[h] * decay[h], 0.0) for h in heads], eye)
    uw = [_dot(tinv[h].astype(BF16), rhs16[h]) for h in heads]
    s = [st_ref[bi, h] for bi, h in units]
    ws = [_dot(jnp.concatenate([uw[h][:, GDN_DV:], qg[h]], axis=0).astype(BF16), s[h].astype(BF16))
          for h in heads]
    v16 = [(uw[h][:, :GDN_DV] - ws[h][:CHUNK]).astype(BF16) for h in heads]
    o_intra = [_dot((qk[h] * decay[h]).astype(BF16), v16[h]) for h in heads]
    s_add = [_dot_tn(kd16[h], v16[h]) for h in heads]
    for u, (bi, h) in enumerate(units):
        st_ref[bi, h] = s[u] * elast[u] + s_add[u]
        o = ws[u][CHUNK:] + o_intra[u]
        o = o * lax.rsqrt(jnp.mean(o * o, axis=-1, keepdims=True) + NORM_EPS) * nw
        o_ref[bi, :, h * GDN_DV:(h + 1) * GDN_DV] = o.astype(o_ref.dtype)

    xpad_ref[:, 0:CONV_HALO, :] = xpad_ref[:, CHUNK:CHUNK + CONV_HALO, :]


def _gdn_call_parts(proj, conv_w, cs, bt, g_rows, norm_w, seq, n_seq):
    b = proj.shape[0]
    conv_dim = 3 * GDN_DQK
    qkv = lambda blk: pl.BlockSpec((n_seq, CHUNK, GDN_DQK), lambda i, c: (i, c, blk))
    return dict(
        in_specs=[
            qkv(0), qkv(1), qkv(2),
            pl.BlockSpec((CONV_K, conv_dim), lambda i, c: (0, 0)),
            pl.BlockSpec((n_seq, CHUNK, LANES), lambda i, c: (i, c, 0)),
            pl.BlockSpec((n_seq, CHUNK, LANES), lambda i, c: (i, c, 0)),
            pl.BlockSpec((n_seq, 1, GDN_HEADS, CHUNK), lambda i, c: (i, c, 0, 0)),
            pl.BlockSpec((1, GDN_DV), lambda i, c: (0, 0)),
            pl.BlockSpec(((CONV_K - 1) * CHUNK, CONV_HALO + CHUNK), lambda i, c: (0, 0)),
        ],
        out_spec=pl.BlockSpec((n_seq, CHUNK, GDN_DQK), lambda i, c: (i, jnp.maximum(c - 1, 0), 0)),
        out_shape=jax.ShapeDtypeStruct((b, seq, GDN_HEADS * GDN_DV), BF16),
        scratch=[
            pltpu.VMEM((n_seq, CONV_HALO + CHUNK, conv_dim), BF16),
            pltpu.VMEM((n_seq, (CONV_K - 1) * CHUNK, conv_dim), F32),
            pltpu.VMEM((n_seq, GDN_HEADS, GDN_DK, GDN_DV), F32),
        ],
        args=(proj, proj, proj, conv_w, cs, bt, g_rows, norm_w.reshape(1, GDN_DV), _shift_matrix()),
    )


def _scans_body(*refs, n_ssd_in, n_gdn_in, pad):
    ssd_in = refs[:n_ssd_in]
    gdn_in = refs[n_ssd_in:n_ssd_in + n_gdn_in]
    y_ref, o_ref = refs[n_ssd_in + n_gdn_in:n_ssd_in + n_gdn_in + 2]
    ssd_scratch = refs[n_ssd_in + n_gdn_in + 2:n_ssd_in + n_gdn_in + 5]
    gdn_scratch = refs[n_ssd_in + n_gdn_in + 5:]

    @pl.when(pl.program_id(1) == 0)
    def _():
        _scan_init(ssd_scratch[2], ssd_scratch[0])
        _scan_init(gdn_scratch[2], gdn_scratch[0])

    _ssd_body(*ssd_in, y_ref, *ssd_scratch, pad=pad, do_init=False)
    _gdn_body(*gdn_in, o_ref, *gdn_scratch, pad=pad, do_init=False)


def _scans(ssd, gdn, b, nc, n_seq, pad):
    return pl.pallas_call(
        functools.partial(_scans_body, n_ssd_in=len(ssd["in_specs"]), n_gdn_in=len(gdn["in_specs"]), pad=pad),
        grid=(b // n_seq, nc),
        in_specs=ssd["in_specs"] + gdn["in_specs"],
        out_specs=(ssd["out_spec"], gdn["out_spec"]),
        out_shape=(ssd["out_shape"], gdn["out_shape"]),
        scratch_shapes=ssd["scratch"] + gdn["scratch"],
        compiler_params=_params("arbitrary", "arbitrary"),
        name="ssd_gdn_scans",
    )(*ssd["args"], *gdn["args"])


def _merge_body(y_ref, z_ref, o_ref, gate_ref, g1_ref, g2_ref, nw_ref, w1_ref, w2_ref, m_ref,
                a1_ref, a2_ref):
    @pl.when(pl.program_id(1) == 0)
    def _():
        gw = SSD_D_INNER // SSD_GROUPS
        for g in range(SSD_GROUPS):
            sl = slice(g * gw, (g + 1) * gw)
            t = y_ref[:, sl].astype(F32) * _silu(z_ref[:, sl].astype(F32))
            ms = jnp.mean(t * t, axis=-1, keepdims=True)
            a1_ref[:, sl] = (t * lax.rsqrt(ms + NORM_EPS) * nw_ref[:, sl]).astype(BF16)
        a2_ref[...] = (o_ref[...].astype(F32) * _silu(gate_ref[...].astype(F32))).astype(BF16)

    m = (_sigmoid(g1_ref[...].astype(F32)) * _dot(a1_ref[...], w1_ref[...])
         + _sigmoid(g2_ref[...].astype(F32)) * _dot(a2_ref[...], w2_ref[...]))
    m_ref[...] = m.astype(m_ref.dtype)


def _merge(y, o, proj_tok, norm_w, w1, w2, tm_pref=256, tn_pref=2048):
    m, d = y.shape
    n = w1.shape[1]
    tm = _divisor_tile(m, tm_pref, 16)
    tn = _divisor_tile(n, tn_pref, LANES)
    nj = n // tn
    row_d = lambda blk: pl.BlockSpec((tm, d), lambda i, j: (i, blk))
    return pl.pallas_call(
        _merge_body,
        grid=(m // tm, nj),
        in_specs=[
            pl.BlockSpec((tm, d), lambda i, j: (i, 0)),
            row_d(0),
            pl.BlockSpec((tm, d), lambda i, j: (i, 0)),
            row_d(1),
            pl.BlockSpec((tm, tn), lambda i, j: (i, 2 * nj + j)),
            pl.BlockSpec((tm, tn), lambda i, j: (i, 3 * nj + j)),
            pl.BlockSpec((1, d), lambda i, j: (0, 0)),
            pl.BlockSpec((d, tn), lambda i, j: (0, j)),
            pl.BlockSpec((d, tn), lambda i, j: (0, j)),
        ],
        out_specs=pl.BlockSpec((tm, tn), lambda i, j: (i, j)),
        out_shape=jax.ShapeDtypeStruct((m, n), BF16),
        scratch_shapes=[pltpu.VMEM((tm, d), BF16), pltpu.VMEM((tm, d), BF16)],
        compiler_params=_params("arbitrary", "arbitrary"),
        name="merge",
    )(y, proj_tok, o, proj_tok, proj_tok, proj_tok, norm_w.reshape(1, d), w1, w2)


def _matmul_residual_body(a_ref, w_ref, r_ref, o_ref):
    o_ref[...] = r_ref[...] + _dot(a_ref[...], w_ref[...])


def _matmul_residual(a, w, res, tm_pref=512, tn_pref=2048):
    m, k = a.shape
    n = w.shape[1]
    tm = _divisor_tile(m, tm_pref, 8)
    tn = _divisor_tile(n, tn_pref, LANES)
    return pl.pallas_call(
        _matmul_residual_body,
        grid=(m // tm, n // tn),
        in_specs=[
            pl.BlockSpec((tm, k), lambda i, j: (i, 0)),
            pl.BlockSpec((k, tn), lambda i, j: (0, j)),
            pl.BlockSpec((tm, tn), lambda i, j: (i, j)),
        ],
        out_specs=pl.BlockSpec((tm, tn), lambda i, j: (i, j)),
        out_shape=jax.ShapeDtypeStruct((m, n), F32),
        compiler_params=_params("arbitrary", "arbitrary"),
        name="out_proj_residual",
    )(a, w, res)


def _top_values(x, count):
    out = []
    for _ in range(count):
        m = jnp.max(x, axis=0, keepdims=True)
        out.append(m)
        x = jnp.where(x == m, -jnp.inf, x)
    return out


def _peer_route_body(q_ref, keys_ref, e2_ref, thr_ref, f1_ref):
    k1 = keys_ref[0].astype(BF16)
    k2 = keys_ref[1].astype(BF16)
    kk = PEER_TOPK
    for h in range(PEER_HEADS):
        lo = 2 * h * PEER_D_KEY
        q1 = q_ref[:, lo:lo + PEER_D_KEY]
        q2 = q_ref[:, lo + PEER_D_KEY:lo + 2 * PEER_D_KEY]
        s1 = _dot_nt(k1, q1)
        s2 = _dot_nt(k2, q2)
        t1 = _top_values(s1, kk + 1)
        t2 = _top_values(s2, kk + 1)
        t1_blk = jnp.concatenate(t1[:kk], axis=0)
        t2_blk = jnp.concatenate(t2[:kk], axis=0)
        half = kk // 2
        cand = jnp.concatenate(
            [t1[0] + t2_blk]
            + [t1[a] + t2_blk[:half] for a in range(1, half)]
            + [t1_blk[half:] + t2[0]], axis=0)
        best = _top_values(cand, kk + 1)
        runner_up = jnp.maximum(best[kk], jnp.maximum(t1[kk] + t2[0], t1[0] + t2[kk]))
        tau = 0.5 * (best[kk - 1] + runner_up)
        m1, m2 = t1[0], t2[0]
        z = jnp.sum(jnp.where(cand >= tau, jnp.exp(cand - (m1 + m2)), 0.0), axis=0, keepdims=True)
        e2 = jnp.exp(s2 - m2)
        for s in range(e2_ref.shape[1]):
            e2_ref[h, s] = e2[:, s * LANES:(s + 1) * LANES]
        thr_ref[h] = jnp.exp(tau - s1 - m2)
        f1_ref[h] = jnp.exp(s1 - m1) / z


def _peer_route(q, keys, t_tok):
    ntok = q.shape[0]
    spec = pl.BlockSpec((PEER_HEADS, PEER_N_KEYS, t_tok), lambda t: (0, 0, t))
    shp = jax.ShapeDtypeStruct((PEER_HEADS, PEER_N_KEYS, ntok), F32)
    strip_spec = pl.BlockSpec((PEER_HEADS, t_tok // LANES, PEER_N_KEYS, LANES), lambda t: (0, t, 0, 0))
    strip_shp = jax.ShapeDtypeStruct((PEER_HEADS, ntok // LANES, PEER_N_KEYS, LANES), F32)
    return pl.pallas_call(
        _peer_route_body,
        grid=(ntok // t_tok,),
        in_specs=[
            pl.BlockSpec((t_tok, q.shape[1]), lambda t: (t, 0)),
            pl.BlockSpec(keys.shape, lambda t: (0, 0, 0)),
        ],
        out_specs=(strip_spec, spec, spec),
        out_shape=(strip_shp, shp, shp),
        compiler_params=_params("arbitrary"),
        name="peer_route",
    )(q, keys)


def _peer_expert_body(hnt_ref, e2_ref, thr_ref, f1_ref, down_ref, upt_ref, acc_ref,
                      w_ref, act_ref, p_ref):
    d, t_tok = hnt_ref.shape
    te = down_ref.shape[0]
    n_strip = t_tok // LANES
    n_part = min(PEER_MATMUL_PARTS, n_strip)
    rows_a = te // n_part
    rows_o = d // n_part
    e = pl.program_id(1)
    last = pl.num_programs(1) - 1
    slot = lax.rem(e, 2)

    def run(build, project):
        for part in range(n_part):
            if build:
                for s in range(part * n_strip // n_part, (part + 1) * n_strip // n_part):
                    ln = slice(s * LANES, (s + 1) * LANES)
                    for i in range(te // PEER_N_KEYS):
                        w = None
                        for hd in range(PEER_HEADS):
                            thr = thr_ref[hd, i:i + 1, ln]
                            f1 = f1_ref[hd, i:i + 1, ln]
                            e2 = e2_ref[hd, s]
                            term = f1 * jnp.where(e2 >= thr, e2, 0.0)
                            w = term if w is None else w + term
                        w_ref[i * PEER_N_KEYS:(i + 1) * PEER_N_KEYS, ln] = w
                ra = slice(part * rows_a, (part + 1) * rows_a)
                a = _dot(down_ref[ra, :], hnt_ref[...])
                act_ref[ra, :] = 0.5 * a * (1.0 + lax.erf(a * (2.0 ** -0.5)))
            if project:
                ro = slice(part * rows_o, (part + 1) * rows_o)
                acc_ref[ro, :] += _dot(upt_ref[ro, :], p_ref[1 - slot])
        if build:
            p_ref[slot] = (w_ref[...] * act_ref[...]).astype(BF16)

    @pl.when(e == 0)
    def _():
        acc_ref[...] = jnp.zeros_like(acc_ref)
        run(build=True, project=False)

    @pl.when(jnp.logical_and(e > 0, e < last))
    def _():
        run(build=True, project=True)

    @pl.when(e == last)
    def _():
        run(build=False, project=True)


def _peer_experts(hn_t, e2, thr, f1, down, up_t, t_tok, keys_per_block):
    d, ntok = hn_t.shape
    n_exp = down.shape[0]
    te = keys_per_block * PEER_N_KEYS
    n_blk = n_exp // te
    cur = lambda e: jnp.minimum(e, n_blk - 1)
    prev = lambda e: jnp.maximum(e - 1, 0)
    full = pl.BlockSpec((PEER_HEADS, t_tok // LANES, PEER_N_KEYS, LANES), lambda t, e: (0, t, 0, 0))
    part = pl.BlockSpec((PEER_HEADS, keys_per_block, t_tok), lambda t, e: (0, cur(e), t))
    return pl.pallas_call(
        _peer_expert_body,
        grid=(ntok // t_tok, n_blk + 1),
        in_specs=[
            pl.BlockSpec((d, t_tok), lambda t, e: (0, t)),
            full, part, part,
            pl.BlockSpec((te, d), lambda t, e: (cur(e), 0)),
            pl.BlockSpec((d, te), lambda t, e: (0, prev(e))),
        ],
        out_specs=pl.BlockSpec((d, t_tok), lambda t, e: (0, t)),
        out_shape=jax.ShapeDtypeStruct((d, ntok), F32),
        scratch_shapes=[
            pltpu.VMEM((te, t_tok), F32),
            pltpu.VMEM((te, t_tok), F32),
            pltpu.VMEM((2, te, t_tok), BF16),
        ],
        compiler_params=_params("arbitrary", "arbitrary"),
        name="peer_experts",
    )(hn_t, e2, thr, f1, down, up_t)


def _residual_norm_body(h_ref, acc_ref, g_ref, o_ref):
    y = h_ref[...] + acc_ref[...].T
    ms = jnp.mean(y * y, axis=-1, keepdims=True)
    o_ref[...] = y * lax.rsqrt(ms + NORM_EPS) * g_ref[...]


def _residual_norm(h2, acc_t, gain, tm_pref=256):
    m, d = h2.shape
    tm = _divisor_tile(m, tm_pref, LANES)
    return pl.pallas_call(
        _residual_norm_body,
        grid=(m // tm,),
        in_specs=[
            pl.BlockSpec((tm, d), lambda i: (i, 0)),
            pl.BlockSpec((d, tm), lambda i: (0, i)),
            pl.BlockSpec((1, d), lambda i: (0, 0)),
        ],
        out_specs=pl.BlockSpec((tm, d), lambda i: (i, 0)),
        out_shape=jax.ShapeDtypeStruct((m, d), F32),
        compiler_params=_params("arbitrary"),
        name="residual_norm",
    )(h2, acc_t, gain.reshape(1, d))


def _cast_body(x_ref, o_ref, *, transpose):
    x = x_ref[...]
    o_ref[...] = (x.T if transpose else x).astype(o_ref.dtype)


def _cast_bf16(w, rows_pref, transpose=False, row_multiple=LANES):
    r, c = w.shape
    tr = _divisor_tile(r, rows_pref, row_multiple)
    out_spec = pl.BlockSpec((c, tr), lambda i: (0, i)) if transpose else pl.BlockSpec((tr, c), lambda i: (i, 0))
    return pl.pallas_call(
        functools.partial(_cast_body, transpose=transpose),
        grid=(r // tr,),
        in_specs=[pl.BlockSpec((tr, c), lambda i: (i, 0))],
        out_specs=out_spec,
        out_shape=jax.ShapeDtypeStruct((c, r) if transpose else (r, c), BF16),
        compiler_params=_params("arbitrary"),
        name="cast_bf16",
    )(w)


def _w_in_segments(d):
    widths = (("z", SSD_D_INNER), ("xbc", SSD_D_INNER + 2 * SSD_BC), ("dt", SSD_HEADS),
              ("qkv", 3 * GDN_DQK), ("gate", GDN_HEADS * GDN_DV), ("a", GDN_HEADS),
              ("beta", GDN_HEADS), ("merge", 2 * d))
    seg, off = {}, 0
    for name, wd in widths:
        seg[name] = slice(off, off + wd)
        off += wd
    return seg


def _row_segments(seg, names):
    return tuple((seg[n].start, seg[n].stop - seg[n].start) for n in names)


def _lane_vec(*parts):
    v = jnp.concatenate([p.astype(F32) for p in parts])
    return jnp.pad(v, (0, LANES - v.shape[0])).reshape(1, LANES)


def _mixer(x, meta_tokens, norm_w, w_in, ssd_conv_w, ssd_conv_b, ssd_dt_bias, ssd_A_log, ssd_D,
           ssd_norm_w, gdn_conv_w, gdn_dt_bias, gdn_A_log, gdn_norm_w, w_branch_ssd, w_branch_gdn, w_out):
    b, seq, d = x.shape
    pad = CHUNK - N_META
    lp = seq + CHUNK
    nc = lp // CHUNK
    seg = _w_in_segments(d)
    w16_t = _cast_bf16(jnp.swapaxes(w_in, 0, 1), rows_pref=512, row_multiple=64)
    small = jnp.concatenate([w16_t[seg[n]] for n in ("dt", "a", "beta")], axis=0)
    w_small = jnp.pad(small, ((0, LANES - small.shape[0]), (0, 0)))

    meta = jnp.broadcast_to(meta_tokens[None].astype(x.dtype), (b, N_META, d))
    hp = jnp.concatenate([jnp.zeros((b, pad, d), x.dtype), meta, x], axis=1).reshape(b * lp, d)
    x2d = x.reshape(b * seq, d)

    proj_scan = _norm_matmul(hp, norm_w, w16_t, tm_pref=1056,
                             w_row_segments=_row_segments(seg, ("qkv", "xbc"))).reshape(b, lp, -1)
    raw_small = _norm_matmul(hp, norm_w, w_small, tm_pref=1056, out_dtype=F32,
                             w_row_segments=((0, LANES),)).reshape(b, lp, LANES)
    proj_tok = _norm_matmul(x2d, norm_w, w16_t, tm_pref=1024,
                            w_row_segments=_row_segments(seg, ("z", "gate", "merge")))

    zeros_h = jnp.zeros((GDN_HEADS,), F32)
    bias_vec = _lane_vec(ssd_dt_bias, gdn_dt_bias, zeros_h)
    alog_vec = _lane_vec(ssd_A_log, gdn_A_log, zeros_h)
    sp, cs, bt = _gates(raw_small, bias_vec, alog_vec, pad)
    cs_c = cs.reshape(b, nc, CHUNK, LANES)
    ssd_rows = jnp.swapaxes(cs_c[..., :SSD_HEADS], 2, 3)
    gdn_rows = jnp.swapaxes(cs_c[..., SSD_HEADS:SSD_HEADS + GDN_HEADS], 2, 3)

    d_vec = jnp.repeat(ssd_D.astype(F32), SSD_HEAD_DIM).reshape(1, SSD_D_INNER)
    n_seq = 2 if b % 2 == 0 else 1
    y, o = _scans(_ssd_call_parts(proj_scan, ssd_conv_w, ssd_conv_b, sp, cs, ssd_rows, d_vec, seq, n_seq),
                  _gdn_call_parts(proj_scan, gdn_conv_w, cs, bt, gdn_rows, gdn_norm_w, seq, n_seq),
                  b, nc, n_seq, pad)

    m = _merge(y.reshape(b * seq, -1), o.reshape(b * seq, -1), proj_tok, ssd_norm_w,
               w_branch_ssd.astype(BF16), w_branch_gdn.astype(BF16))
    return _matmul_residual(m, w_out.astype(BF16), x2d)


def _peer(h2, norm_w, w_query, sub_keys, expert_down, expert_up, final_w):
    ntok = h2.shape[0]
    q, hn_t = _norm_matmul(h2, norm_w, w_query.astype(BF16), tm_pref=512, tn_pref=2048,
                           with_transposed=True)
    e2, thr, f1 = _peer_route(q, sub_keys, _divisor_tile(ntok, 256, LANES))
    acc_t = _peer_experts(hn_t, e2, thr, f1, _cast_bf16(expert_down, rows_pref=1024),
                          _cast_bf16(expert_up, rows_pref=512, transpose=True),
                          _divisor_tile(ntok, 512, LANES), keys_per_block=8)
    return _residual_norm(h2, acc_t, final_w)


def kernel(x, meta_tokens, norm_mix_w, w_in, ssd_conv_w, ssd_conv_b, ssd_dt_bias, ssd_A_log, ssd_D, ssd_norm_w, gdn_conv_w, gdn_dt_bias, gdn_A_log, gdn_norm_w, w_branch_ssd, w_branch_gdn, w_out, norm_ffn_w, peer_w_query, peer_sub_keys, peer_expert_down, peer_expert_up, final_norm_w):
    b, seq, d = x.shape
    assert norm_mix_w.shape[0] == 1, "single-layer trunk"
    h2 = _mixer(x, meta_tokens, norm_mix_w[0], w_in[0], ssd_conv_w[0], ssd_conv_b[0], ssd_dt_bias[0],
                ssd_A_log[0], ssd_D[0], ssd_norm_w[0], gdn_conv_w[0], gdn_dt_bias[0], gdn_A_log[0],
                gdn_norm_w[0], w_branch_ssd[0], w_branch_gdn[0], w_out[0])
    out = _peer(h2, norm_ffn_w[0], peer_w_query[0], peer_sub_keys[0], peer_expert_down[0],
                peer_expert_up[0], final_norm_w)
    return out.reshape(b, seq, d)
```
